```python
import jax
import jax.numpy as jnp
from jax import lax
import numpy as np

D_MODEL = 2048
BATCH = 4
SEQ = 2048
DEPTH = 2

N_MEM = 256
EPS = 1e-6
NEG_BIG = -1e30
MIN_FORGET = 1e-20
HG_HEADS = 8
HG_DK = 128
HG_DV = 128
HG_CHUNK = 64
MLA_HEADS = 8
MLA_Q_RANK = 512
MLA_KV_RANK = 512
MLA_NOPE = 128
MLA_ROPE = 64
MLA_V = 128
ROPE_THETA = 10000.0
Q_BLOCK = 128
MEM_HEADS = 4
MEM_HEAD_DIM = 256
N_BRANCH = 3
BRANCH_WIDTH = HG_HEADS * HG_DV
D_IN = (2 * HG_HEADS * HG_DK + 2 * HG_HEADS * HG_DV + MLA_Q_RANK + MLA_KV_RANK
        + MLA_ROPE + MEM_HEADS * MEM_HEAD_DIM + N_BRANCH * D_MODEL)
D_FF = 5632
N_EXPERTS = 8
TOP_K = 2
D_FF_EXPERT = 7168
MOE_BLOCK = 128
N_DENSE = (DEPTH + 1) // 2
N_MOE = DEPTH // 2

kernel_name = 'hybrid_hgrn2_mla_memxattn_moe_trunk'


def rmsnorm(x, g):
    xf = x.astype(jnp.float32)
    y = xf * lax.rsqrt(jnp.mean(xf * xf, axis=-1, keepdims=True) + EPS)
    return (y * g.astype(jnp.float32)).astype(x.dtype)


def apply_rope(x, cos, sin):
    half = MLA_ROPE // 2
    x1 = x[..., :half].astype(jnp.float32)
    x2 = x[..., half:].astype(jnp.float32)
    return jnp.concatenate([x1 * cos - x2 * sin, x2 * cos + x1 * sin], axis=-1).astype(x.dtype)


def hgrn2_mix(q, f_logit, i, lb):
    B, S = q.shape[:2]
    lbf = lb.astype(jnp.float32)
    f = lbf + (1.0 - lbf) * jax.nn.sigmoid(f_logit.astype(jnp.float32))
    log_f = jnp.log(jnp.maximum(f, MIN_FORGET))
    k = 1.0 - f
    nc = S // HG_CHUNK

    def to_chunks(t, d):
        return t.astype(jnp.float32).reshape(B, nc, HG_CHUNK, HG_HEADS, d).transpose(1, 0, 3, 2, 4)

    qc, kc, vc, lfc = to_chunks(q, HG_DK), to_chunks(k, HG_DK), to_chunks(i, HG_DV), to_chunks(log_f, HG_DK)
    causal = jnp.tril(jnp.ones((HG_CHUNK, HG_CHUNK), dtype=bool))

    def step(state, inp):
        qb, kb, vb, lfb = inp
        b = jnp.cumsum(lfb, axis=2)
        o_inter = jnp.einsum('bhtd,bhde->bhte', qb * jnp.exp(b), state)
        diff = b[:, :, :, None, :] - b[:, :, None, :, :]
        decay = jnp.exp(jnp.where(causal[None, None, :, :, None], diff, NEG_BIG))
        scores = jnp.einsum('bhtd,bhsd,bhtsd->bhts', qb, kb, decay)
        o_intra = jnp.einsum('bhts,bhse->bhte', scores, vb)
        b_last = b[:, :, -1:, :]
        new_state = (jnp.exp(b_last[:, :, 0, :])[..., None] * state
                     + jnp.einsum('bhsd,bhse->bhde', kb * jnp.exp(b_last - b), vb))
        return new_state, o_inter + o_intra

    state0 = jnp.zeros((B, HG_HEADS, HG_DK, HG_DV), jnp.float32)
    _, oc = lax.scan(step, state0, (qc, kc, vc, lfc))
    return oc.transpose(1, 0, 3, 2, 4).reshape(B, S, HG_HEADS, HG_DV)


def causal_block_attention(q, k, v):
    B, S, H, _ = q.shape
    nqb = S // Q_BLOCK
    qb = q.reshape(B, nqb, Q_BLOCK, H, q.shape[-1]).transpose(1, 0, 2, 3, 4)
    key_pos = jnp.arange(S)

    def one_block(args):
        q_blk, blk = args
        s = jnp.einsum('bqhd,bkhd->bhqk', q_blk, k, preferred_element_type=jnp.float32)
        q_pos = blk * Q_BLOCK + jnp.arange(Q_BLOCK)
        s = jnp.where(key_pos[None, :] <= q_pos[:, None], s, NEG_BIG)
        p = jax.nn.softmax(s, axis=-1).astype(v.dtype)
        return jnp.einsum('bhqk,bkhd->bqhd', p, v)

    out = lax.map(one_block, (qb, jnp.arange(nqb)))
    return out.transpose(1, 0, 2, 3, 4).reshape(B, S, H, v.shape[-1])


def mla_mix(c_q, kv_a, q_norm, w_q_up, kv_norm, w_kv_up, cos, sin):
    B, S = c_q.shape[:2]
    q = (rmsnorm(c_q, q_norm) @ w_q_up).reshape(B, S, MLA_HEADS, MLA_NOPE + MLA_ROPE)
    q_nope, q_rope = q[..., :MLA_NOPE], apply_rope(q[..., MLA_NOPE:], cos, sin)
    c_kv, k_rope = kv_a[..., :MLA_KV_RANK], kv_a[..., MLA_KV_RANK:]
    kv = (rmsnorm(c_kv, kv_norm) @ w_kv_up).reshape(B, S, MLA_HEADS, MLA_NOPE + MLA_V)
    k_nope, v = kv[..., :MLA_NOPE], kv[..., MLA_NOPE:]
    k_rope = apply_rope(k_rope[:, :, None, :], cos, sin)
    k = jnp.concatenate([k_nope, jnp.broadcast_to(k_rope, (B, S, MLA_HEADS, MLA_ROPE))], axis=-1)
    q = jnp.concatenate([q_nope, q_rope], axis=-1) * (MLA_NOPE + MLA_ROPE) ** -0.5
    return causal_block_attention(q, k, v).reshape(B, S, MLA_HEADS * MLA_V)


def memory_xattn(q_mem, mem_n, w_mem_kv):
    B, S = q_mem.shape[:2]
    q = q_mem.reshape(B, S, MEM_HEADS, MEM_HEAD_DIM)
    kv = (mem_n @ w_mem_kv).reshape(B, N_MEM, 2, MEM_HEADS, MEM_HEAD_DIM)
    k, v = kv[:, :, 0], kv[:, :, 1]
    s = jnp.einsum('bshd,bmhd->bhsm', q, k, preferred_element_type=jnp.float32) * MEM_HEAD_DIM ** -0.5
    p = jax.nn.softmax(s, axis=-1).astype(v.dtype)
    return jnp.einsum('bhsm,bmhd->bshd', p, v).reshape(B, S, MEM_HEADS * MEM_HEAD_DIM)


def mixer(a, lb, w_in, hg_norm, mla_q_norm, w_q_up, mla_kv_norm, w_kv_up, mem_n, w_mem_kv,
          w_branch, w_o, cos, sin):
    B, S, _ = a.shape
    sizes = [HG_HEADS * HG_DK, HG_HEADS * HG_DK, HG_HEADS * HG_DV, HG_HEADS * HG_DV,
             MLA_Q_RANK, MLA_KV_RANK + MLA_ROPE, MEM_HEADS * MEM_HEAD_DIM, N_BRANCH * D_MODEL]
    idx = [int(c) for c in np.cumsum(sizes)[:-1]]
    q_hg, f_hg, i_hg, g_hg, c_q, kv_a, q_mem, gate_logits = jnp.split(a @ w_in, idx, axis=-1)
    o_hg = hgrn2_mix(q_hg, f_hg, i_hg, lb)
    y_hg = rmsnorm(o_hg, hg_norm).reshape(B, S, BRANCH_WIDTH).astype(a.dtype) * jax.nn.silu(g_hg)
    y_mla = mla_mix(c_q, kv_a, mla_q_norm, w_q_up, mla_kv_norm, w_kv_up, cos, sin)
    y_mem = memory_xattn(q_mem, mem_n, w_mem_kv)
    branches = jnp.stack([y_hg, y_mla, y_mem], axis=2)
    proj = jnp.einsum('bsnc,ncd->bsnd', branches, w_branch)
    gates = jax.nn.sigmoid(gate_logits).reshape(B, S, N_BRANCH, D_MODEL)
    merged = jnp.sum(gates * proj, axis=2)
    return merged @ w_o


def swiglu(x, w_gu, w_down):
    g, u = jnp.split(x @ w_gu, 2, axis=-1)
    return (jax.nn.silu(g) * u) @ w_down


def moe_ffn(x, w_router, w_gu, w_down):
    B, S, D = x.shape
    T = B * S
    A = T * TOP_K
    xt = x.reshape(T, D)
    logits = (xt @ w_router).astype(jnp.float32)
    top_logit, top_idx = lax.top_k(logits, TOP_K)
    top_w = jax.nn.softmax(top_logit, axis=-1)
    flat_e = top_idx.reshape(A)
    flat_tok = jnp.repeat(jnp.arange(T, dtype=jnp.int32), TOP_K)
    flat_w = top_w.reshape(A)
    order = jnp.argsort(flat_e)
    sorted_e = flat_e[order]
    counts = jnp.bincount(flat_e, length=N_EXPERTS)
    padded = (counts + MOE_BLOCK - 1) // MOE_BLOCK * MOE_BLOCK
    pad_end = jnp.cumsum(padded)
    pad_start = pad_end - padded
    grp_start = jnp.cumsum(counts) - counts
    dest = pad_start[sorted_e] + jnp.arange(A) - grp_start[sorted_e]
    P = ((A + MOE_BLOCK - 1) // MOE_BLOCK + N_EXPERTS) * MOE_BLOCK
    n_blk = P // MOE_BLOCK
    slot_tok = jnp.full((P,), T, jnp.int32).at[dest].set(flat_tok[order])
    slot_w = jnp.zeros((P,), jnp.float32).at[dest].set(flat_w[order])
    blk_e = jnp.minimum(jnp.sum(jnp.arange(n_blk)[:, None] * MOE_BLOCK >= pad_end[None, :], axis=1),
                        N_EXPERTS - 1)
    x_pad = jnp.concatenate([xt, jnp.zeros((1, D), xt.dtype)], axis=0)
    xs = x_pad[slot_tok].reshape(n_blk, MOE_BLOCK, D)

    def expert_block(args):
        xb, e = args
        return swiglu(xb, w_gu[e], w_down[e])

    ys = lax.map(expert_block, (xs, blk_e)).reshape(P, D)
    out = jax.ops.segment_sum(ys.astype(jnp.float32) * slot_w[:, None], slot_tok, num_segments=T + 1)[:T]
    return out.astype(x.dtype).reshape(B, S, D)


def setup_inputs(seed: int = 0) -> dict:
    key = jax.random.key(seed)
    ks = jax.random.split(key, 22)
    f32 = jnp.float32

    def nrm(k, shape, fan_in):
        return jax.random.normal(k, shape, f32) * fan_in ** -0.5

    def gain(k, shape):
        return 1.0 + 0.02 * jax.random.normal(k, shape, f32)

    positions = (jax.random.randint(ks[2], (BATCH, 1), 0, 4096, dtype=jnp.int32)
                 + jnp.arange(SEQ, dtype=jnp.int32)[None, :])
    return {
        'x': jax.random.normal(ks[0], (BATCH, SEQ, D_MODEL), f32),
        'mem': jax.random.normal(ks[1], (BATCH, N_MEM, D_MODEL), f32),
        'positions': positions,
        'hg_lb_logits': 0.1 * jax.random.normal(ks[3], (DEPTH, HG_HEADS * HG_DK), f32),
        'attn_norm': gain(ks[4], (DEPTH, D_MODEL)),
        'w_in': nrm(ks[5], (DEPTH, D_MODEL, D_IN), D_MODEL),
        'hg_norm': gain(ks[6], (DEPTH, HG_DV)),
        'mla_q_norm': gain(ks[7], (DEPTH, MLA_Q_RANK)),
        'w_q_up': nrm(ks[8], (DEPTH, MLA_Q_RANK, MLA_HEADS * (MLA_NOPE + MLA_ROPE)), MLA_Q_RANK),
        'mla_kv_norm': gain(ks[9], (DEPTH, MLA_KV_RANK)),
        'w_kv_up': nrm(ks[10], (DEPTH, MLA_KV_RANK, MLA_HEADS * (MLA_NOPE + MLA_V)), MLA_KV_RANK),
        'mem_norm': gain(ks[11], (D_MODEL,)),
        'w_mem_kv': nrm(ks[12], (DEPTH, D_MODEL, 2 * MEM_HEADS * MEM_HEAD_DIM), D_MODEL),
        'w_branch': nrm(ks[13], (DEPTH, N_BRANCH, BRANCH_WIDTH, D_MODEL), BRANCH_WIDTH),
        'w_o': nrm(ks[14], (DEPTH, D_MODEL, D_MODEL), D_MODEL),
        'ffn_norm': gain(ks[15], (DEPTH, D_MODEL)),
        'w_dense_gu': nrm(ks[16], (N_DENSE, D_MODEL, 2 * D_FF), D_MODEL),
        'w_dense_down': nrm(ks[17], (N_DENSE, D_FF, D_MODEL), D_FF),
        'w_router': nrm(ks[18], (N_MOE, D_MODEL, N_EXPERTS), D_MODEL),
        'w_expert_gu': nrm(ks[19], (N_MOE, N_EXPERTS, D_MODEL, 2 * D_FF_EXPERT), D_MODEL),
        'w_expert_down': nrm(ks[20], (N_MOE, N_EXPERTS, D_FF_EXPERT, D_MODEL), D_FF_EXPERT),
        'final_norm': gain(ks[21], (D_MODEL,)),
    }


def reference(x, mem, positions, hg_lb_logits, attn_norm, w_in, hg_norm, mla_q_norm, w_q_up,
              mla_kv_norm, w_kv_up, mem_norm, w_mem_kv, w_branch, w_o, ffn_norm, w_dense_gu,
              w_dense_down, w_router, w_expert_gu, w_expert_down, final_norm):
    half = MLA_ROPE // 2
    inv_freq = ROPE_THETA ** (-(jnp.arange(half, dtype=jnp.float32) / half))
    ang = positions.astype(jnp.float32)[:, :, None, None] * inv_freq
    cos, sin = jnp.cos(ang), jnp.sin(ang)
    lb_sm = jax.nn.softmax(hg_lb_logits.astype(jnp.float32), axis=0)
    lb_all = jnp.clip(jnp.cumsum(lb_sm, axis=0) - lb_sm[0:1], 0.0, 1.0)
    mem_n = rmsnorm(mem, mem_norm)
    h = x
    for l in range(DEPTH):
        a = rmsnorm(h, attn_norm[l])
        h = h + mixer(a, lb_all[l], w_in[l], hg_norm[l], mla_q_norm[l], w_q_up[l], mla_kv_norm[l],
                      w_kv_up[l], mem_n, w_mem_kv[l], w_branch[l], w_o[l], cos, sin)
        a = rmsnorm(h, ffn_norm[l])
        if l % 2 == 0:
            h = h + swiglu(a, w_dense_gu[l // 2], w_dense_down[l // 2])
        else:
            h = h + moe_ffn(a, w_router[l // 2], w_expert_gu[l // 2], w_expert_down[l // 2])
    return rmsnorm(h, final_norm)
```

```python
import functools
import math

import numpy as np
import jax
import jax.numpy as jnp
from jax import lax
from jax.experimental import pallas as pl
from jax.experimental.pallas import tpu as pltpu

F32 = jnp.float32
BF16 = jnp.bfloat16

EPS = 1e-6
NEG_BIG = -1e30
MIN_FORGET = 1e-20
ROPE_THETA = 10000.0

HG_HEADS = 8
HG_D = 128
HG_CHUNK = 64
HG_SUB = 16
MLA_HEADS = 8
MLA_RANK = 512
MLA_NOPE = 128
MLA_ROPE = 64
MLA_V = 128
MLA_QK_PAD = 256
MEM_HEADS = 4
MEM_HEAD_DIM = 256
N_EXPERTS = 8
TOP_K = 2

LANES = 128
FFN_TM = 256
FFN_RB = 1024
FFN_TF = 256


def _cparams(n_axes, vmem_mb=None):
    kw = dict(dimension_semantics=("arbitrary",) * n_axes)
    if vmem_mb is not None:
        kw["vmem_limit_bytes"] = vmem_mb * 1024 * 1024
    return pltpu.CompilerParams(**kw)


def _sigmoid(x):
    return 1.0 / (1.0 + jnp.exp(-x))


def _dot(a, b):
    return jnp.dot(a, b, preferred_element_type=F32)


def _dot_nt(a, b):
    return lax.dot_general(a, b, (((1,), (1,)), ((), ())), preferred_element_type=F32)


def _rope_kernel(pos_ref, c_ref, cos_ref, sin_ref):
    ang = pos_ref[...].astype(F32) * c_ref[0:1, :]
    cos_ref[...] = jnp.cos(ang) * c_ref[1:2, :]
    sin_ref[...] = jnp.sin(ang) * c_ref[2:3, :]


def rope_tables(positions, tm=1024):
    T = positions.size
    half = MLA_ROPE // 2
    inv = ROPE_THETA ** (-(np.arange(half, dtype=np.float32) / half))
    consts = np.zeros((8, LANES), np.float32)
    consts[0, :half] = inv
    consts[0, half:2 * half] = inv
    consts[1, :2 * half] = 1.0
    consts[2, :half] = -1.0
    consts[2, half:2 * half] = 1.0
    tm = min(tm, T)
    out = jax.ShapeDtypeStruct((T, LANES), F32)
    return pl.pallas_call(
        _rope_kernel,
        grid=(T // tm,),
        in_specs=[pl.BlockSpec((tm, 1), lambda i: (i, 0)),
                  pl.BlockSpec((8, LANES), lambda i: (0, 0))],
        out_specs=[pl.BlockSpec((tm, LANES), lambda i: (i, 0)),
                   pl.BlockSpec((tm, LANES), lambda i: (i, 0))],
        out_shape=[out, out],
        compiler_params=_cparams(1),
        name="rope_tables",
    )(positions.reshape(T, 1), jnp.asarray(consts))


def _rmsnorm_kernel(x_ref, g_ref, o_ref):
    x = x_ref[...].astype(F32)
    y = x * lax.rsqrt(jnp.mean(x * x, axis=-1, keepdims=True) + EPS)
    o_ref[...] = (y * g_ref[...]).astype(o_ref.dtype)


def rmsnorm_cast(x, g, out_dtype, tm=512):
    M, D = x.shape
    tm = min(tm, M)
    return pl.pallas_call(
        _rmsnorm_kernel,
        grid=(M // tm,),
        in_specs=[pl.BlockSpec((tm, D), lambda i: (i, 0)),
                  pl.BlockSpec((1, D), lambda i: (0, 0))],
        out_specs=pl.BlockSpec((tm, D), lambda i: (i, 0)),
        out_shape=jax.ShapeDtypeStruct((M, D), out_dtype),
        compiler_params=_cparams(1),
        name="rmsnorm",
    )(x, g.reshape(1, D).astype(F32))


def _mm_kernel(x_ref, w_ref, o_ref, *scratch, cast):
    if cast:
        (wb,) = scratch

        @pl.when(pl.program_id(1) == 0)
        def _():
            wb[...] = w_ref[...].astype(BF16)

        w = wb[...]
    else:
        w = w_ref[...]
    o_ref[...] = _dot(x_ref[...], w).astype(o_ref.dtype)


def matmul(x, w, *, n_cols, tn, tm, out_dtype, layer=None, name="matmul"):
    M, K = x.shape
    tm = min(tm, M)
    cast = w.dtype != BF16
    if layer is None:
        w_spec = pl.BlockSpec((K, tn), lambda j, i: (0, j))
    else:
        w_spec = pl.BlockSpec((None, K, tn), lambda j, i: (layer, 0, j))
    scratch = [pltpu.VMEM((K, tn), BF16)] if cast else []
    return pl.pallas_call(
        functools.partial(_mm_kernel, cast=cast),
        grid=(n_cols // tn, M // tm),
        in_specs=[pl.BlockSpec((tm, K), lambda j, i: (i, 0)), w_spec],
        out_specs=pl.BlockSpec((tm, tn), lambda j, i: (i, j)),
        out_shape=jax.ShapeDtypeStruct((M, n_cols), out_dtype),
        scratch_shapes=scratch,
        compiler_params=_cparams(2, 48),
        name=name,
    )(x, w)


def _hgrn_kernel(q_ref, f_ref, i_ref, g_ref, lb_ref, gn_ref, tri_ref, ones_ref, o_ref, st_ref):
    L, SUB, D = HG_CHUNK, HG_SUB, HG_D

    @pl.when(pl.program_id(1) == 0)
    def _():
        st_ref[...] = jnp.zeros_like(st_ref)

    lb = lb_ref[...]
    f = lb + (1.0 - lb) * _sigmoid(f_ref[...])
    lf = jnp.log(jnp.maximum(f, MIN_FORGET))
    kk = 1.0 - f
    tri = tri_ref[...]
    p1 = lf.astype(BF16)
    r1 = lf - p1.astype(F32)
    p2 = r1.astype(BF16)
    p3 = (r1 - p2.astype(F32)).astype(BF16)
    b = _dot(tri, p1) + _dot(tri, p2) + _dot(tri, p3)

    q = q_ref[...]
    v = i_ref[...]
    g = g_ref[...]
    gn = gn_ref[...]
    ones = ones_ref[...]
    b_last = b[L - 1:L, :]
    q_in = (q * jnp.exp(b)).astype(BF16)
    k_dec = (kk * jnp.exp(b_last - b)).astype(BF16)
    e_last = jnp.exp(b_last)
    row_sub = lax.broadcasted_iota(jnp.int32, (SUB, D), 0)
    row_all = lax.broadcasted_iota(jnp.int32, (L, D), 0)

    for h in range(HG_HEADS):
        hs = slice(h * D, (h + 1) * D)
        bh, qh, kh, vh = b[:, hs], q[:, hs], kk[:, hs], v[:, hs]
        vb = vh.astype(BF16)
        st = st_ref[h]
        o = _dot_nt(q_in[:, hs], st.astype(BF16))
        diag_parts = []
        off_parts = []
        for i in range(L // SUB):
            r0 = i * SUB
            bs, qs, ks, vs = bh[r0:r0 + SUB], qh[r0:r0 + SUB], kh[r0:r0 + SUB], vh[r0:r0 + SUB]
            terms = []
            for s in range(SUB):
                d = jnp.where(row_sub >= s, bs - bs[s:s + 1], NEG_BIG)
                terms.append((qs * ks[s:s + 1] * jnp.exp(d)).astype(BF16))
            ssum = _dot(jnp.concatenate(terms, axis=0), ones)
            acc = ssum[0:SUB] * vs[0:1]
            for s in range(1, SUB):
                acc = acc + ssum[s * SUB:(s + 1) * SUB] * vs[s:s + 1]
            diag_parts.append(acc)
            if i == 0:
                off_parts.append(jnp.zeros((SUB, L), F32))
            else:
                ref_row = bh[r0 - 1:r0]
                q_i = (qs * jnp.exp(bs - ref_row)).astype(BF16)
                k_i = (kh * jnp.exp(jnp.where(row_all < r0, ref_row - bh, NEG_BIG))).astype(BF16)
                off_parts.append(_dot_nt(q_i, k_i))
        s_off = jnp.concatenate(off_parts, axis=0).astype(BF16)
        o = o + jnp.concatenate(diag_parts, axis=0) + _dot(s_off, vb)
        st_ref[h] = e_last[:, hs] * st + lax.dot_general(
            vb, k_dec[:, hs], (((0,), (0,)), ((), ())), preferred_element_type=F32)
        on = o * lax.rsqrt(jnp.mean(o * o, axis=-1, keepdims=True) + EPS) * gn[:, hs]
        gh = g[:, hs]
        o_ref[:, hs] = (on * (gh * _sigmoid(gh))).astype(o_ref.dtype)


def hgrn_mix(z1, lb, hg_norm, batch, seq):
    T = batch * seq
    W = HG_HEADS * HG_D
    L = HG_CHUNK
    nc = seq // L
    tri = jnp.asarray(np.tril(np.ones((L, L), np.float32)), BF16)
    ones = jnp.ones((HG_D, HG_D), BF16)

    def col(c):
        return pl.BlockSpec((L, W), lambda b, j, c=c: (b * nc + j, c))

    const = lambda shape: pl.BlockSpec(shape, lambda b, j: (0, 0))
    return pl.pallas_call(
        _hgrn_kernel,
        grid=(batch, nc),
        in_specs=[col(0), col(1), col(2), col(3), const((1, W)), const((1, W)),
                  const((L, L)), const((HG_D, HG_D))],
        out_specs=pl.BlockSpec((L, W), lambda b, j: (b * nc + j, 0)),
        out_shape=jax.ShapeDtypeStruct((T, W), BF16),
        scratch_shapes=[pltpu.VMEM((HG_HEADS, HG_D, HG_D), F32)],
        compiler_params=_cparams(2, 48),
        name="hgrn2",
    )(z1, z1, z1, z1, lb.reshape(1, W).astype(F32),
      jnp.tile(hg_norm.astype(F32), HG_HEADS).reshape(1, W), tri, ones)


def _mla_prep_kernel(cq_ref, ckv_ref, kr_ref, cos_ref, sin_ref, gq_ref, gkv_ref,
                     wq1_ref, wq2_ref, wkv_ref, q_ref, k_ref, v_ref):
    scale = (MLA_NOPE + MLA_ROPE) ** -0.5
    cos = cos_ref[...]
    sin = sin_ref[...]

    def norm(x, g):
        return (x * lax.rsqrt(jnp.mean(x * x, axis=-1, keepdims=True) + EPS) * g).astype(BF16)

    cn = norm(cq_ref[...], gq_ref[...])
    q1 = _dot(cn, wq1_ref[...])
    q2 = _dot(cn, wq2_ref[...])
    cvn = norm(ckv_ref[...], gkv_ref[...])
    kv = _dot(cvn, wkv_ref[...])
    kr = kr_ref[...].astype(F32)
    kr_rot = (kr[:, 0:LANES] * cos + kr[:, LANES:2 * LANES] * sin).astype(BF16)
    P = MLA_QK_PAD
    for h in range(MLA_HEADS):
        q_ref[:, h * P:h * P + LANES] = (q1[:, h * P:h * P + LANES] * scale).astype(BF16)
        q_ref[:, h * P + LANES:(h + 1) * P] = (
            (q1[:, h * P + LANES:(h + 1) * P] * cos + q2[:, h * LANES:(h + 1) * LANES] * sin) * scale
        ).astype(BF16)
        k_ref[:, h * P:h * P + LANES] = kv[:, h * LANES:(h + 1) * LANES].astype(BF16)
        k_ref[:, h * P + LANES:(h + 1) * P] = kr_rot
    nv = MLA_HEADS * MLA_V
    v_ref[...] = kv[:, nv:2 * nv].astype(BF16)


def mla_prep(z1, z3, cos_t, sin_t, gq, gkv, wq1, wq2, wkv, tm=512):
    T = z1.shape[0]
    tm = min(tm, T)
    R = MLA_RANK
    cq_blk = (4 * HG_HEADS * HG_D) // R
    row = lambda w: pl.BlockSpec((tm, w), lambda i: (i, 0))
    const = lambda a: pl.BlockSpec(a.shape, lambda i: (0, 0))
    qk_w = MLA_HEADS * MLA_QK_PAD
    return pl.pallas_call(
        _mla_prep_kernel,
        grid=(T // tm,),
        in_specs=[pl.BlockSpec((tm, R), lambda i: (i, cq_blk)),
                  pl.BlockSpec((tm, R), lambda i: (i, cq_blk + 1)),
                  row(2 * LANES), row(LANES), row(LANES),
                  pl.BlockSpec((1, R), lambda i: (0, 0)), pl.BlockSpec((1, R), lambda i: (0, 0)),
                  const(wq1), const(wq2), const(wkv)],
        out_specs=[row(qk_w), row(qk_w), row(MLA_HEADS * MLA_V)],
        out_shape=[jax.ShapeDtypeStruct((T, qk_w), BF16), jax.ShapeDtypeStruct((T, qk_w), BF16),
                   jax.ShapeDtypeStruct((T, MLA_HEADS * MLA_V), BF16)],
        compiler_params=_cparams(1, 48),
        name="mla_prep",
    )(z1, z1, z3, cos_t, sin_t, gq.reshape(1, R).astype(F32), gkv.reshape(1, R).astype(F32),
      wq1, wq2, wkv)


def mla_weights(w_q_up, w_kv_up, w_kr):
    R, half = MLA_RANK, MLA_ROPE // 2
    wq = w_q_up.reshape(R, MLA_HEADS, MLA_NOPE + MLA_ROPE)
    nope, rope = wq[..., :MLA_NOPE], wq[..., MLA_NOPE:]
    rope_sw = jnp.concatenate([rope[..., half:], rope[..., :half]], axis=-1)
    z64 = jnp.zeros((R, MLA_HEADS, MLA_QK_PAD - MLA_NOPE - MLA_ROPE), w_q_up.dtype)
    wq1 = jnp.concatenate([nope, rope, z64], axis=-1).reshape(R, MLA_HEADS * MLA_QK_PAD)
    wq2 = jnp.concatenate([rope_sw, z64], axis=-1).reshape(R, MLA_HEADS * LANES)
    wkv = w_kv_up.reshape(R, MLA_HEADS, MLA_NOPE + MLA_V)
    wkv = jnp.concatenate([wkv[..., :MLA_NOPE].reshape(R, -1), wkv[..., MLA_NOPE:].reshape(R, -1)], axis=-1)
    zk = jnp.zeros((w_kr.shape[0], LANES - MLA_ROPE), w_kr.dtype)
    kr_sw = jnp.concatenate([w_kr[:, half:], w_kr[:, :half]], axis=-1)
    wkr = jnp.concatenate([w_kr, zk, kr_sw, zk], axis=-1)
    return wq1.astype(BF16), wq2.astype(BF16), wkv.astype(BF16), wkr.astype(BF16)


def _causal_attn_kernel(q_ref, k_ref, v_ref, o_ref, m_ref, l_ref, acc_ref, *, tq):
    i = pl.program_id(2)
    q = q_ref[...]
    m_ref[...] = jnp.full_like(m_ref, NEG_BIG)
    l_ref[...] = jnp.zeros_like(l_ref)
    acc_ref[...] = jnp.zeros_like(acc_ref)
    rows = i * tq + lax.broadcasted_iota(jnp.int32, (tq, tq), 0)
    cols0 = lax.broadcasted_iota(jnp.int32, (tq, tq), 1)

    def body(j, carry):
        start = pl.multiple_of(j * tq, tq)
        s = _dot_nt(q, k_ref[pl.ds(start, tq), :])
        s = jnp.where(cols0 + j * tq <= rows, s, NEG_BIG)
        m_old = m_ref[...]
        m_new = jnp.maximum(m_old, jnp.max(s, axis=-1, keepdims=True))
        p = jnp.exp(s - m_new)
        alpha = jnp.exp(m_old - m_new)
        l_ref[...] = alpha * l_ref[...] + jnp.sum(p, axis=-1, keepdims=True)
        acc_ref[...] = alpha * acc_ref[...] + _dot(p.astype(BF16), v_ref[pl.ds(start, tq), :])
        m_ref[...] = m_new
        return carry

    lax.fori_loop(0, i + 1, body, 0)
    o_ref[...] = (acc_ref[...] / l_ref[...]).astype(o_ref.dtype)


def causal_attention(qp, kp, v, batch, seq, tq=512):
    T = batch * seq
    tq = min(tq, seq)
    nq = seq // tq
    P, DV = MLA_QK_PAD, MLA_V
    return pl.pallas_call(
        functools.partial(_causal_attn_kernel, tq=tq),
        grid=(batch, MLA_HEADS, nq),
        in_specs=[pl.BlockSpec((tq, P), lambda b, h, i: (b * nq + i, h)),
                  pl.BlockSpec((seq, P), lambda b, h, i: (b, h)),
                  pl.BlockSpec((seq, DV), lambda b, h, i: (b, h))],
        out_specs=pl.BlockSpec((tq, DV), lambda b, h, i: (b * nq + i, h)),
        out_shape=jax.ShapeDtypeStruct((T, MLA_HEADS * DV), BF16),
        scratch_shapes=[pltpu.VMEM((tq, 1), F32), pltpu.VMEM((tq, 1), F32), pltpu.VMEM((tq, DV), F32)],
        compiler_params=_cparams(3, 48),
        name="mla_attention",
    )(qp, kp, v)


def _mem_attn_kernel(q_ref, k_ref, v_ref, o_ref):
    scale = MEM_HEAD_DIM ** -0.5
    Dh = MEM_HEAD_DIM
    for h in range(MEM_HEADS):
        hs = slice(h * Dh, (h + 1) * Dh)
        s = _dot_nt(q_ref[:, hs], k_ref[:, hs]) * scale
        m = jnp.max(s, axis=-1, keepdims=True)
        p = jnp.exp(s - m)
        l = jnp.sum(p, axis=-1, keepdims=True)
        o = _dot(p.astype(BF16), v_ref[:, hs])
        o_ref[:, hs] = (o / l).astype(o_ref.dtype)


def memory_attention(z2, mem_kv, batch, seq, n_mem, tq=512):
    T = batch * seq
    tq = min(tq, seq)
    nq = seq // tq
    W = MEM_HEADS * MEM_HEAD_DIM
    return pl.pallas_call(
        _mem_attn_kernel,
        grid=(batch, nq),
        in_specs=[pl.BlockSpec((tq, W), lambda b, i: (b * nq + i, 0)),
                  pl.BlockSpec((n_mem, W), lambda b, i: (b, 0)),
                  pl.BlockSpec((n_mem, W), lambda b, i: (b, 1))],
        out_specs=pl.BlockSpec((tq, W), lambda b, i: (b * nq + i, 0)),
        out_shape=jax.ShapeDtypeStruct((T, W), BF16),
        compiler_params=_cparams(2, 48),
        name="mem_attention",
    )(z2, mem_kv, mem_kv)


def _merge_kernel(y0_ref, y1_ref, y2_ref, g0_ref, g1_ref, g2_ref, w_ref, o_ref):
    acc = None
    for n, (y_ref, g_ref) in enumerate(((y0_ref, g0_ref), (y1_ref, g1_ref), (y2_ref, g2_ref))):
        term = _sigmoid(g_ref[...].astype(F32)) * _dot(y_ref[...], w_ref[n])
        acc = term if acc is None else acc + term
    o_ref[...] = acc.astype(o_ref.dtype)


def branch_merge(y_hg, y_mla, y_mem, z2, w_branch, tm=1024, tn=512):
    T, W = y_hg.shape
    D = w_branch.shape[-1]
    tm = min(tm, T)
    g_off = W // tn

    def gate(n):
        return pl.BlockSpec((tm, tn), lambda i, j, n=n: (i, g_off + n * (D // tn) + j))

    y_spec = pl.BlockSpec((tm, W), lambda i, j: (i, 0))
    return pl.pallas_call(
        _merge_kernel,
        grid=(T // tm, D // tn),
        in_specs=[y_spec, y_spec, y_spec, gate(0), gate(1), gate(2),
                  pl.BlockSpec((3, W, tn), lambda i, j: (0, 0, j))],
        out_specs=pl.BlockSpec((tm, tn), lambda i, j: (i, j)),
        out_shape=jax.ShapeDtypeStruct((T, D), BF16),
        compiler_params=_cparams(2, 48),
        name="branch_merge",
    )(y_hg, y_mla, y_mem, z2, z2, z2, w_branch)


def _proj_res_norm_kernel(x_ref, w_ref, h_ref, g_ref, hn_ref, a_ref):
    hn = h_ref[...] + _dot(x_ref[...], w_ref[...])
    hn_ref[...] = hn
    a = hn * lax.rsqrt(jnp.mean(hn * hn, axis=-1, keepdims=True) + EPS) * g_ref[...]
    a_ref[...] = a.astype(a_ref.dtype)


def proj_residual_norm(x, w, h, gain, a_dtype, tm=512):
    T, K = x.shape
    D = w.shape[1]
    tm = min(tm, T)
    row = lambda width: pl.BlockSpec((tm, width), lambda i: (i, 0))
    return pl.pallas_call(
        _proj_res_norm_kernel,
        grid=(T // tm,),
        in_specs=[row(K), pl.BlockSpec((K, D), lambda i: (0, 0)), row(D),
                  pl.BlockSpec((1, D), lambda i: (0, 0))],
        out_specs=[row(D), row(D)],
        out_shape=[jax.ShapeDtypeStruct((T, D), F32), jax.ShapeDtypeStruct((T, D), a_dtype)],
        compiler_params=_cparams(1, 48),
        name="proj_residual_norm",
    )(x, w, h, gain.reshape(1, D).astype(F32))


def _res_norm_kernel(y_ref, h_ref, g_ref, hn_ref, a_ref):
    hn = h_ref[...] + y_ref[...]
    hn_ref[...] = hn
    a = hn * lax.rsqrt(jnp.mean(hn * hn, axis=-1, keepdims=True) + EPS) * g_ref[...]
    a_ref[...] = a.astype(a_ref.dtype)


def residual_norm(y, h, gain, a_dtype, tm=512):
    T, D = y.shape
    tm = min(tm, T)
    row = pl.BlockSpec((tm, D), lambda i: (i, 0))
    return pl.pallas_call(
        _res_norm_kernel,
        grid=(T // tm,),
        in_specs=[row, row, pl.BlockSpec((1, D), lambda i: (0, 0))],
        out_specs=[row, row],
        out_shape=[jax.ShapeDtypeStruct((T, D), F32), jax.ShapeDtypeStruct((T, D), a_dtype)],
        compiler_params=_cparams(1),
        name="residual_norm",
    )(y, h, gain.reshape(1, D).astype(F32))


def _ffn_kernel(sb_ref, ex_ref, jw_ref, rlo_ref, rhi_ref, first_ref,
                x_ref, wg_ref, wu_ref, wd_ref, o_ref, wgb, wub, wdb, *, n_f):
    v = pl.program_id(0)
    j = pl.program_id(1)

    @pl.when(jnp.logical_and(first_ref[v] == 1, j == 0))
    def _():
        o_ref[...] = jnp.zeros_like(o_ref)

    rlo = rlo_ref[v]
    rhi = rhi_ref[v]

    @pl.when(rhi > rlo)
    def _():
        wgb[...] = wg_ref[...].astype(BF16)
        wub[...] = wu_ref[...].astype(BF16)
        wdb[...] = wd_ref[...].astype(BF16)

        def body(r, carry):
            rows = pl.ds(pl.multiple_of(r * FFN_TM, FFN_TM), FFN_TM)
            xb = x_ref[rows, :]
            gate = _dot(xb, wgb[...])
            up = _dot(xb, wub[...])
            hid = (gate * _sigmoid(gate) * up).astype(BF16)
            o_ref[rows, :] += _dot(hid, wdb[...])
            return carry

        lax.fori_loop(rlo, rhi, body, 0)


def ffn_schedule(pad_start, pad_end, n_rows):
    n_sb = n_rows // FFN_RB
    E = pad_start.shape[0]
    nv = n_sb + E - 1
    sb_lo = jnp.arange(n_sb, dtype=jnp.int32)[:, None] * FFN_RB
    lo = jnp.maximum(pad_start[None, :], sb_lo)
    hi = jnp.minimum(pad_end[None, :], sb_lo + FFN_RB)
    act = hi > lo
    none = jnp.logical_not(jnp.any(act, axis=1))
    cand = act.at[:, 0].set(jnp.logical_or(act[:, 0], none))
    flat = cand.reshape(-1)
    order = jnp.argsort(jnp.logical_not(flat), stable=True)[:nv].astype(jnp.int32)
    n_valid = jnp.sum(flat.astype(jnp.int32))
    valid = jnp.arange(nv, dtype=jnp.int32) < n_valid
    last = order[jnp.maximum(n_valid - 1, 0)]
    order = jnp.where(valid, order, last)
    sb = order // E
    ex = order % E
    work = jnp.logical_and(valid, act.reshape(-1)[order])
    rlo = jnp.where(work, (lo.reshape(-1)[order] - sb * FFN_RB) // FFN_TM, 0).astype(jnp.int32)
    rhi = jnp.where(work, (hi.reshape(-1)[order] - sb * FFN_RB) // FFN_TM, 0).astype(jnp.int32)
    prev_sb = jnp.concatenate([jnp.full((1,), -1, jnp.int32), sb[:-1]])
    first = jnp.logical_and(valid, sb != prev_sb).astype(jnp.int32)
    idx = jnp.arange(nv, dtype=jnp.int32)
    last_work = lax.cummax(jnp.where(work, idx, -1), axis=0)
    ex_eff = jnp.where(last_work >= 0, ex[jnp.maximum(last_work, 0)], ex[0]).astype(jnp.int32)
    return sb.astype(jnp.int32), ex_eff, work.astype(jnp.int32), rlo, rhi, first


def grouped_ffn(x, w_gu, w_down, sched):
    R, D = x.shape
    E, _, F2 = w_gu.shape
    F = F2 // 2
    n_f = F // FFN_TF
    sb, ex, work, rlo, rhi, first = sched
    nv = sb.shape[0]

    def jf(j, work_ref, v):
        return jnp.where(work_ref[v] == 1, j, n_f - 1)

    grid_spec = pltpu.PrefetchScalarGridSpec(
        num_scalar_prefetch=6,
        grid=(nv, n_f),
        in_specs=[
            pl.BlockSpec((FFN_RB, D), lambda v, j, sb, ex, wk, rlo, rhi, fi: (sb[v], 0)),
            pl.BlockSpec((None, D, FFN_TF), lambda v, j, sb, ex, wk, rlo, rhi, fi: (ex[v], 0, jf(j, wk, v))),
            pl.BlockSpec((None, D, FFN_TF), lambda v, j, sb, ex, wk, rlo, rhi, fi: (ex[v], 0, n_f + jf(j, wk, v))),
            pl.BlockSpec((None, FFN_TF, D), lambda v, j, sb, ex, wk, rlo, rhi, fi: (ex[v], jf(j, wk, v), 0)),
        ],
        out_specs=pl.BlockSpec((FFN_RB, D), lambda v, j, sb, ex, wk, rlo, rhi, fi: (sb[v], 0)),
        scratch_shapes=[pltpu.VMEM((D, FFN_TF), BF16), pltpu.VMEM((D, FFN_TF), BF16),
                        pltpu.VMEM((FFN_TF, D), BF16)],
    )
    return pl.pallas_call(
        functools.partial(_ffn_kernel, n_f=n_f),
        grid_spec=grid_spec,
        out_shape=jax.ShapeDtypeStruct((R, D), F32),
        compiler_params=_cparams(2, 56),
        name="grouped_ffn",
    )(sb, ex, work, rlo, rhi, first, x, w_gu, w_gu, w_down)


def _router_kernel(a_ref, w_ref, o_ref):
    logits = _dot(a_ref[...].astype(BF16), w_ref[...])
    lane = lax.broadcasted_iota(jnp.int32, logits.shape, 1)
    lg = jnp.where(lane < N_EXPERTS, logits, NEG_BIG)
    m1 = jnp.max(lg, axis=-1, keepdims=True)
    i1 = jnp.min(jnp.where(lg == m1, lane, LANES), axis=-1, keepdims=True)
    lg2 = jnp.where(lane == i1, NEG_BIG, lg)
    m2 = jnp.max(lg2, axis=-1, keepdims=True)
    i2 = jnp.min(jnp.where(lg2 == m2, lane, LANES), axis=-1, keepdims=True)
    e2 = jnp.exp(m2 - m1)
    w1 = 1.0 / (1.0 + e2)
    w2 = e2 / (1.0 + e2)
    out = jnp.where(lane == 0, i1.astype(F32),
                    jnp.where(lane == 1, i2.astype(F32),
                              jnp.where(lane == 2, w1, jnp.where(lane == 3, w2, 0.0))))
    o_ref[...] = out


def router(a, w_router, tm=1024):
    T, D = a.shape
    tm = min(tm, T)
    w = jnp.zeros((D, LANES), BF16).at[:, :N_EXPERTS].set(w_router.astype(BF16))
    return pl.pallas_call(
        _router_kernel,
        grid=(T // tm,),
        in_specs=[pl.BlockSpec((tm, D), lambda i: (i, 0)), pl.BlockSpec((D, LANES), lambda i: (0, 0))],
        out_specs=pl.BlockSpec((tm, LANES), lambda i: (i, 0)),
        out_shape=jax.ShapeDtypeStruct((T, LANES), F32),
        compiler_params=_cparams(1, 48),
        name="router",
    )(a, w)


def _gather_kernel(tok_ref, a_ref, o_ref, buf, sem, *, tm):
    base = pl.program_id(0) * tm

    def row_copy(r, tok):
        return pltpu.make_async_copy(a_ref.at[pl.ds(tok, 1), :], buf.at[pl.ds(r, 1), :], sem)

    def issue(r, carry):
        row_copy(r, tok_ref[base + r]).start()
        return carry

    lax.fori_loop(0, tm, issue, 0)

    def drain(r, carry):
        row_copy(r, 0).wait()
        return carry

    lax.fori_loop(0, tm, drain, 0)
    o_ref[...] = buf[...].astype(o_ref.dtype)


def gather_rows(a, slot_tok, tm=256):
    T, D = a.shape
    R = slot_tok.shape[0]
    grid_spec = pltpu.PrefetchScalarGridSpec(
        num_scalar_prefetch=1,
        grid=(R // tm,),
        in_specs=[pl.BlockSpec(memory_space=pl.ANY)],
        out_specs=pl.BlockSpec((tm, D), lambda i, tok: (i, 0)),
        scratch_shapes=[pltpu.VMEM((tm, D), F32), pltpu.SemaphoreType.DMA(())],
    )
    return pl.pallas_call(
        functools.partial(_gather_kernel, tm=tm),
        grid_spec=grid_spec,
        out_shape=jax.ShapeDtypeStruct((R, D), BF16),
        compiler_params=_cparams(1),
        name="gather_rows",
    )(slot_tok, a)


def _combine_kernel(dest_ref, y_ref, rw_ref, h_ref, g_ref, hn_ref, a_ref, buf, sem, *, tm):
    base = pl.program_id(0) * tm

    def row_copy(k, r, slot):
        return pltpu.make_async_copy(y_ref.at[pl.ds(slot, 1), :], buf.at[k, pl.ds(r, 1), :], sem)

    def issue(r, carry):
        for k in range(TOP_K):
            row_copy(k, r, dest_ref[(base + r) * TOP_K + k]).start()
        return carry

    lax.fori_loop(0, tm, issue, 0)

    def drain(r, carry):
        for k in range(TOP_K):
            row_copy(k, r, 0).wait()
        return carry

    lax.fori_loop(0, tm, drain, 0)
    rw = rw_ref[...]
    hn = h_ref[...] + rw[:, 2:3] * buf[0] + rw[:, 3:4] * buf[1]
    hn_ref[...] = hn
    a = hn * lax.rsqrt(jnp.mean(hn * hn, axis=-1, keepdims=True) + EPS) * g_ref[...]
    a_ref[...] = a.astype(a_ref.dtype)


def combine_residual_norm(y_sorted, dest, rw, h, gain, a_dtype, tm=256):
    T, D = h.shape
    tm = min(tm, T)
    row = pl.BlockSpec((tm, D), lambda i, d: (i, 0))
    grid_spec = pltpu.PrefetchScalarGridSpec(
        num_scalar_prefetch=1,
        grid=(T // tm,),
        in_specs=[pl.BlockSpec(memory_space=pl.ANY),
                  pl.BlockSpec((tm, LANES), lambda i, d: (i, 0)),
                  row,
                  pl.BlockSpec((1, D), lambda i, d: (0, 0))],
        out_specs=[row, row],
        scratch_shapes=[pltpu.VMEM((TOP_K, tm, D), F32), pltpu.SemaphoreType.DMA(())],
    )
    return pl.pallas_call(
        functools.partial(_combine_kernel, tm=tm),
        grid_spec=grid_spec,
        out_shape=[jax.ShapeDtypeStruct((T, D), F32), jax.ShapeDtypeStruct((T, D), a_dtype)],
        compiler_params=_cparams(1),
        name="combine_residual_norm",
    )(dest, y_sorted, rw, h, gain.reshape(1, D).astype(F32))


def mixer_layer(h, a, l, lb, mem_n, cos_t, sin_t, p, batch, seq, n_mem, next_gain, a_dtype):
    D = h.shape[1]
    n_a = 4 * HG_HEADS * HG_D + 2 * MLA_RANK
    kr0 = n_a
    q_mem0 = kr0 + MLA_ROPE
    z1 = matmul(a, p["w_in"], layer=l, n_cols=n_a, tn=512, tm=1024, out_dtype=F32, name="in_proj_a")
    w_rest = p["w_in"][l, :, q_mem0:].astype(BF16)
    z2 = matmul(a, w_rest, n_cols=w_rest.shape[1], tn=512, tm=1024, out_dtype=BF16, name="in_proj_b")
    wq1, wq2, wkv, wkr = mla_weights(p["w_q_up"][l], p["w_kv_up"][l], p["w_in"][l, :, kr0:q_mem0])
    z3 = matmul(a, wkr, n_cols=2 * LANES, tn=2 * LANES, tm=1024, out_dtype=F32, name="in_proj_kr")

    y_hg = hgrn_mix(z1, lb, p["hg_norm"][l], batch, seq)
    qp, kp, v = mla_prep(z1, z3, cos_t, sin_t, p["mla_q_norm"][l], p["mla_kv_norm"][l], wq1, wq2, wkv)
    y_mla = causal_attention(qp, kp, v, batch, seq)
    mem_kv = matmul(mem_n, p["w_mem_kv"], layer=l, n_cols=2 * MEM_HEADS * MEM_HEAD_DIM, tn=512,
                    tm=1024, out_dtype=BF16, name="mem_kv")
    y_mem = memory_attention(z2, mem_kv, batch, seq, n_mem)
    merged = branch_merge(y_hg, y_mla, y_mem, z2, p["w_branch"][l].astype(BF16))
    return proj_residual_norm(merged, p["w_o"][l].astype(BF16), h, next_gain, a_dtype, tm=256)


def group_bounds(starts, ends, index, n_groups, n_rows):
    E = starts.shape[0]
    before = jnp.zeros((index * E,), jnp.int32)
    after = jnp.full(((n_groups - index - 1) * E,), n_rows, jnp.int32)
    return (jnp.concatenate([before, starts.astype(jnp.int32), after]),
            jnp.concatenate([before, ends.astype(jnp.int32), after]))


def moe_routing(rw, n_tokens):
    A = n_tokens * TOP_K
    flat_e = rw[:, :TOP_K].astype(jnp.int32).reshape(A)
    onehot = (flat_e[:, None] == jnp.arange(N_EXPERTS, dtype=jnp.int32)[None, :]).astype(jnp.int32)
    csum = jnp.cumsum(onehot, axis=0)
    rank = jnp.take_along_axis(csum, flat_e[:, None], axis=1)[:, 0] - 1
    counts = csum[-1]
    padded = (counts + FFN_TM - 1) // FFN_TM * FFN_TM
    pad_end = jnp.cumsum(padded).astype(jnp.int32)
    pad_start = pad_end - padded
    dest = (pad_start[flat_e] + rank).astype(jnp.int32)
    n_rows = (A + N_EXPERTS * FFN_TM + FFN_RB - 1) // FFN_RB * FFN_RB
    slot_tok = jnp.zeros((n_rows,), jnp.int32).at[dest].set(jnp.arange(A, dtype=jnp.int32) // TOP_K)
    return dest, slot_tok, pad_start, pad_end, n_rows


def kernel(x, mem, positions, hg_lb_logits, attn_norm, w_in, hg_norm, mla_q_norm, w_q_up, mla_kv_norm,
           w_kv_up, mem_norm, w_mem_kv, w_branch, w_o, ffn_norm, w_dense_gu, w_dense_down, w_router,
           w_expert_gu, w_expert_down, final_norm):
    batch, seq, D = x.shape
    n_mem = mem.shape[1]
    depth = w_in.shape[0]
    T = batch * seq
    p = dict(w_in=w_in, hg_norm=hg_norm, mla_q_norm=mla_q_norm, w_q_up=w_q_up, mla_kv_norm=mla_kv_norm,
             w_kv_up=w_kv_up, w_mem_kv=w_mem_kv, w_branch=w_branch, w_o=w_o)

    cos_t, sin_t = rope_tables(positions.astype(jnp.int32))
    lb_sm = jax.nn.softmax(hg_lb_logits.astype(F32), axis=0)
    lb_all = jnp.clip(jnp.cumsum(lb_sm, axis=0) - lb_sm[0:1], 0.0, 1.0)
    mem_n = rmsnorm_cast(mem.reshape(batch * n_mem, D), mem_norm, BF16)

    n_dense, n_moe = w_dense_gu.shape[0], w_expert_gu.shape[0]
    w_moe_gu = w_expert_gu.reshape((n_moe * N_EXPERTS,) + w_expert_gu.shape[2:])
    w_moe_down = w_expert_down.reshape((n_moe * N_EXPERTS,) + w_expert_down.shape[2:])

    h = x.reshape(T, D)
    a = rmsnorm_cast(h, attn_norm[0], BF16)
    for l in range(depth):
        moe = l % 2 == 1
        last = l + 1 == depth
        next_gain = final_norm if last else attn_norm[l + 1]
        next_dtype = F32 if last else BF16
        h, a_ffn = mixer_layer(h, a, l, lb_all[l], mem_n, cos_t, sin_t, p, batch, seq, n_mem,
                               ffn_norm[l], F32 if moe else BF16)
        if not moe:
            zero = jnp.zeros((1,), jnp.int32)
            starts, ends = group_bounds(zero, zero + T, l // 2, n_dense, T)
            y = grouped_ffn(a_ffn, w_dense_gu, w_dense_down, ffn_schedule(starts, ends, T))
            h, a = residual_norm(y, h, next_gain, next_dtype)
        else:
            rw = router(a_ffn, w_router[l // 2])
            dest, slot_tok, pad_start, pad_end, n_rows = moe_routing(rw, T)
            xs = gather_rows(a_ffn, slot_tok)
            starts, ends = group_bounds(pad_start, pad_end, l // 2, n_moe, n_rows)
            ys = grouped_ffn(xs, w_moe_gu, w_moe_down, ffn_schedule(starts, ends, n_rows))
            h, a = combine_residual_norm(ys, dest, rw, h, next_gain, next_dtype)
    return a.reshape(batch, seq, D)
```

```python
import functools
import math

import numpy as np
import jax
import jax.numpy as jnp
from jax import lax
from jax.experimental import pallas as pl
from jax.experimental.pallas import tpu as pltpu

F32 = jnp.float32
BF16 = jnp.bfloat16

EPS = 1e-6
NEG_BIG = -1e30
MIN_FORGET = 1e-20
ROPE_THETA = 10000.0

HG_HEADS = 8
HG_D = 128
HG_CHUNK = 64
HG_SUB = 16
MLA_HEADS = 8
MLA_RANK = 512
MLA_NOPE = 128
MLA_ROPE = 64
MLA_V = 128
MLA_QK_PAD = 256
MEM_HEADS = 4
MEM_HEAD_DIM = 256
N_EXPERTS = 8
TOP_K = 2

LANES = 128
ATTN_BLOCK = 512
FFN_TM = 256
FFN_RB = 2048
FFN_TF = 256


def _cparams(n_axes, vmem_mb=None):
    kw = dict(dimension_semantics=("arbitrary",) * n_axes)
    if vmem_mb is not None:
        kw["vmem_limit_bytes"] = vmem_mb * 1024 * 1024
    return pltpu.CompilerParams(**kw)


def _sigmoid(x):
    return 1.0 / (1.0 + jnp.exp(-x))


def _dot(a, b):
    return jnp.dot(a, b, preferred_element_type=F32)


def _dot_nt(a, b):
    return lax.dot_general(a, b, (((1,), (1,)), ((), ())), preferred_element_type=F32)


def _rope_kernel(pos_ref, c_ref, cos_ref, sin_ref):
    ang = pos_ref[...].astype(F32) * c_ref[0:1, :]
    cos_ref[...] = jnp.cos(ang) * c_ref[1:2, :]
    sin_ref[...] = jnp.sin(ang) * c_ref[2:3, :]


def rope_tables(positions, tm=1024):
    T = positions.size
    half = MLA_ROPE // 2
    inv = ROPE_THETA ** (-(np.arange(half, dtype=np.float32) / half))
    consts = np.zeros((8, LANES), np.float32)
    consts[0, :half] = inv
    consts[0, half:2 * half] = inv
    consts[1, :2 * half] = 1.0
    consts[2, :half] = -1.0
    consts[2, half:2 * half] = 1.0
    tm = min(tm, T)
    out = jax.ShapeDtypeStruct((T, LANES), F32)
    return pl.pallas_call(
        _rope_kernel,
        grid=(T // tm,),
        in_specs=[pl.BlockSpec((tm, 1), lambda i: (i, 0)),
                  pl.BlockSpec((8, LANES), lambda i: (0, 0))],
        out_specs=[pl.BlockSpec((tm, LANES), lambda i: (i, 0)),
                   pl.BlockSpec((tm, LANES), lambda i: (i, 0))],
        out_shape=[out, out],
        compiler_params=_cparams(1),
        name="rope_tables",
    )(positions.reshape(T, 1), jnp.asarray(consts))


def _rmsnorm_kernel(x_ref, g_ref, o_ref):
    x = x_ref[...].astype(F32)
    y = x * lax.rsqrt(jnp.mean(x * x, axis=-1, keepdims=True) + EPS)
    o_ref[...] = (y * g_ref[...]).astype(o_ref.dtype)


def rmsnorm_cast(x, g, out_dtype, tm=512):
    M, D = x.shape
    tm = min(tm, M)
    return pl.pallas_call(
        _rmsnorm_kernel,
        grid=(M // tm,),
        in_specs=[pl.BlockSpec((tm, D), lambda i: (i, 0)),
                  pl.BlockSpec((1, D), lambda i: (0, 0))],
        out_specs=pl.BlockSpec((tm, D), lambda i: (i, 0)),
        out_shape=jax.ShapeDtypeStruct((M, D), out_dtype),
        compiler_params=_cparams(1),
        name="rmsnorm",
    )(x, g.reshape(1, D).astype(F32))


def _mm_kernel(x_ref, w_ref, o_ref, *scratch, cast):
    if cast:
        (wb,) = scratch

        @pl.when(pl.program_id(1) == 0)
        def _():
            wb[...] = w_ref[...].astype(BF16)

        w = wb[...]
    else:
        w = w_ref[...]
    o_ref[...] = _dot(x_ref[...], w).astype(o_ref.dtype)


def matmul(x, w, *, n_cols, tn, tm, out_dtype, layer=None, name="matmul"):
    M, K = x.shape
    tm = min(tm, M)
    cast = w.dtype != BF16
    if layer is None:
        w_spec = pl.BlockSpec((K, tn), lambda j, i: (0, j))
    else:
        w_spec = pl.BlockSpec((None, K, tn), lambda j, i: (layer, 0, j))
    scratch = [pltpu.VMEM((K, tn), BF16)] if cast else []
    return pl.pallas_call(
        functools.partial(_mm_kernel, cast=cast),
        grid=(n_cols // tn, M // tm),
        in_specs=[pl.BlockSpec((tm, K), lambda j, i: (i, 0)), w_spec],
        out_specs=pl.BlockSpec((tm, tn), lambda j, i: (i, j)),
        out_shape=jax.ShapeDtypeStruct((M, n_cols), out_dtype),
        scratch_shapes=scratch,
        compiler_params=_cparams(2, 48),
        name=name,
    )(x, w)


def _mm_wt_kernel(x_ref, wt_ref, o_ref, wb):
    @pl.when(pl.program_id(1) == 0)
    def _():
        wb[...] = wt_ref[...].T.astype(BF16)

    o_ref[...] = _dot(x_ref[...], wb[...]).astype(o_ref.dtype)


def matmul_wt(x, wt, *, layer, row0, n_cols, tn, tm, out_dtype, name):
    M, K = x.shape
    tm = min(tm, M)
    assert n_cols % tn == 0
    if row0 % tn == 0:
        w_spec = pl.BlockSpec((None, tn, K), lambda j, i: (layer, row0 // tn + j, 0))
    else:
        w_spec = pl.BlockSpec((pl.Squeezed(), pl.Element(tn), pl.Element(K)),
                              lambda j, i: (layer, pl.multiple_of(row0 + j * tn, 8), 0))
    return pl.pallas_call(
        _mm_wt_kernel,
        grid=(n_cols // tn, M // tm),
        in_specs=[pl.BlockSpec((tm, K), lambda j, i: (i, 0)), w_spec],
        out_specs=pl.BlockSpec((tm, tn), lambda j, i: (i, j)),
        out_shape=jax.ShapeDtypeStruct((M, n_cols), out_dtype),
        scratch_shapes=[pltpu.VMEM((K, tn), BF16)],
        compiler_params=_cparams(2, 48),
        name=name,
    )(x, wt)


def _rope_key_weight_kernel(wt_ref, o_ref):
    half = MLA_ROPE // 2
    w = wt_ref[...]
    zero = jnp.zeros((LANES - MLA_ROPE, w.shape[1]), F32)
    rows = jnp.concatenate([w, zero, w[half:], w[:half], zero], axis=0)
    o_ref[...] = rows.T.astype(BF16)


def rope_key_weight(wt, layer, row0):
    K = wt.shape[2]
    assert row0 % MLA_ROPE == 0
    return pl.pallas_call(
        _rope_key_weight_kernel,
        grid=(1,),
        in_specs=[pl.BlockSpec((None, MLA_ROPE, K), lambda i: (layer, row0 // MLA_ROPE, 0))],
        out_specs=pl.BlockSpec((K, 2 * LANES), lambda i: (0, 0)),
        out_shape=jax.ShapeDtypeStruct((K, 2 * LANES), BF16),
        compiler_params=_cparams(1),
        name="rope_key_weight",
    )(wt)


def _hgrn_kernel(q_ref, f_ref, i_ref, g_ref, lb_ref, gn_ref, tri_ref, ones_ref, o_ref, st_ref):
    L, SUB, D = HG_CHUNK, HG_SUB, HG_D

    @pl.when(pl.program_id(1) == 0)
    def _():
        st_ref[...] = jnp.zeros_like(st_ref)

    lb = lb_ref[...]
    f = lb + (1.0 - lb) * _sigmoid(f_ref[...])
    lf = jnp.log(jnp.maximum(f, MIN_FORGET))
    kk = 1.0 - f
    tri = tri_ref[...]
    p1 = lf.astype(BF16)
    r1 = lf - p1.astype(F32)
    p2 = r1.astype(BF16)
    p3 = (r1 - p2.astype(F32)).astype(BF16)
    b = _dot(tri, p1) + _dot(tri, p2) + _dot(tri, p3)

    q = q_ref[...]
    v = i_ref[...]
    g = g_ref[...]
    gn = gn_ref[...]
    ones = ones_ref[...]
    b_last = b[L - 1:L, :]
    q_in = (q * jnp.exp(b)).astype(BF16)
    k_dec = (kk * jnp.exp(b_last - b)).astype(BF16)
    e_last = jnp.exp(b_last)
    row_sub = lax.broadcasted_iota(jnp.int32, (SUB, D), 0)
    row_all = lax.broadcasted_iota(jnp.int32, (L, D), 0)

    def first_matmuls(h):
        hs = slice(h * D, (h + 1) * D)
        bh, qh, kh, vh = b[:, hs], q[:, hs], kk[:, hs], v[:, hs]
        vb = vh.astype(BF16)
        st = st_ref[h]
        inter = _dot_nt(q_in[:, hs], st.astype(BF16))
        st_ref[h] = e_last[:, hs] * st + lax.dot_general(
            vb, k_dec[:, hs], (((0,), (0,)), ((), ())), preferred_element_type=F32)
        ssums, off_parts = [], []
        for i in range(L // SUB):
            r0 = i * SUB
            bs, qs, ks = bh[r0:r0 + SUB], qh[r0:r0 + SUB], kh[r0:r0 + SUB]
            terms = []
            for s in range(SUB):
                d = jnp.where(row_sub >= s, bs - bs[s:s + 1], NEG_BIG)
                terms.append((qs * ks[s:s + 1] * jnp.exp(d)).astype(BF16))
            ssums.append(_dot(jnp.concatenate(terms, axis=0), ones))
            if i == 0:
                off_parts.append(jnp.zeros((SUB, L), F32))
            else:
                ref_row = bh[r0 - 1:r0]
                q_i = (qs * jnp.exp(bs - ref_row)).astype(BF16)
                k_i = (kh * jnp.exp(jnp.where(row_all < r0, ref_row - bh, NEG_BIG))).astype(BF16)
                off_parts.append(_dot_nt(q_i, k_i))
        return inter, ssums, off_parts, vh, vb

    def finish(h, inter, ssums, off_parts, vh, vb):
        hs = slice(h * D, (h + 1) * D)
        diag_parts = []
        for i in range(L // SUB):
            vs = vh[i * SUB:(i + 1) * SUB]
            acc = ssums[i][0:SUB] * vs[0:1]
            for s in range(1, SUB):
                acc = acc + ssums[i][s * SUB:(s + 1) * SUB] * vs[s:s + 1]
            diag_parts.append(acc)
        s_off = jnp.concatenate(off_parts, axis=0).astype(BF16)
        o = inter + jnp.concatenate(diag_parts, axis=0) + _dot(s_off, vb)
        on = o * lax.rsqrt(jnp.mean(o * o, axis=-1, keepdims=True) + EPS) * gn[:, hs]
        gh = g[:, hs]
        o_ref[:, hs] = (on * (gh * _sigmoid(gh))).astype(o_ref.dtype)

    pending = None
    for h in range(HG_HEADS):
        cur = first_matmuls(h)
        if pending is not None:
            finish(h - 1, *pending)
        pending = cur
    finish(HG_HEADS - 1, *pending)


def hgrn_mix(z1, lb, hg_norm, batch, seq):
    T = batch * seq
    W = HG_HEADS * HG_D
    L = HG_CHUNK
    nc = seq // L
    tri = jnp.asarray(np.tril(np.ones((L, L), np.float32)), BF16)
    ones = jnp.ones((HG_D, HG_D), BF16)

    def col(c):
        return pl.BlockSpec((L, W), lambda b, j, c=c: (b * nc + j, c))

    const = lambda shape: pl.BlockSpec(shape, lambda b, j: (0, 0))
    return pl.pallas_call(
        _hgrn_kernel,
        grid=(batch, nc),
        in_specs=[col(0), col(1), col(2), col(3), const((1, W)), const((1, W)),
                  const((L, L)), const((HG_D, HG_D))],
        out_specs=pl.BlockSpec((L, W), lambda b, j: (b * nc + j, 0)),
        out_shape=jax.ShapeDtypeStruct((T, W), BF16),
        scratch_shapes=[pltpu.VMEM((HG_HEADS, HG_D, HG_D), F32)],
        compiler_params=_cparams(2, 48),
        name="hgrn2",
    )(z1, z1, z1, z1, lb.reshape(1, W).astype(F32),
      jnp.tile(hg_norm.astype(F32), HG_HEADS).reshape(1, W), tri, ones)


def _mla_prep_kernel(cq_ref, ckv_ref, kr_ref, cos_ref, sin_ref, gq_ref, gkv_ref,
                     wq1_ref, wq2_ref, wkv_ref, q_ref, kt_ref, v_ref):
    scale = (MLA_NOPE + MLA_ROPE) ** -0.5
    cos = cos_ref[...]
    sin = sin_ref[...]

    def norm(x, g):
        return (x * lax.rsqrt(jnp.mean(x * x, axis=-1, keepdims=True) + EPS) * g).astype(BF16)

    cn = norm(cq_ref[...], gq_ref[...])
    q1 = _dot(cn, wq1_ref[...])
    q2 = _dot(cn, wq2_ref[...])
    cvn = norm(ckv_ref[...], gkv_ref[...])
    kv = _dot(cvn, wkv_ref[...])
    kr = kr_ref[...].astype(F32)
    kr_rot_t = (kr[:, 0:LANES] * cos + kr[:, LANES:2 * LANES] * sin).T.astype(BF16)
    P = MLA_QK_PAD
    for h in range(MLA_HEADS):
        q_ref[:, h * P:h * P + LANES] = (q1[:, h * P:h * P + LANES] * scale).astype(BF16)
        q_ref[:, h * P + LANES:(h + 1) * P] = (
            (q1[:, h * P + LANES:(h + 1) * P] * cos + q2[:, h * LANES:(h + 1) * LANES] * sin) * scale
        ).astype(BF16)
        kt_ref[h * P:h * P + LANES, :] = kv[:, h * LANES:(h + 1) * LANES].T.astype(BF16)
        kt_ref[h * P + LANES:(h + 1) * P, :] = kr_rot_t
    nv = MLA_HEADS * MLA_V
    v_ref[...] = kv[:, nv:2 * nv].astype(BF16)


def mla_prep(z1, z3, cos_t, sin_t, gq, gkv, wq1, wq2, wkv, tm):
    T = z1.shape[0]
    R = MLA_RANK
    cq_blk = (4 * HG_HEADS * HG_D) // R
    row = lambda w: pl.BlockSpec((tm, w), lambda i: (i, 0))
    const = lambda a: pl.BlockSpec(a.shape, lambda i: (0, 0))
    qk_w = MLA_HEADS * MLA_QK_PAD
    return pl.pallas_call(
        _mla_prep_kernel,
        grid=(T // tm,),
        in_specs=[pl.BlockSpec((tm, R), lambda i: (i, cq_blk)),
                  pl.BlockSpec((tm, R), lambda i: (i, cq_blk + 1)),
                  row(2 * LANES), row(LANES), row(LANES),
                  pl.BlockSpec((1, R), lambda i: (0, 0)), pl.BlockSpec((1, R), lambda i: (0, 0)),
                  const(wq1), const(wq2), const(wkv)],
        out_specs=[row(qk_w), pl.BlockSpec((None, qk_w, tm), lambda i: (i, 0, 0)), row(MLA_HEADS * MLA_V)],
        out_shape=[jax.ShapeDtypeStruct((T, qk_w), BF16), jax.ShapeDtypeStruct((T // tm, qk_w, tm), BF16),
                   jax.ShapeDtypeStruct((T, MLA_HEADS * MLA_V), BF16)],
        compiler_params=_cparams(1, 48),
        name="mla_prep",
    )(z1, z1, z3, cos_t, sin_t, gq.reshape(1, R).astype(F32), gkv.reshape(1, R).astype(F32),
      wq1, wq2, wkv)


def mla_weights(w_q_up, w_kv_up):
    R, half = MLA_RANK, MLA_ROPE // 2
    wq = w_q_up.reshape(R, MLA_HEADS, MLA_NOPE + MLA_ROPE)
    nope, rope = wq[..., :MLA_NOPE], wq[..., MLA_NOPE:]
    rope_sw = jnp.concatenate([rope[..., half:], rope[..., :half]], axis=-1)
    z64 = jnp.zeros((R, MLA_HEADS, MLA_QK_PAD - MLA_NOPE - MLA_ROPE), w_q_up.dtype)
    wq1 = jnp.concatenate([nope, rope, z64], axis=-1).reshape(R, MLA_HEADS * MLA_QK_PAD)
    wq2 = jnp.concatenate([rope_sw, z64], axis=-1).reshape(R, MLA_HEADS * LANES)
    wkv = w_kv_up.reshape(R, MLA_HEADS, MLA_NOPE + MLA_V)
    wkv = jnp.concatenate([wkv[..., :MLA_NOPE].reshape(R, -1), wkv[..., MLA_NOPE:].reshape(R, -1)], axis=-1)
    return wq1.astype(BF16), wq2.astype(BF16), wkv.astype(BF16)


def _causal_attn_kernel(q_ref, kt_ref, v_ref, o_ref, m_ref, l_ref, acc_ref, *, tq):
    i = pl.program_id(1)
    H, P, DV = MLA_HEADS, MLA_QK_PAD, MLA_V
    n_c = tq // LANES
    m_ref[...] = jnp.full_like(m_ref, NEG_BIG)
    l_ref[...] = jnp.zeros_like(l_ref)
    acc_ref[...] = jnp.zeros_like(acc_ref)

    def scores(j, h):
        return _dot(q_ref[:, h * P:(h + 1) * P], kt_ref[j, h * P:(h + 1) * P, :])

    def softmax(h, s, masked):
        if masked:
            rows = lax.broadcasted_iota(jnp.int32, (tq, tq), 0)
            cols = lax.broadcasted_iota(jnp.int32, (tq, tq), 1)
            s = jnp.where(cols <= rows, s, NEG_BIG)
        chunks = [s[:, c * LANES:(c + 1) * LANES] for c in range(n_c)]
        m_old = m_ref[h]
        m_new = jnp.maximum(m_old, jnp.max(functools.reduce(jnp.maximum, chunks), axis=-1, keepdims=True))
        ps = [jnp.exp(c - m_new) for c in chunks]
        alpha = jnp.exp(m_old - m_new)
        l_ref[h] = alpha * l_ref[h] + jnp.sum(functools.reduce(jnp.add, ps), axis=-1, keepdims=True)
        m_ref[h] = m_new
        return jnp.concatenate([c.astype(BF16) for c in ps], axis=1), alpha

    def values(j, h, p, alpha):
        start = pl.multiple_of(j * tq, tq)
        acc_ref[h] = alpha * acc_ref[h] + _dot(p, v_ref[pl.ds(start, tq), h * DV:(h + 1) * DV])

    def block(j, masked):
        s, pa = {}, {}
        for step in range(H + 2):
            if step < H:
                s[step] = scores(j, step)
            if 1 <= step <= H:
                pa[step - 1] = softmax(step - 1, s.pop(step - 1), masked)
            if step >= 2:
                values(j, step - 2, *pa.pop(step - 2))

    def body(j, carry):
        block(j, False)
        return carry

    lax.fori_loop(0, i, body, 0)
    block(i, True)
    for h in range(H):
        o_ref[:, h * DV:(h + 1) * DV] = (acc_ref[h] / l_ref[h]).astype(o_ref.dtype)


def causal_attention(qp, kt, v, batch, seq, tq):
    T = batch * seq
    nq = seq // tq
    H, P, DV = MLA_HEADS, MLA_QK_PAD, MLA_V
    return pl.pallas_call(
        functools.partial(_causal_attn_kernel, tq=tq),
        grid=(batch, nq),
        in_specs=[pl.BlockSpec((tq, H * P), lambda b, i: (b * nq + i, 0)),
                  pl.BlockSpec((nq, H * P, tq), lambda b, i: (b, 0, 0)),
                  pl.BlockSpec((seq, H * DV), lambda b, i: (b, 0))],
        out_specs=pl.BlockSpec((tq, H * DV), lambda b, i: (b * nq + i, 0)),
        out_shape=jax.ShapeDtypeStruct((T, H * DV), BF16),
        scratch_shapes=[pltpu.VMEM((H, tq, LANES), F32), pltpu.VMEM((H, tq, LANES), F32),
                        pltpu.VMEM((H, tq, DV), F32)],
        compiler_params=_cparams(2, 48),
        name="mla_attention",
    )(qp, kt, v)


def _mem_attn_kernel(q_ref, k_ref, v_ref, o_ref):
    scale = MEM_HEAD_DIM ** -0.5
    Dh = MEM_HEAD_DIM
    for h in range(MEM_HEADS):
        hs = slice(h * Dh, (h + 1) * Dh)
        s = _dot_nt(q_ref[:, hs], k_ref[:, hs]) * scale
        m = jnp.max(s, axis=-1, keepdims=True)
        p = jnp.exp(s - m)
        l = jnp.sum(p, axis=-1, keepdims=True)
        o = _dot(p.astype(BF16), v_ref[:, hs])
        o_ref[:, hs] = (o / l).astype(o_ref.dtype)


def memory_attention(z2, mem_kv, batch, seq, n_mem, tq=512):
    T = batch * seq
    tq = min(tq, seq)
    nq = seq // tq
    W = MEM_HEADS * MEM_HEAD_DIM
    return pl.pallas_call(
        _mem_attn_kernel,
        grid=(batch, nq),
        in_specs=[pl.BlockSpec((tq, W), lambda b, i: (b * nq + i, 0)),
                  pl.BlockSpec((n_mem, W), lambda b, i: (b, 0)),
                  pl.BlockSpec((n_mem, W), lambda b, i: (b, 1))],
        out_specs=pl.BlockSpec((tq, W), lambda b, i: (b * nq + i, 0)),
        out_shape=jax.ShapeDtypeStruct((T, W), BF16),
        compiler_params=_cparams(2, 48),
        name="mem_attention",
    )(z2, mem_kv, mem_kv)


def _merge_kernel(y0_ref, y1_ref, y2_ref, g0_ref, g1_ref, g2_ref, w_ref, o_ref):
    acc = None
    for n, (y_ref, g_ref) in enumerate(((y0_ref, g0_ref), (y1_ref, g1_ref), (y2_ref, g2_ref))):
        term = _sigmoid(g_ref[...].astype(F32)) * _dot(y_ref[...], w_ref[n])
        acc = term if acc is None else acc + term
    o_ref[...] = acc.astype(o_ref.dtype)


def branch_merge(y_hg, y_mla, y_mem, z2, w_branch, tm=1024, tn=512):
    T, W = y_hg.shape
    D = w_branch.shape[-1]
    tm = min(tm, T)
    g_off = W // tn

    def gate(n):
        return pl.BlockSpec((tm, tn), lambda i, j, n=n: (i, g_off + n * (D // tn) + j))

    y_spec = pl.BlockSpec((tm, W), lambda i, j: (i, 0))
    return pl.pallas_call(
        _merge_kernel,
        grid=(T // tm, D // tn),
        in_specs=[y_spec, y_spec, y_spec, gate(0), gate(1), gate(2),
                  pl.BlockSpec((3, W, tn), lambda i, j: (0, 0, j))],
        out_specs=pl.BlockSpec((tm, tn), lambda i, j: (i, j)),
        out_shape=jax.ShapeDtypeStruct((T, D), BF16),
        compiler_params=_cparams(2, 48),
        name="branch_merge",
    )(y_hg, y_mla, y_mem, z2, z2, z2, w_branch)


def _proj_res_norm_kernel(x_ref, w_ref, h_ref, g_ref, hn_ref, a_ref):
    hn = h_ref[...] + _dot(x_ref[...], w_ref[...])
    hn_ref[...] = hn
    a = hn * lax.rsqrt(jnp.mean(hn * hn, axis=-1, keepdims=True) + EPS) * g_ref[...]
    a_ref[...] = a.astype(a_ref.dtype)


def proj_residual_norm(x, w, h, gain, a_dtype, tm=512):
    T, K = x.shape
    D = w.shape[1]
    tm = min(tm, T)
    row = lambda width: pl.BlockSpec((tm, width), lambda i: (i, 0))
    return pl.pallas_call(
        _proj_res_norm_kernel,
        grid=(T // tm,),
        in_specs=[row(K), pl.BlockSpec((K, D), lambda i: (0, 0)), row(D),
                  pl.BlockSpec((1, D), lambda i: (0, 0))],
        out_specs=[row(D), row(D)],
        out_shape=[jax.ShapeDtypeStruct((T, D), F32), jax.ShapeDtypeStruct((T, D), a_dtype)],
        compiler_params=_cparams(1, 48),
        name="proj_residual_norm",
    )(x, w, h, gain.reshape(1, D).astype(F32))


def _res_norm_kernel(y_ref, h_ref, g_ref, hn_ref, a_ref):
    hn = h_ref[...] + y_ref[...]
    hn_ref[...] = hn
    a = hn * lax.rsqrt(jnp.mean(hn * hn, axis=-1, keepdims=True) + EPS) * g_ref[...]
    a_ref[...] = a.astype(a_ref.dtype)


def residual_norm(y, h, gain, a_dtype, tm=512):
    T, D = y.shape
    tm = min(tm, T)
    row = pl.BlockSpec((tm, D), lambda i: (i, 0))
    return pl.pallas_call(
        _res_norm_kernel,
        grid=(T // tm,),
        in_specs=[row, row, pl.BlockSpec((1, D), lambda i: (0, 0))],
        out_specs=[row, row],
        out_shape=[jax.ShapeDtypeStruct((T, D), F32), jax.ShapeDtypeStruct((T, D), a_dtype)],
        compiler_params=_cparams(1),
        name="residual_norm",
    )(y, h, gain.reshape(1, D).astype(F32))


def _ffn_kernel(sb_ref, ex_ref, jw_ref, rlo_ref, rhi_ref, first_ref,
                x_ref, wg_ref, wu_ref, wd_ref, o_ref, wgb, wub, wdb, *, n_f):
    v = pl.program_id(0)
    j = pl.program_id(1)

    @pl.when(jnp.logical_and(first_ref[v] == 1, j == 0))
    def _():
        o_ref[...] = jnp.zeros_like(o_ref)

    rlo = rlo_ref[v]
    rhi = rhi_ref[v]

    def row_blocks(r, n):
        rows = [pl.ds(pl.multiple_of((r + t) * FFN_TM, FFN_TM), FFN_TM) for t in range(n)]
        hid = []
        for t in range(n):
            xb = x_ref[rows[t], :].astype(BF16)
            gate = _dot(xb, wgb[...])
            up = _dot(xb, wub[...])
            hid.append((gate * _sigmoid(gate) * up).astype(BF16))
        for t in range(n):
            o_ref[rows[t], :] += _dot(hid[t], wdb[...])

    @pl.when(rhi > rlo)
    def _():
        wgb[...] = wg_ref[...].astype(BF16)
        wub[...] = wu_ref[...].astype(BF16)
        wdb[...] = wd_ref[...].astype(BF16)
        row_blocks(rlo, 1)
        rest = rhi - rlo - 1

        def pair(t, carry):
            row_blocks(rlo + 1 + 2 * t, 2)
            return carry

        lax.fori_loop(0, lax.shift_right_logical(rest, 1), pair, 0)

        @pl.when((rest & 1) == 1)
        def _():
            row_blocks(rhi - 1, 1)


def ffn_schedule(pad_start, pad_end, n_rows):
    n_sb = n_rows // FFN_RB
    E = pad_start.shape[0]
    nv = n_sb + E - 1
    sb_lo = jnp.arange(n_sb, dtype=jnp.int32)[:, None] * FFN_RB
    lo = jnp.maximum(pad_start[None, :], sb_lo)
    hi = jnp.minimum(pad_end[None, :], sb_lo + FFN_RB)
    act = hi > lo
    none = jnp.logical_not(jnp.any(act, axis=1))
    cand = act.at[:, 0].set(jnp.logical_or(act[:, 0], none))
    flat = cand.reshape(-1)
    order = jnp.argsort(jnp.logical_not(flat), stable=True)[:nv].astype(jnp.int32)
    n_valid = jnp.sum(flat.astype(jnp.int32))
    valid = jnp.arange(nv, dtype=jnp.int32) < n_valid
    last = order[jnp.maximum(n_valid - 1, 0)]
    order = jnp.where(valid, order, last)
    sb = order // E
    ex = order % E
    work = jnp.logical_and(valid, act.reshape(-1)[order])
    rlo = jnp.where(work, (lo.reshape(-1)[order] - sb * FFN_RB) // FFN_TM, 0).astype(jnp.int32)
    rhi = jnp.where(work, (hi.reshape(-1)[order] - sb * FFN_RB) // FFN_TM, 0).astype(jnp.int32)
    prev_sb = jnp.concatenate([jnp.full((1,), -1, jnp.int32), sb[:-1]])
    first = jnp.logical_and(valid, sb != prev_sb).astype(jnp.int32)
    idx = jnp.arange(nv, dtype=jnp.int32)
    last_work = lax.cummax(jnp.where(work, idx, -1), axis=0)
    ex_eff = jnp.where(last_work >= 0, ex[jnp.maximum(last_work, 0)], ex[0]).astype(jnp.int32)
    return sb.astype(jnp.int32), ex_eff, work.astype(jnp.int32), rlo, rhi, first


def grouped_ffn(x, w_gu, w_down, sched):
    R, D = x.shape
    E, _, F2 = w_gu.shape
    F = F2 // 2
    n_f = F // FFN_TF
    sb, ex, work, rlo, rhi, first = sched
    nv = sb.shape[0]

    def jf(j, work_ref, v):
        return jnp.where(work_ref[v] == 1, j, n_f - 1)

    grid_spec = pltpu.PrefetchScalarGridSpec(
        num_scalar_prefetch=6,
        grid=(nv, n_f),
        in_specs=[
            pl.BlockSpec((FFN_RB, D), lambda v, j, sb, ex, wk, rlo, rhi, fi: (sb[v], 0),
                         pipeline_mode=pl.Buffered(1)),
            pl.BlockSpec((None, D, FFN_TF), lambda v, j, sb, ex, wk, rlo, rhi, fi: (ex[v], 0, jf(j, wk, v))),
            pl.BlockSpec((None, D, FFN_TF), lambda v, j, sb, ex, wk, rlo, rhi, fi: (ex[v], 0, n_f + jf(j, wk, v))),
            pl.BlockSpec((None, FFN_TF, D), lambda v, j, sb, ex, wk, rlo, rhi, fi: (ex[v], jf(j, wk, v), 0)),
        ],
        out_specs=pl.BlockSpec((FFN_RB, D), lambda v, j, sb, ex, wk, rlo, rhi, fi: (sb[v], 0),
                               pipeline_mode=pl.Buffered(1)),
        scratch_shapes=[pltpu.VMEM((D, FFN_TF), BF16), pltpu.VMEM((D, FFN_TF), BF16),
                        pltpu.VMEM((FFN_TF, D), BF16)],
    )
    return pl.pallas_call(
        functools.partial(_ffn_kernel, n_f=n_f),
        grid_spec=grid_spec,
        out_shape=jax.ShapeDtypeStruct((R, D), F32),
        compiler_params=_cparams(2, 56),
        name="grouped_ffn",
    )(sb, ex, work, rlo, rhi, first, x, w_gu, w_gu, w_down)


def _router_kernel(a_ref, w_ref, o_ref):
    logits = _dot(a_ref[...].astype(BF16), w_ref[...])
    lane = lax.broadcasted_iota(jnp.int32, logits.shape, 1)
    lg = jnp.where(lane < N_EXPERTS, logits, NEG_BIG)
    m1 = jnp.max(lg, axis=-1, keepdims=True)
    i1 = jnp.min(jnp.where(lg == m1, lane, LANES), axis=-1, keepdims=True)
    lg2 = jnp.where(lane == i1, NEG_BIG, lg)
    m2 = jnp.max(lg2, axis=-1, keepdims=True)
    i2 = jnp.min(jnp.where(lg2 == m2, lane, LANES), axis=-1, keepdims=True)
    e2 = jnp.exp(m2 - m1)
    w1 = 1.0 / (1.0 + e2)
    w2 = e2 / (1.0 + e2)
    out = jnp.where(lane == 0, i1.astype(F32),
                    jnp.where(lane == 1, i2.astype(F32),
                              jnp.where(lane == 2, w1, jnp.where(lane == 3, w2, 0.0))))
    o_ref[...] = out


def router(a, w_router, tm=1024):
    T, D = a.shape
    tm = min(tm, T)
    w = jnp.zeros((D, LANES), BF16).at[:, :N_EXPERTS].set(w_router.astype(BF16))
    return pl.pallas_call(
        _router_kernel,
        grid=(T // tm,),
        in_specs=[pl.BlockSpec((tm, D), lambda i: (i, 0)), pl.BlockSpec((D, LANES), lambda i: (0, 0))],
        out_specs=pl.BlockSpec((tm, LANES), lambda i: (i, 0)),
        out_shape=jax.ShapeDtypeStruct((T, LANES), F32),
        compiler_params=_cparams(1, 48),
        name="router",
    )(a, w)


ROW_DMA_UNROLL = 8


def _gather_kernel(tok_ref, a_ref, o_ref, sem, *, tm):
    base = pl.program_id(0) * tm

    def issue(g, carry):
        for u in range(ROW_DMA_UNROLL):
            r = base + g * ROW_DMA_UNROLL + u
            pltpu.make_async_copy(a_ref.at[pl.ds(tok_ref[r], 1), :], o_ref.at[pl.ds(r, 1), :], sem).start()
        return carry

    lax.fori_loop(0, tm // ROW_DMA_UNROLL, issue, 0)
    pltpu.make_async_copy(a_ref.at[pl.ds(0, tm), :], o_ref.at[pl.ds(base, tm), :], sem).wait()


def gather_rows(a, slot_tok, tm=1024):
    T, D = a.shape
    R = slot_tok.shape[0]
    tm = min(tm, T)
    grid_spec = pltpu.PrefetchScalarGridSpec(
        num_scalar_prefetch=1,
        grid=(R // tm,),
        in_specs=[pl.BlockSpec(memory_space=pl.ANY)],
        out_specs=pl.BlockSpec(memory_space=pl.ANY),
        scratch_shapes=[pltpu.SemaphoreType.DMA(())],
    )
    return pl.pallas_call(
        functools.partial(_gather_kernel, tm=tm),
        grid_spec=grid_spec,
        out_shape=jax.ShapeDtypeStruct((R, D), a.dtype),
        compiler_params=_cparams(1),
        name="gather_rows",
    )(slot_tok, a)


def _combine_kernel(dest_ref, y_ref, rw_ref, h_ref, g_ref, hn_ref, a_ref, buf, sem, *, tm):
    base = pl.program_id(0) * tm

    def issue(g, carry):
        for u in range(ROW_DMA_UNROLL // TOP_K):
            r = g * (ROW_DMA_UNROLL // TOP_K) + u
            for k in range(TOP_K):
                slot = dest_ref[(base + r) * TOP_K + k]
                pltpu.make_async_copy(y_ref.at[pl.ds(slot, 1), :], buf.at[k, pl.ds(r, 1), :], sem).start()
        return carry

    lax.fori_loop(0, tm // (ROW_DMA_UNROLL // TOP_K), issue, 0)
    for k in range(TOP_K):
        pltpu.make_async_copy(y_ref.at[pl.ds(0, tm), :], buf.at[k], sem).wait()
    rw = rw_ref[...]
    hn = h_ref[...] + rw[:, 2:3] * buf[0] + rw[:, 3:4] * buf[1]
    hn_ref[...] = hn
    a = hn * lax.rsqrt(jnp.mean(hn * hn, axis=-1, keepdims=True) + EPS) * g_ref[...]
    a_ref[...] = a.astype(a_ref.dtype)


def combine_residual_norm(y_sorted, dest, rw, h, gain, a_dtype, tm=256):
    T, D = h.shape
    tm = min(tm, T)
    row = pl.BlockSpec((tm, D), lambda i, d: (i, 0))
    grid_spec = pltpu.PrefetchScalarGridSpec(
        num_scalar_prefetch=1,
        grid=(T // tm,),
        in_specs=[pl.BlockSpec(memory_space=pl.ANY),
                  pl.BlockSpec((tm, LANES), lambda i, d: (i, 0)),
                  row,
                  pl.BlockSpec((1, D), lambda i, d: (0, 0))],
        out_specs=[row, row],
        scratch_shapes=[pltpu.VMEM((TOP_K, tm, D), F32), pltpu.SemaphoreType.DMA(())],
    )
    return pl.pallas_call(
        functools.partial(_combine_kernel, tm=tm),
        grid_spec=grid_spec,
        out_shape=[jax.ShapeDtypeStruct((T, D), F32), jax.ShapeDtypeStruct((T, D), a_dtype)],
        compiler_params=_cparams(1),
        name="combine_residual_norm",
    )(dest, y_sorted, rw, h, gain.reshape(1, D).astype(F32))


def mixer_layer(h, a, l, lb, mem_n, cos_t, sin_t, p, batch, seq, n_mem, next_gain, a_dtype):
    D = h.shape[1]
    n_a = 4 * HG_HEADS * HG_D + 2 * MLA_RANK
    kr0 = n_a
    q_mem0 = kr0 + MLA_ROPE
    z1 = matmul_wt(a, p["w_in_t"], layer=l, row0=0, n_cols=n_a, tn=512, tm=1024, out_dtype=F32,
                   name="in_proj_a")
    n_b = MEM_HEADS * MEM_HEAD_DIM + 3 * D
    z2 = matmul_wt(a, p["w_in_t"], layer=l, row0=q_mem0, n_cols=n_b, tn=512, tm=1024, out_dtype=BF16,
                   name="in_proj_b")
    wq1, wq2, wkv = mla_weights(p["w_q_up"][l], p["w_kv_up"][l])
    wkr = rope_key_weight(p["w_in_t"], l, kr0)
    z3 = matmul(a, wkr, n_cols=2 * LANES, tn=2 * LANES, tm=1024, out_dtype=F32, name="in_proj_kr")

    y_hg = hgrn_mix(z1, lb, p["hg_norm"][l], batch, seq)
    tq = min(ATTN_BLOCK, seq)
    qp, kt, v = mla_prep(z1, z3, cos_t, sin_t, p["mla_q_norm"][l], p["mla_kv_norm"][l], wq1, wq2, wkv, tq)
    y_mla = causal_attention(qp, kt, v, batch, seq, tq)
    mem_kv = matmul(mem_n, p["w_mem_kv"], layer=l, n_cols=2 * MEM_HEADS * MEM_HEAD_DIM, tn=512,
                    tm=1024, out_dtype=BF16, name="mem_kv")
    y_mem = memory_attention(z2, mem_kv, batch, seq, n_mem)
    merged = branch_merge(y_hg, y_mla, y_mem, z2, p["w_branch"][l].astype(BF16))
    return proj_residual_norm(merged, p["w_o"][l].astype(BF16), h, next_gain, a_dtype, tm=256)


def group_bounds(starts, ends, index, n_groups, n_rows):
    E = starts.shape[0]
    before = jnp.zeros((index * E,), jnp.int32)
    after = jnp.full(((n_groups - index - 1) * E,), n_rows, jnp.int32)
    return (jnp.concatenate([before, starts.astype(jnp.int32), after]),
            jnp.concatenate([before, ends.astype(jnp.int32), after]))


def moe_routing(rw, n_tokens):
    A = n_tokens * TOP_K
    flat_e = rw[:, :TOP_K].astype(jnp.int32).reshape(A)
    onehot = (flat_e[:, None] == jnp.arange(N_EXPERTS, dtype=jnp.int32)[None, :]).astype(jnp.int32)
    csum = jnp.cumsum(onehot, axis=0)
    rank = jnp.take_along_axis(csum, flat_e[:, None], axis=1)[:, 0] - 1
    counts = csum[-1]
    padded = (counts + FFN_TM - 1) // FFN_TM * FFN_TM
    pad_end = jnp.cumsum(padded).astype(jnp.int32)
    pad_start = pad_end - padded
    dest = (pad_start[flat_e] + rank).astype(jnp.int32)
    n_rows = (A + N_EXPERTS * FFN_TM + FFN_RB - 1) // FFN_RB * FFN_RB
    slot_tok = jnp.zeros((n_rows,), jnp.int32).at[dest].set(jnp.arange(A, dtype=jnp.int32) // TOP_K)
    return dest, slot_tok, pad_start, pad_end, n_rows


def kernel(x, mem, positions, hg_lb_logits, attn_norm, w_in, hg_norm, mla_q_norm, w_q_up, mla_kv_norm,
           w_kv_up, mem_norm, w_mem_kv, w_branch, w_o, ffn_norm, w_dense_gu, w_dense_down, w_router,
           w_expert_gu, w_expert_down, final_norm):
    batch, seq, D = x.shape
    n_mem = mem.shape[1]
    depth = w_in.shape[0]
    T = batch * seq
    p = dict(w_in_t=jnp.swapaxes(w_in, 1, 2), hg_norm=hg_norm, mla_q_norm=mla_q_norm, w_q_up=w_q_up, mla_kv_norm=mla_kv_norm,
             w_kv_up=w_kv_up, w_mem_kv=w_mem_kv, w_branch=w_branch, w_o=w_o)

    cos_t, sin_t = rope_tables(positions.astype(jnp.int32))
    lb_sm = jax.nn.softmax(hg_lb_logits.astype(F32), axis=0)
    lb_all = jnp.clip(jnp.cumsum(lb_sm, axis=0) - lb_sm[0:1], 0.0, 1.0)
    mem_n = rmsnorm_cast(mem.reshape(batch * n_mem, D), mem_norm, BF16)

    n_dense, n_moe = w_dense_gu.shape[0], w_expert_gu.shape[0]
    w_moe_gu = w_expert_gu.reshape((n_moe * N_EXPERTS,) + w_expert_gu.shape[2:])
    w_moe_down = w_expert_down.reshape((n_moe * N_EXPERTS,) + w_expert_down.shape[2:])

    h = x.reshape(T, D)
    a = rmsnorm_cast(h, attn_norm[0], BF16)
    for l in range(depth):
        moe = l % 2 == 1
        last = l + 1 == depth
        next_gain = final_norm if last else attn_norm[l + 1]
        next_dtype = F32 if last else BF16
        h, a_ffn = mixer_layer(h, a, l, lb_all[l], mem_n, cos_t, sin_t, p, batch, seq, n_mem,
                               ffn_norm[l], F32 if moe else BF16)
        if not moe:
            zero = jnp.zeros((1,), jnp.int32)
            starts, ends = group_bounds(zero, zero + T, l // 2, n_dense, T)
            y = grouped_ffn(a_ffn, w_dense_gu, w_dense_down, ffn_schedule(starts, ends, T))
            h, a = residual_norm(y, h, next_gain, next_dtype)
        else:
            rw = router(a_ffn, w_router[l // 2])
            dest, slot_tok, pad_start, pad_end, n_rows = moe_routing(rw, T)
            xs = gather_rows(a_ffn, slot_tok)
            starts, ends = group_bounds(pad_start, pad_end, l // 2, n_moe, n_rows)
            ys = grouped_ffn(xs, w_moe_gu, w_moe_down, ffn_schedule(starts, ends, n_rows))
            h, a = combine_residual_norm(ys, dest, rw, h, next_gain, next_dtype)
    return a.reshape(batch, seq, D)
```

```python
import functools
import math

import numpy as np
import jax
import jax.numpy as jnp
from jax import lax
from jax.experimental import pallas as pl
from jax.experimental.pallas import tpu as pltpu

F32 = jnp.float32
BF16 = jnp.bfloat16

EPS = 1e-6
NEG_BIG = -1e30
MIN_FORGET = 1e-20
ROPE_THETA = 10000.0

HG_HEADS = 8
HG_D = 128
HG_CHUNK = 64
HG_SUB = 16
MLA_HEADS = 8
MLA_RANK = 512
MLA_NOPE = 128
MLA_ROPE = 64
MLA_V = 128
MLA_QK_PAD = 256
MEM_HEADS = 4
MEM_HEAD_DIM = 256
N_EXPERTS = 8
TOP_K = 2

LOG2E = 1.4426950408889634

LANES = 128
SUBLANES = 8
ATTN_BLOCK = 512
FFN_TM = 256
FFN_RB = 2048
FFN_TF = 256


def _cparams(n_axes, vmem_mb=None):
    kw = dict(dimension_semantics=("arbitrary",) * n_axes)
    if vmem_mb is not None:
        kw["vmem_limit_bytes"] = vmem_mb * 1024 * 1024
    return pltpu.CompilerParams(**kw)


def _sigmoid(x):
    return 1.0 / (1.0 + jnp.exp(-x))


def _dot(a, b):
    return jnp.dot(a, b, preferred_element_type=F32)


def _dot_nt(a, b):
    return lax.dot_general(a, b, (((1,), (1,)), ((), ())), preferred_element_type=F32)


def _rope_kernel(pos_ref, c_ref, cos_ref, sin_ref):
    ang = pos_ref[...].astype(F32) * c_ref[0:1, :]
    cos_ref[...] = jnp.cos(ang) * c_ref[1:2, :]
    sin_ref[...] = jnp.sin(ang) * c_ref[2:3, :]


def rope_tables(positions, tm=1024):
    T = positions.size
    half = MLA_ROPE // 2
    inv = ROPE_THETA ** (-(np.arange(half, dtype=np.float32) / half))
    consts = np.zeros((8, LANES), np.float32)
    consts[0, :half] = inv
    consts[0, half:2 * half] = inv
    consts[1, :2 * half] = 1.0
    consts[2, :half] = -1.0
    consts[2, half:2 * half] = 1.0
    tm = min(tm, T)
    out = jax.ShapeDtypeStruct((T, LANES), F32)
    return pl.pallas_call(
        _rope_kernel,
        grid=(T // tm,),
        in_specs=[pl.BlockSpec((tm, 1), lambda i: (i, 0)),
                  pl.BlockSpec((8, LANES), lambda i: (0, 0))],
        out_specs=[pl.BlockSpec((tm, LANES), lambda i: (i, 0)),
                   pl.BlockSpec((tm, LANES), lambda i: (i, 0))],
        out_shape=[out, out],
        compiler_params=_cparams(1),
        name="rope_tables",
    )(positions.reshape(T, 1), jnp.asarray(consts))


def _rmsnorm_kernel(x_ref, g_ref, o_ref):
    x = x_ref[...].astype(F32)
    y = x * lax.rsqrt(jnp.mean(x * x, axis=-1, keepdims=True) + EPS)
    o_ref[...] = (y * g_ref[...]).astype(o_ref.dtype)


def rmsnorm_cast(x, g, out_dtype, tm=512):
    M, D = x.shape
    tm = min(tm, M)
    return pl.pallas_call(
        _rmsnorm_kernel,
        grid=(M // tm,),
        in_specs=[pl.BlockSpec((tm, D), lambda i: (i, 0)),
                  pl.BlockSpec((1, D), lambda i: (0, 0))],
        out_specs=pl.BlockSpec((tm, D), lambda i: (i, 0)),
        out_shape=jax.ShapeDtypeStruct((M, D), out_dtype),
        compiler_params=_cparams(1),
        name="rmsnorm",
    )(x, g.reshape(1, D).astype(F32))


def _mm_kernel(x_ref, w_ref, o_ref, *scratch, cast):
    if cast:
        (wb,) = scratch

        @pl.when(pl.program_id(1) == 0)
        def _():
            wb[...] = w_ref[...].astype(BF16)

        w = wb[...]
    else:
        w = w_ref[...]
    o_ref[...] = _dot(x_ref[...], w).astype(o_ref.dtype)


def matmul(x, w, *, n_cols, tn, tm, out_dtype, layer=None, name="matmul"):
    M, K = x.shape
    tm = min(tm, M)
    cast = w.dtype != BF16
    if layer is None:
        w_spec = pl.BlockSpec((K, tn), lambda j, i: (0, j))
    else:
        w_spec = pl.BlockSpec((None, K, tn), lambda j, i: (layer, 0, j))
    scratch = [pltpu.VMEM((K, tn), BF16)] if cast else []
    return pl.pallas_call(
        functools.partial(_mm_kernel, cast=cast),
        grid=(n_cols // tn, M // tm),
        in_specs=[pl.BlockSpec((tm, K), lambda j, i: (i, 0)), w_spec],
        out_specs=pl.BlockSpec((tm, tn), lambda j, i: (i, j)),
        out_shape=jax.ShapeDtypeStruct((M, n_cols), out_dtype),
        scratch_shapes=scratch,
        compiler_params=_cparams(2, 48),
        name=name,
    )(x, w)


def _mm_wt_kernel(x_ref, wt_ref, o_ref, wb):
    @pl.when(pl.program_id(1) == 0)
    def _():
        wb[...] = wt_ref[...].T.astype(BF16)

    o_ref[...] = _dot(x_ref[...], wb[...]).astype(o_ref.dtype)


def matmul_wt(x, wt, *, layer, row0, n_cols, tn, tm, out_dtype, name):
    M, K = x.shape
    tm = min(tm, M)
    assert n_cols % tn == 0
    if row0 % tn == 0:
        w_spec = pl.BlockSpec((None, tn, K), lambda j, i: (layer, row0 // tn + j, 0))
    else:
        w_spec = pl.BlockSpec((pl.Squeezed(), pl.Element(tn), pl.Element(K)),
                              lambda j, i: (layer, pl.multiple_of(row0 + j * tn, 8), 0))
    return pl.pallas_call(
        _mm_wt_kernel,
        grid=(n_cols // tn, M // tm),
        in_specs=[pl.BlockSpec((tm, K), lambda j, i: (i, 0)), w_spec],
        out_specs=pl.BlockSpec((tm, tn), lambda j, i: (i, j)),
        out_shape=jax.ShapeDtypeStruct((M, n_cols), out_dtype),
        scratch_shapes=[pltpu.VMEM((K, tn), BF16)],
        compiler_params=_cparams(2, 48),
        name=name,
    )(x, wt)


def _rope_key_weight_kernel(wt_ref, o_ref):
    half = MLA_ROPE // 2
    w = wt_ref[...]
    zero = jnp.zeros((LANES - MLA_ROPE, w.shape[1]), F32)
    rows = jnp.concatenate([w, zero, w[half:], w[:half], zero], axis=0)
    o_ref[...] = rows.T.astype(BF16)


def rope_key_weight(wt, layer, row0):
    K = wt.shape[2]
    assert row0 % MLA_ROPE == 0
    return pl.pallas_call(
        _rope_key_weight_kernel,
        grid=(1,),
        in_specs=[pl.BlockSpec((None, MLA_ROPE, K), lambda i: (layer, row0 // MLA_ROPE, 0))],
        out_specs=pl.BlockSpec((K, 2 * LANES), lambda i: (0, 0)),
        out_shape=jax.ShapeDtypeStruct((K, 2 * LANES), BF16),
        compiler_params=_cparams(1),
        name="rope_key_weight",
    )(wt)


def _hgrn_kernel(q_ref, f_ref, i_ref, g_ref, lb_ref, gn_ref, tri_ref, ones_ref, dmask_ref, o_ref, st_ref):
    L, SUB, D, G = HG_CHUNK, HG_SUB, HG_D, SUBLANES

    @pl.when(pl.program_id(1) == 0)
    def _():
        st_ref[...] = jnp.zeros_like(st_ref)

    lb = lb_ref[...]
    f = lb + (1.0 - lb) * _sigmoid(f_ref[...])
    lf = jnp.log(jnp.maximum(f, MIN_FORGET)) * LOG2E
    kk = 1.0 - f
    tri = tri_ref[...]
    p1 = lf.astype(BF16)
    r1 = lf - p1.astype(F32)
    p2 = r1.astype(BF16)
    p3 = (r1 - p2.astype(F32)).astype(BF16)
    b = _dot(tri, p1) + _dot(tri, p2) + _dot(tri, p3)

    q = q_ref[...]
    v = i_ref[...]
    g = g_ref[...]
    gn = gn_ref[...]
    ones = ones_ref[...]
    b_last = b[L - 1:L, :]
    q_in = (q * jnp.exp2(b)).astype(BF16)
    k_dec = (kk * jnp.exp2(b_last - b)).astype(BF16)
    e_last = jnp.exp2(b_last)
    row_all = lax.broadcasted_iota(jnp.int32, (L, D), 0)

    def first_matmuls(h):
        hs = slice(h * D, (h + 1) * D)
        bh, qh, kh, vh = b[:, hs], q[:, hs], kk[:, hs], v[:, hs]
        vb = vh.astype(BF16)
        st = st_ref[h]
        inter = _dot_nt(q_in[:, hs], st.astype(BF16))
        st_ref[h] = e_last[:, hs] * st + lax.dot_general(
            vb, k_dec[:, hs], (((0,), (0,)), ((), ())), preferred_element_type=F32)
        ssums, off_parts = [], []
        for i in range(L // SUB):
            r0 = i * SUB
            bs, qs = bh[r0:r0 + SUB], qh[r0:r0 + SUB]
            terms = []
            for s in range(SUB):
                b_s = bh[r0 + s:r0 + s + 1]
                k_s = kh[r0 + s:r0 + s + 1]
                for grp in range(s // G, SUB // G):
                    rows = slice(grp * G, (grp + 1) * G)
                    d = bs[rows] - b_s
                    if grp == s // G and s % G:
                        d = d + dmask_ref[s % G]
                    terms.append(qs[rows] * k_s * jnp.exp2(d))
            ssums.append(_dot(jnp.concatenate(terms, axis=0).astype(BF16), ones))
            if i == 0:
                off_parts.append(jnp.zeros((SUB, L), F32))
            else:
                ref_row = bh[r0 - 1:r0]
                q_i = (qs * jnp.exp2(bs - ref_row)).astype(BF16)
                k_i = (kh * jnp.exp2(jnp.where(row_all < r0, ref_row - bh, NEG_BIG))).astype(BF16)
                off_parts.append(_dot_nt(q_i, k_i))
        return inter, ssums, off_parts, vh, vb

    def finish(h, inter, ssums, off_parts, vh, vb):
        hs = slice(h * D, (h + 1) * D)
        diag_parts = []
        for i in range(L // SUB):
            acc = [None] * (SUB // G)
            t = 0
            for s in range(SUB):
                v_s = vh[i * SUB + s:i * SUB + s + 1]
                for grp in range(s // G, SUB // G):
                    term = ssums[i][t * G:(t + 1) * G] * v_s
                    acc[grp] = term if acc[grp] is None else acc[grp] + term
                    t += 1
            diag_parts.extend(acc)
        s_off = jnp.concatenate(off_parts, axis=0).astype(BF16)
        o = inter + jnp.concatenate(diag_parts, axis=0) + _dot(s_off, vb)
        on = o * lax.rsqrt(jnp.mean(o * o, axis=-1, keepdims=True) + EPS) * gn[:, hs]
        gh = g[:, hs]
        o_ref[:, hs] = (on * (gh * _sigmoid(gh))).astype(o_ref.dtype)

    pending = None
    for h in range(HG_HEADS):
        cur = first_matmuls(h)
        if pending is not None:
            finish(h - 1, *pending)
        pending = cur
    finish(HG_HEADS - 1, *pending)


def hgrn_mix(z1, lb, hg_norm, batch, seq):
    T = batch * seq
    W = HG_HEADS * HG_D
    L = HG_CHUNK
    nc = seq // L
    tri = jnp.asarray(np.tril(np.ones((L, L), np.float32)), BF16)
    ones = jnp.ones((HG_D, HG_D), BF16)
    G = SUBLANES
    dmask_np = np.where(np.arange(G)[None, :, None] >= np.arange(G)[:, None, None], 0.0, NEG_BIG)
    dmask = jnp.asarray(np.broadcast_to(dmask_np, (G, G, HG_D)).astype(np.float32))

    def col(c):
        return pl.BlockSpec((L, W), lambda b, j, c=c: (b * nc + j, c))

    const = lambda shape: pl.BlockSpec(shape, lambda b, j: (0,) * len(shape))
    return pl.pallas_call(
        _hgrn_kernel,
        grid=(batch, nc),
        in_specs=[col(0), col(1), col(2), col(3), const((1, W)), const((1, W)),
                  const((L, L)), const((HG_D, HG_D)), const((G, G, HG_D))],
        out_specs=pl.BlockSpec((L, W), lambda b, j: (b * nc + j, 0)),
        out_shape=jax.ShapeDtypeStruct((T, W), BF16),
        scratch_shapes=[pltpu.VMEM((HG_HEADS, HG_D, HG_D), F32)],
        compiler_params=_cparams(2, 48),
        name="hgrn2",
    )(z1, z1, z1, z1, lb.reshape(1, W).astype(F32),
      jnp.tile(hg_norm.astype(F32), HG_HEADS).reshape(1, W), tri, ones, dmask)


def _mla_prep_kernel(cq_ref, ckv_ref, kr_ref, cos_ref, sin_ref, gq_ref, gkv_ref,
                     wq1_ref, wq2_ref, wkv_ref, q_ref, kt_ref, v_ref):
    scale = (MLA_NOPE + MLA_ROPE) ** -0.5
    cos = cos_ref[...]
    sin = sin_ref[...]

    def norm(x, g):
        return (x * lax.rsqrt(jnp.mean(x * x, axis=-1, keepdims=True) + EPS) * g).astype(BF16)

    cn = norm(cq_ref[...], gq_ref[...])
    q1 = _dot(cn, wq1_ref[...])
    q2 = _dot(cn, wq2_ref[...])
    cvn = norm(ckv_ref[...], gkv_ref[...])
    kv = _dot(cvn, wkv_ref[...])
    kr = kr_ref[...].astype(F32)
    kr_rot_t = (kr[:, 0:LANES] * cos + kr[:, LANES:2 * LANES] * sin).T.astype(BF16)
    P = MLA_QK_PAD
    for h in range(MLA_HEADS):
        q_ref[:, h * P:h * P + LANES] = (q1[:, h * P:h * P + LANES] * scale).astype(BF16)
        q_ref[:, h * P + LANES:(h + 1) * P] = (
            (q1[:, h * P + LANES:(h + 1) * P] * cos + q2[:, h * LANES:(h + 1) * LANES] * sin) * scale
        ).astype(BF16)
        kt_ref[h * P:h * P + LANES, :] = kv[:, h * LANES:(h + 1) * LANES].T.astype(BF16)
        kt_ref[h * P + LANES:(h + 1) * P, :] = kr_rot_t
    nv = MLA_HEADS * MLA_V
    v_ref[...] = kv[:, nv:2 * nv].astype(BF16)


def mla_prep(z1, z3, cos_t, sin_t, gq, gkv, wq1, wq2, wkv, tm):
    T = z1.shape[0]
    R = MLA_RANK
    cq_blk = (4 * HG_HEADS * HG_D) // R
    row = lambda w: pl.BlockSpec((tm, w), lambda i: (i, 0))
    const = lambda a: pl.BlockSpec(a.shape, lambda i: (0, 0))
    qk_w = MLA_HEADS * MLA_QK_PAD
    return pl.pallas_call(
        _mla_prep_kernel,
        grid=(T // tm,),
        in_specs=[pl.BlockSpec((tm, R), lambda i: (i, cq_blk)),
                  pl.BlockSpec((tm, R), lambda i: (i, cq_blk + 1)),
                  row(2 * LANES), row(LANES), row(LANES),
                  pl.BlockSpec((1, R), lambda i: (0, 0)), pl.BlockSpec((1, R), lambda i: (0, 0)),
                  const(wq1), const(wq2), const(wkv)],
        out_specs=[row(qk_w), pl.BlockSpec((None, qk_w, tm), lambda i: (i, 0, 0)), row(MLA_HEADS * MLA_V)],
        out_shape=[jax.ShapeDtypeStruct((T, qk_w), BF16), jax.ShapeDtypeStruct((T // tm, qk_w, tm), BF16),
                   jax.ShapeDtypeStruct((T, MLA_HEADS * MLA_V), BF16)],
        compiler_params=_cparams(1, 48),
        name="mla_prep",
    )(z1, z1, z3, cos_t, sin_t, gq.reshape(1, R).astype(F32), gkv.reshape(1, R).astype(F32),
      wq1, wq2, wkv)


def mla_weights(w_q_up, w_kv_up):
    R, half = MLA_RANK, MLA_ROPE // 2
    wq = w_q_up.reshape(R, MLA_HEADS, MLA_NOPE + MLA_ROPE)
    nope, rope = wq[..., :MLA_NOPE], wq[..., MLA_NOPE:]
    rope_sw = jnp.concatenate([rope[..., half:], rope[..., :half]], axis=-1)
    z64 = jnp.zeros((R, MLA_HEADS, MLA_QK_PAD - MLA_NOPE - MLA_ROPE), w_q_up.dtype)
    wq1 = jnp.concatenate([nope, rope, z64], axis=-1).reshape(R, MLA_HEADS * MLA_QK_PAD)
    wq2 = jnp.concatenate([rope_sw, z64], axis=-1).reshape(R, MLA_HEADS * LANES)
    wkv = w_kv_up.reshape(R, MLA_HEADS, MLA_NOPE + MLA_V)
    wkv = jnp.concatenate([wkv[..., :MLA_NOPE].reshape(R, -1), wkv[..., MLA_NOPE:].reshape(R, -1)], axis=-1)
    return wq1.astype(BF16), wq2.astype(BF16), wkv.astype(BF16)


def _causal_attn_kernel(q_ref, kt_ref, v_ref, o_ref, m_ref, l_ref, acc_ref, *, tq):
    i = pl.program_id(1)
    H, P, DV = MLA_HEADS, MLA_QK_PAD, MLA_V
    n_c = tq // LANES
    m_ref[...] = jnp.full_like(m_ref, NEG_BIG)
    l_ref[...] = jnp.zeros_like(l_ref)
    acc_ref[...] = jnp.zeros_like(acc_ref)

    def scores(j, h):
        return _dot(q_ref[:, h * P:(h + 1) * P], kt_ref[j, h * P:(h + 1) * P, :])

    def softmax(h, s, masked):
        if masked:
            rows = lax.broadcasted_iota(jnp.int32, (tq, tq), 0)
            cols = lax.broadcasted_iota(jnp.int32, (tq, tq), 1)
            s = jnp.where(cols <= rows, s, NEG_BIG)
        chunks = [s[:, c * LANES:(c + 1) * LANES] for c in range(n_c)]
        m_old = m_ref[h]
        m_new = jnp.maximum(m_old, jnp.max(functools.reduce(jnp.maximum, chunks), axis=-1, keepdims=True))
        ps = [jnp.exp(c - m_new) for c in chunks]
        alpha = jnp.exp(m_old - m_new)
        l_ref[h] = alpha * l_ref[h] + jnp.sum(functools.reduce(jnp.add, ps), axis=-1, keepdims=True)
        m_ref[h] = m_new
        return jnp.concatenate([c.astype(BF16) for c in ps], axis=1), alpha

    def values(j, h, p, alpha):
        start = pl.multiple_of(j * tq, tq)
        acc_ref[h] = alpha * acc_ref[h] + _dot(p, v_ref[pl.ds(start, tq), h * DV:(h + 1) * DV])

    def block(j, masked):
        s, pa = {}, {}
        for step in range(H + 2):
            if step < H:
                s[step] = scores(j, step)
            if 1 <= step <= H:
                pa[step - 1] = softmax(step - 1, s.pop(step - 1), masked)
            if step >= 2:
                values(j, step - 2, *pa.pop(step - 2))

    def body(j, carry):
        block(j, False)
        return carry

    lax.fori_loop(0, i, body, 0)
    block(i, True)
    for h in range(H):
        o_ref[:, h * DV:(h + 1) * DV] = (acc_ref[h] / l_ref[h]).astype(o_ref.dtype)


def causal_attention(qp, kt, v, batch, seq, tq):
    T = batch * seq
    nq = seq // tq
    H, P, DV = MLA_HEADS, MLA_QK_PAD, MLA_V
    return pl.pallas_call(
        functools.partial(_causal_attn_kernel, tq=tq),
        grid=(batch, nq),
        in_specs=[pl.BlockSpec((tq, H * P), lambda b, i: (b * nq + i, 0)),
                  pl.BlockSpec((nq, H * P, tq), lambda b, i: (b, 0, 0)),
                  pl.BlockSpec((seq, H * DV), lambda b, i: (b, 0))],
        out_specs=pl.BlockSpec((tq, H * DV), lambda b, i: (b * nq + i, 0)),
        out_shape=jax.ShapeDtypeStruct((T, H * DV), BF16),
        scratch_shapes=[pltpu.VMEM((H, tq, LANES), F32), pltpu.VMEM((H, tq, LANES), F32),
                        pltpu.VMEM((H, tq, DV), F32)],
        compiler_params=_cparams(2, 48),
        name="mla_attention",
    )(qp, kt, v)


def _mem_attn_kernel(q_ref, k_ref, v_ref, o_ref):
    scale = MEM_HEAD_DIM ** -0.5
    Dh = MEM_HEAD_DIM
    for h in range(MEM_HEADS):
        hs = slice(h * Dh, (h + 1) * Dh)
        s = _dot_nt(q_ref[:, hs], k_ref[:, hs]) * scale
        m = jnp.max(s, axis=-1, keepdims=True)
        p = jnp.exp(s - m)
        l = jnp.sum(p, axis=-1, keepdims=True)
        o = _dot(p.astype(BF16), v_ref[:, hs])
        o_ref[:, hs] = (o / l).astype(o_ref.dtype)


def memory_attention(z2, mem_kv, batch, seq, n_mem, tq=512):
    T = batch * seq
    tq = min(tq, seq)
    nq = seq // tq
    W = MEM_HEADS * MEM_HEAD_DIM
    return pl.pallas_call(
        _mem_attn_kernel,
        grid=(batch, nq),
        in_specs=[pl.BlockSpec((tq, W), lambda b, i: (b * nq + i, 0)),
                  pl.BlockSpec((n_mem, W), lambda b, i: (b, 0)),
                  pl.BlockSpec((n_mem, W), lambda b, i: (b, 1))],
        out_specs=pl.BlockSpec((tq, W), lambda b, i: (b * nq + i, 0)),
        out_shape=jax.ShapeDtypeStruct((T, W), BF16),
        compiler_params=_cparams(2, 48),
        name="mem_attention",
    )(z2, mem_kv, mem_kv)


def _merge_kernel(y0_ref, y1_ref, y2_ref, g0_ref, g1_ref, g2_ref, w_ref, o_ref):
    acc = None
    for n, (y_ref, g_ref) in enumerate(((y0_ref, g0_ref), (y1_ref, g1_ref), (y2_ref, g2_ref))):
        term = _sigmoid(g_ref[...].astype(F32)) * _dot(y_ref[...], w_ref[n])
        acc = term if acc is None else acc + term
    o_ref[...] = acc.astype(o_ref.dtype)


def branch_merge(y_hg, y_mla, y_mem, z2, w_branch, tm=1024, tn=512):
    T, W = y_hg.shape
    D = w_branch.shape[-1]
    tm = min(tm, T)
    g_off = W // tn

    def gate(n):
        return pl.BlockSpec((tm, tn), lambda i, j, n=n: (i, g_off + n * (D // tn) + j))

    y_spec = pl.BlockSpec((tm, W), lambda i, j: (i, 0))
    return pl.pallas_call(
        _merge_kernel,
        grid=(T // tm, D // tn),
        in_specs=[y_spec, y_spec, y_spec, gate(0), gate(1), gate(2),
                  pl.BlockSpec((3, W, tn), lambda i, j: (0, 0, j))],
        out_specs=pl.BlockSpec((tm, tn), lambda i, j: (i, j)),
        out_shape=jax.ShapeDtypeStruct((T, D), BF16),
        compiler_params=_cparams(2, 48),
        name="branch_merge",
    )(y_hg, y_mla, y_mem, z2, z2, z2, w_branch)


def _proj_res_norm_kernel(x_ref, w_ref, h_ref, g_ref, hn_ref, a_ref):
    hn = h_ref[...] + _dot(x_ref[...], w_ref[...])
    hn_ref[...] = hn
    a = hn * lax.rsqrt(jnp.mean(hn * hn, axis=-1, keepdims=True) + EPS) * g_ref[...]
    a_ref[...] = a.astype(a_ref.dtype)


def proj_residual_norm(x, w, h, gain, a_dtype, tm=512):
    T, K = x.shape
    D = w.shape[1]
    tm = min(tm, T)
    row = lambda width: pl.BlockSpec((tm, width), lambda i: (i, 0))
    return pl.pallas_call(
        _proj_res_norm_kernel,
        grid=(T // tm,),
        in_specs=[row(K), pl.BlockSpec((K, D), lambda i: (0, 0)), row(D),
                  pl.BlockSpec((1, D), lambda i: (0, 0))],
        out_specs=[row(D), row(D)],
        out_shape=[jax.ShapeDtypeStruct((T, D), F32), jax.ShapeDtypeStruct((T, D), a_dtype)],
        compiler_params=_cparams(1, 48),
        name="proj_residual_norm",
    )(x, w, h, gain.reshape(1, D).astype(F32))


def _res_norm_kernel(y_ref, h_ref, g_ref, hn_ref, a_ref):
    hn = h_ref[...] + y_ref[...]
    hn_ref[...] = hn
    a = hn * lax.rsqrt(jnp.mean(hn * hn, axis=-1, keepdims=True) + EPS) * g_ref[...]
    a_ref[...] = a.astype(a_ref.dtype)


def residual_norm(y, h, gain, a_dtype, tm=512):
    T, D = y.shape
    tm = min(tm, T)
    row = pl.BlockSpec((tm, D), lambda i: (i, 0))
    return pl.pallas_call(
        _res_norm_kernel,
        grid=(T // tm,),
        in_specs=[row, row, pl.BlockSpec((1, D), lambda i: (0, 0))],
        out_specs=[row, row],
        out_shape=[jax.ShapeDtypeStruct((T, D), F32), jax.ShapeDtypeStruct((T, D), a_dtype)],
        compiler_params=_cparams(1),
        name="residual_norm",
    )(y, h, gain.reshape(1, D).astype(F32))


def _ffn_kernel(row0_ref, ex_ref, work_ref, nblk_ref, tail_ref,
                x_ref, wg_ref, wu_ref, wd_ref, o_ref, acc, wgb, wub, wdb, sem, *, n_f):
    v = pl.program_id(0)
    j = pl.program_id(1)
    rlo = 0
    rhi = nblk_ref[v]

    @pl.when(jnp.logical_and(rhi > 0, j == 0))
    def _():
        acc[...] = jnp.zeros_like(acc)

    def row_blocks(r, n):
        rows = [pl.ds(pl.multiple_of((r + t) * FFN_TM, FFN_TM), FFN_TM) for t in range(n)]
        hid = []
        for t in range(n):
            xb = x_ref[rows[t], :].astype(BF16)
            gate = _dot(xb, wgb[...])
            up = _dot(xb, wub[...])
            hid.append((gate * _sigmoid(gate) * up).astype(BF16))
        for t in range(n):
            acc[rows[t], :] += _dot(hid[t], wdb[...])

    @pl.when(rhi > rlo)
    def _():
        wgb[...] = wg_ref[...].astype(BF16)
        wub[...] = wu_ref[...].astype(BF16)
        wdb[...] = wd_ref[...].astype(BF16)
        row_blocks(rlo, 1)
        rest = rhi - rlo - 1

        def pair(t, carry):
            row_blocks(rlo + 1 + 2 * t, 2)
            return carry

        lax.fori_loop(0, lax.shift_right_logical(rest, 1), pair, 0)

        @pl.when((rest & 1) == 1)
        def _():
            row_blocks(rhi - 1, 1)

    last_j = j == n_f - 1
    row0 = pl.multiple_of(row0_ref[v], FFN_TM)

    def out_copy(t):
        return pltpu.make_async_copy(acc.at[pl.ds(pl.multiple_of(t * FFN_TM, FFN_TM), FFN_TM), :],
                                     o_ref.at[pl.ds(pl.multiple_of(row0 + t * FFN_TM, FFN_TM), FFN_TM), :], sem)

    def zero_copy(t):
        dst = pl.multiple_of((tail_ref[0] + t) * FFN_TM, FFN_TM)
        return pltpu.make_async_copy(acc.at[pl.ds(0, FFN_TM), :], o_ref.at[pl.ds(dst, FFN_TM), :], sem)

    def run_copies(make, n):
        def start(t, carry):
            make(t).start()
            return carry

        def wait(t, carry):
            make(t).wait()
            return carry

        lax.fori_loop(0, n, start, 0)
        lax.fori_loop(0, n, wait, 0)

    @pl.when(jnp.logical_and(rhi > 0, last_j))
    def _():
        run_copies(out_copy, rhi)

    @pl.when(jnp.logical_and(v == pl.num_programs(0) - 1, last_j))
    def _():
        acc[0:FFN_TM, :] = jnp.zeros((FFN_TM, acc.shape[1]), F32)
        run_copies(zero_copy, tail_ref[1])


def ffn_schedule(pad_start, pad_end, n_rows):
    E = pad_start.shape[0]
    kmax = n_rows // FFN_RB
    nv = min(kmax + E, kmax * E)
    start = pad_start[:, None] + jnp.arange(kmax, dtype=jnp.int32)[None, :] * FFN_RB
    flat = (start < pad_end[:, None]).reshape(-1)
    nblk = jnp.clip((pad_end[:, None] - start) // FFN_TM, 0, FFN_RB // FFN_TM).reshape(-1)
    order = jnp.argsort(jnp.logical_not(flat), stable=True)[:nv].astype(jnp.int32)
    n_valid = jnp.sum(flat.astype(jnp.int32))
    valid = jnp.arange(nv, dtype=jnp.int32) < n_valid
    order = jnp.where(valid, order, order[jnp.maximum(n_valid - 1, 0)])
    row0 = start.reshape(-1)[order].astype(jnp.int32)
    ex = (order // kmax).astype(jnp.int32)
    return row0, ex, valid.astype(jnp.int32), jnp.where(valid, nblk[order], 0).astype(jnp.int32)


def grouped_ffn(x, w_gu, w_down, sched, n_out, rows_end):
    R, D = x.shape
    E, _, F2 = w_gu.shape
    F = F2 // 2
    n_f = F // FFN_TF
    row0, ex, work, nblk = sched
    nv = row0.shape[0]
    tail = jnp.stack([rows_end // FFN_TM, (n_out - rows_end) // FFN_TM]).astype(jnp.int32)

    def jf(j, work_ref, v):
        return jnp.where(work_ref[v] == 1, j, n_f - 1)

    grid_spec = pltpu.PrefetchScalarGridSpec(
        num_scalar_prefetch=5,
        grid=(nv, n_f),
        in_specs=[
            pl.BlockSpec((pl.Element(FFN_RB), pl.Element(D)),
                         lambda v, j, row0, ex, wk, nb, tl: (pl.multiple_of(row0[v], FFN_TM), 0),
                         pipeline_mode=pl.Buffered(1)),
            pl.BlockSpec((None, D, FFN_TF), lambda v, j, row0, ex, wk, nb, tl: (ex[v], 0, jf(j, wk, v))),
            pl.BlockSpec((None, D, FFN_TF), lambda v, j, row0, ex, wk, nb, tl: (ex[v], 0, n_f + jf(j, wk, v))),
            pl.BlockSpec((None, FFN_TF, D), lambda v, j, row0, ex, wk, nb, tl: (ex[v], jf(j, wk, v), 0)),
        ],
        out_specs=pl.BlockSpec(memory_space=pl.ANY),
        scratch_shapes=[pltpu.VMEM((FFN_RB, D), F32),
                        pltpu.VMEM((D, FFN_TF), BF16), pltpu.VMEM((D, FFN_TF), BF16),
                        pltpu.VMEM((FFN_TF, D), BF16), pltpu.SemaphoreType.DMA(())],
    )
    return pl.pallas_call(
        functools.partial(_ffn_kernel, n_f=n_f),
        grid_spec=grid_spec,
        out_shape=jax.ShapeDtypeStruct((n_out, D), F32),
        compiler_params=_cparams(2, 56),
        name="grouped_ffn",
    )(row0, ex, work, nblk, tail, x, w_gu, w_gu, w_down)


def _router_kernel(a_ref, w_ref, o_ref):
    logits = _dot(a_ref[...].astype(BF16), w_ref[...])
    lane = lax.broadcasted_iota(jnp.int32, logits.shape, 1)
    lg = jnp.where(lane < N_EXPERTS, logits, NEG_BIG)
    m1 = jnp.max(lg, axis=-1, keepdims=True)
    i1 = jnp.min(jnp.where(lg == m1, lane, LANES), axis=-1, keepdims=True)
    lg2 = jnp.where(lane == i1, NEG_BIG, lg)
    m2 = jnp.max(lg2, axis=-1, keepdims=True)
    i2 = jnp.min(jnp.where(lg2 == m2, lane, LANES), axis=-1, keepdims=True)
    e2 = jnp.exp(m2 - m1)
    w1 = 1.0 / (1.0 + e2)
    w2 = e2 / (1.0 + e2)
    out = jnp.where(lane == 0, i1.astype(F32),
                    jnp.where(lane == 1, i2.astype(F32),
                              jnp.where(lane == 2, w1, jnp.where(lane == 3, w2, 0.0))))
    o_ref[...] = out


def router(a, w_router, tm=1024):
    T, D = a.shape
    tm = min(tm, T)
    w = jnp.zeros((D, LANES), BF16).at[:, :N_EXPERTS].set(w_router.astype(BF16))
    return pl.pallas_call(
        _router_kernel,
        grid=(T // tm,),
        in_specs=[pl.BlockSpec((tm, D), lambda i: (i, 0)), pl.BlockSpec((D, LANES), lambda i: (0, 0))],
        out_specs=pl.BlockSpec((tm, LANES), lambda i: (i, 0)),
        out_shape=jax.ShapeDtypeStruct((T, LANES), F32),
        compiler_params=_cparams(1, 48),
        name="router",
    )(a, w)


ROW_DMA_UNROLL = 8


def _gather_kernel(tok_ref, a_ref, o_ref, buf, sem, *, tm, n_steps):
    i = pl.program_id(0)

    def issue(step, slot):
        base = step * tm

        def body(g, carry):
            for u in range(ROW_DMA_UNROLL):
                r = g * ROW_DMA_UNROLL + u
                pltpu.make_async_copy(a_ref.at[pl.ds(tok_ref[base + r], 1), :],
                                      buf.at[slot, pl.ds(r, 1), :], sem.at[slot]).start()
            return carry

        lax.fori_loop(0, tm // ROW_DMA_UNROLL, body, 0)

    @pl.when(i == 0)
    def _():
        issue(0, 0)

    @pl.when(i + 1 < n_steps)
    def _():
        issue(i + 1, (i + 1) & 1)

    slot = i & 1
    pltpu.make_async_copy(a_ref.at[pl.ds(0, tm), :], buf.at[slot], sem.at[slot]).wait()
    o_ref[...] = buf[slot].astype(o_ref.dtype)


def gather_rows(a, slot_tok, tm=512):
    T, D = a.shape
    R = slot_tok.shape[0]
    tm = min(tm, T)
    n_steps = R // tm
    grid_spec = pltpu.PrefetchScalarGridSpec(
        num_scalar_prefetch=1,
        grid=(n_steps,),
        in_specs=[pl.BlockSpec(memory_space=pl.ANY)],
        out_specs=pl.BlockSpec((tm, D), lambda i, tok: (i, 0)),
        scratch_shapes=[pltpu.VMEM((2, tm, D), a.dtype), pltpu.SemaphoreType.DMA((2,))],
    )
    return pl.pallas_call(
        functools.partial(_gather_kernel, tm=tm, n_steps=n_steps),
        grid_spec=grid_spec,
        out_shape=jax.ShapeDtypeStruct((R, D), BF16),
        compiler_params=_cparams(1, 48),
        name="gather_rows",
    )(slot_tok, a)


def _combine_kernel(dest_ref, y_ref, rw_ref, h_ref, g_ref, hn_ref, a_ref, buf, sem, *, tm, n_steps):
    i = pl.program_id(0)

    def issue(step, slot):
        base = step * tm

        def body(g, carry):
            for u in range(ROW_DMA_UNROLL // TOP_K):
                r = g * (ROW_DMA_UNROLL // TOP_K) + u
                for k in range(TOP_K):
                    src = dest_ref[(base + r) * TOP_K + k]
                    pltpu.make_async_copy(y_ref.at[pl.ds(src, 1), :], buf.at[slot, k, pl.ds(r, 1), :],
                                          sem.at[slot]).start()
            return carry

        lax.fori_loop(0, tm // (ROW_DMA_UNROLL // TOP_K), body, 0)

    @pl.when(i == 0)
    def _():
        issue(0, 0)

    @pl.when(i + 1 < n_steps)
    def _():
        issue(i + 1, (i + 1) & 1)

    slot = i & 1
    for k in range(TOP_K):
        pltpu.make_async_copy(y_ref.at[pl.ds(0, tm), :], buf.at[slot, k], sem.at[slot]).wait()
    rw = rw_ref[...]
    hn = h_ref[...] + rw[:, 2:3] * buf[slot, 0] + rw[:, 3:4] * buf[slot, 1]
    hn_ref[...] = hn
    a = hn * lax.rsqrt(jnp.mean(hn * hn, axis=-1, keepdims=True) + EPS) * g_ref[...]
    a_ref[...] = a.astype(a_ref.dtype)


def combine_residual_norm(y_sorted, dest, rw, h, gain, a_dtype, tm=256):
    T, D = h.shape
    tm = min(tm, T)
    n_steps = T // tm
    row = pl.BlockSpec((tm, D), lambda i, d: (i, 0))
    grid_spec = pltpu.PrefetchScalarGridSpec(
        num_scalar_prefetch=1,
        grid=(n_steps,),
        in_specs=[pl.BlockSpec(memory_space=pl.ANY),
                  pl.BlockSpec((tm, LANES), lambda i, d: (i, 0)),
                  row,
                  pl.BlockSpec((1, D), lambda i, d: (0, 0))],
        out_specs=[row, row],
        scratch_shapes=[pltpu.VMEM((2, TOP_K, tm, D), F32), pltpu.SemaphoreType.DMA((2,))],
    )
    return pl.pallas_call(
        functools.partial(_combine_kernel, tm=tm, n_steps=n_steps),
        grid_spec=grid_spec,
        out_shape=[jax.ShapeDtypeStruct((T, D), F32), jax.ShapeDtypeStruct((T, D), a_dtype)],
        compiler_params=_cparams(1, 48),
        name="combine_residual_norm",
    )(dest, y_sorted, rw, h, gain.reshape(1, D).astype(F32))


def mixer_layer(h, a, l, lb, mem_n, cos_t, sin_t, p, batch, seq, n_mem, next_gain, a_dtype):
    D = h.shape[1]
    n_a = 4 * HG_HEADS * HG_D + 2 * MLA_RANK
    kr0 = n_a
    q_mem0 = kr0 + MLA_ROPE
    z1 = matmul_wt(a, p["w_in_t"], layer=l, row0=0, n_cols=n_a, tn=1024, tm=1024, out_dtype=F32,
                   name="in_proj_a")
    n_b = MEM_HEADS * MEM_HEAD_DIM + 3 * D
    z2 = matmul_wt(a, p["w_in_t"], layer=l, row0=q_mem0, n_cols=n_b, tn=1024, tm=1024, out_dtype=BF16,
                   name="in_proj_b")
    wq1, wq2, wkv = mla_weights(p["w_q_up"][l], p["w_kv_up"][l])
    wkr = rope_key_weight(p["w_in_t"], l, kr0)
    z3 = matmul(a, wkr, n_cols=2 * LANES, tn=2 * LANES, tm=1024, out_dtype=F32, name="in_proj_kr")

    y_hg = hgrn_mix(z1, lb, p["hg_norm"][l], batch, seq)
    tq = min(ATTN_BLOCK, seq)
    qp, kt, v = mla_prep(z1, z3, cos_t, sin_t, p["mla_q_norm"][l], p["mla_kv_norm"][l], wq1, wq2, wkv, tq)
    y_mla = causal_attention(qp, kt, v, batch, seq, tq)
    mem_kv = matmul(mem_n, p["w_mem_kv"], layer=l, n_cols=2 * MEM_HEADS * MEM_HEAD_DIM, tn=512,
                    tm=1024, out_dtype=BF16, name="mem_kv")
    y_mem = memory_attention(z2, mem_kv, batch, seq, n_mem)
    merged = branch_merge(y_hg, y_mla, y_mem, z2, p["w_branch"][l].astype(BF16))
    return proj_residual_norm(merged, p["w_o"][l].astype(BF16), h, next_gain, a_dtype, tm=256)


def group_bounds(starts, ends, index, n_groups, n_rows):
    E = starts.shape[0]
    before = jnp.zeros((index * E,), jnp.int32)
    after = jnp.full(((n_groups - index - 1) * E,), n_rows, jnp.int32)
    return (jnp.concatenate([before, starts.astype(jnp.int32), after]),
            jnp.concatenate([before, ends.astype(jnp.int32), after]))


def moe_routing(rw, n_tokens):
    A = n_tokens * TOP_K
    flat_e = rw[:, :TOP_K].astype(jnp.int32).reshape(A)
    onehot = (flat_e[:, None] == jnp.arange(N_EXPERTS, dtype=jnp.int32)[None, :]).astype(jnp.int32)
    csum = jnp.cumsum(onehot, axis=0)
    rank = jnp.take_along_axis(csum, flat_e[:, None], axis=1)[:, 0] - 1
    counts = csum[-1]
    padded = (counts + FFN_TM - 1) // FFN_TM * FFN_TM
    pad_end = jnp.cumsum(padded).astype(jnp.int32)
    pad_start = pad_end - padded
    dest = (pad_start[flat_e] + rank).astype(jnp.int32)
    n_rows = (A + N_EXPERTS * FFN_TM + FFN_RB - 1) // FFN_RB * FFN_RB + FFN_RB
    slot_tok = jnp.zeros((n_rows,), jnp.int32).at[dest].set(jnp.arange(A, dtype=jnp.int32) // TOP_K)
    return dest, slot_tok, pad_start, pad_end, n_rows


def kernel(x, mem, positions, hg_lb_logits, attn_norm, w_in, hg_norm, mla_q_norm, w_q_up, mla_kv_norm,
           w_kv_up, mem_norm, w_mem_kv, w_branch, w_o, ffn_norm, w_dense_gu, w_dense_down, w_router,
           w_expert_gu, w_expert_down, final_norm):
    batch, seq, D = x.shape
    n_mem = mem.shape[1]
    depth = w_in.shape[0]
    T = batch * seq
    p = dict(w_in_t=jnp.swapaxes(w_in, 1, 2), hg_norm=hg_norm, mla_q_norm=mla_q_norm, w_q_up=w_q_up, mla_kv_norm=mla_kv_norm,
             w_kv_up=w_kv_up, w_mem_kv=w_mem_kv, w_branch=w_branch, w_o=w_o)

    cos_t, sin_t = rope_tables(positions.astype(jnp.int32))
    lb_sm = jax.nn.softmax(hg_lb_logits.astype(F32), axis=0)
    lb_all = jnp.clip(jnp.cumsum(lb_sm, axis=0) - lb_sm[0:1], 0.0, 1.0)
    mem_n = rmsnorm_cast(mem.reshape(batch * n_mem, D), mem_norm, BF16)

    n_dense, n_moe = w_dense_gu.shape[0], w_expert_gu.shape[0]
    w_moe_gu = w_expert_gu.reshape((n_moe * N_EXPERTS,) + w_expert_gu.shape[2:])
    w_moe_down = w_expert_down.reshape((n_moe * N_EXPERTS,) + w_expert_down.shape[2:])

    h = x.reshape(T, D)
    a = rmsnorm_cast(h, attn_norm[0], BF16)
    for l in range(depth):
        moe = l % 2 == 1
        last = l + 1 == depth
        next_gain = final_norm if last else attn_norm[l + 1]
        next_dtype = F32 if last else BF16
        h, a_ffn = mixer_layer(h, a, l, lb_all[l], mem_n, cos_t, sin_t, p, batch, seq, n_mem,
                               ffn_norm[l], F32 if moe else BF16)
        if not moe:
            zero = jnp.zeros((1,), jnp.int32)
            starts, ends = group_bounds(zero, zero + T, l // 2, n_dense, T)
            y = grouped_ffn(a_ffn, w_dense_gu, w_dense_down, ffn_schedule(starts, ends, T), T,
                            jnp.int32(T))
            h, a = residual_norm(y, h, next_gain, next_dtype)
        else:
            rw = router(a_ffn, w_router[l // 2])
            dest, slot_tok, pad_start, pad_end, n_rows = moe_routing(rw, T)
            xs = gather_rows(a_ffn, slot_tok)
            starts, ends = group_bounds(pad_start, pad_end, l // 2, n_moe, n_rows)
            ys = grouped_ffn(xs, w_moe_gu, w_moe_down, ffn_schedule(starts, ends, n_rows),
                             n_rows - FFN_RB, pad_end[-1])
            h, a = combine_residual_norm(ys, dest, rw, h, next_gain, next_dtype)
    return a.reshape(batch, seq, D)
```

```python
import functools
import math

import numpy as np
import jax
import jax.numpy as jnp
from jax import lax
from jax.experimental import pallas as pl
from jax.experimental.pallas import tpu as pltpu

F32 = jnp.float32
BF16 = jnp.bfloat16

EPS = 1e-6
NEG_BIG = -1e30
MIN_FORGET = 1e-20
ROPE_THETA = 10000.0

HG_HEADS = 8
HG_D = 128
HG_CHUNK = 64
HG_SUB = 16
MLA_HEADS = 8
MLA_RANK = 512
MLA_NOPE = 128
MLA_ROPE = 64
MLA_V = 128
MLA_QK_PAD = 256
MEM_HEADS = 4
MEM_HEAD_DIM = 256
N_EXPERTS = 8
TOP_K = 2

LOG2E = 1.4426950408889634

LANES = 128
SUBLANES = 8
ATTN_BLOCK = 512
FFN_TM = 256
FFN_RB = 2048
FFN_TF = 512


def _cparams(n_axes, vmem_mb=None):
    kw = dict(dimension_semantics=("arbitrary",) * n_axes)
    if vmem_mb is not None:
        kw["vmem_limit_bytes"] = vmem_mb * 1024 * 1024
    return pltpu.CompilerParams(**kw)


def _sigmoid(x):
    return 1.0 / (1.0 + jnp.exp(-x))


def _dot(a, b):
    return jnp.dot(a, b, preferred_element_type=F32)


def _dot_nt(a, b):
    return lax.dot_general(a, b, (((1,), (1,)), ((), ())), preferred_element_type=F32)


def _rope_kernel(pos_ref, c_ref, cos_ref, sin_ref):
    ang = pos_ref[...].astype(F32) * c_ref[0:1, :]
    cos_ref[...] = jnp.cos(ang) * c_ref[1:2, :]
    sin_ref[...] = jnp.sin(ang) * c_ref[2:3, :]


def rope_tables(positions, tm=1024):
    T = positions.size
    half = MLA_ROPE // 2
    inv = ROPE_THETA ** (-(np.arange(half, dtype=np.float32) / half))
    consts = np.zeros((8, LANES), np.float32)
    consts[0, :half] = inv
    consts[0, half:2 * half] = inv
    consts[1, :2 * half] = 1.0
    consts[2, :half] = -1.0
    consts[2, half:2 * half] = 1.0
    tm = min(tm, T)
    out = jax.ShapeDtypeStruct((T, LANES), F32)
    return pl.pallas_call(
        _rope_kernel,
        grid=(T // tm,),
        in_specs=[pl.BlockSpec((tm, 1), lambda i: (i, 0)),
                  pl.BlockSpec((8, LANES), lambda i: (0, 0))],
        out_specs=[pl.BlockSpec((tm, LANES), lambda i: (i, 0)),
                   pl.BlockSpec((tm, LANES), lambda i: (i, 0))],
        out_shape=[out, out],
        compiler_params=_cparams(1),
        name="rope_tables",
    )(positions.reshape(T, 1), jnp.asarray(consts))


def _rmsnorm_kernel(x_ref, g_ref, o_ref):
    x = x_ref[...].astype(F32)
    y = x * lax.rsqrt(jnp.mean(x * x, axis=-1, keepdims=True) + EPS)
    o_ref[...] = (y * g_ref[...]).astype(o_ref.dtype)


def rmsnorm_cast(x, g, out_dtype, tm=512):
    M, D = x.shape
    tm = min(tm, M)
    return pl.pallas_call(
        _rmsnorm_kernel,
        grid=(M // tm,),
        in_specs=[pl.BlockSpec((tm, D), lambda i: (i, 0)),
                  pl.BlockSpec((1, D), lambda i: (0, 0))],
        out_specs=pl.BlockSpec((tm, D), lambda i: (i, 0)),
        out_shape=jax.ShapeDtypeStruct((M, D), out_dtype),
        compiler_params=_cparams(1),
        name="rmsnorm",
    )(x, g.reshape(1, D).astype(F32))


def _mm_kernel(x_ref, w_ref, o_ref, *scratch, cast):
    if cast:
        (wb,) = scratch

        @pl.when(pl.program_id(1) == 0)
        def _():
            wb[...] = w_ref[...].astype(BF16)

        w = wb[...]
    else:
        w = w_ref[...]
    o_ref[...] = _dot(x_ref[...], w).astype(o_ref.dtype)


def matmul(x, w, *, n_cols, tn, tm, out_dtype, layer=None, name="matmul"):
    M, K = x.shape
    tm = min(tm, M)
    cast = w.dtype != BF16
    if layer is None:
        w_spec = pl.BlockSpec((K, tn), lambda j, i: (0, j))
    else:
        w_spec = pl.BlockSpec((None, K, tn), lambda j, i: (layer, 0, j))
    scratch = [pltpu.VMEM((K, tn), BF16)] if cast else []
    return pl.pallas_call(
        functools.partial(_mm_kernel, cast=cast),
        grid=(n_cols // tn, M // tm),
        in_specs=[pl.BlockSpec((tm, K), lambda j, i: (i, 0)), w_spec],
        out_specs=pl.BlockSpec((tm, tn), lambda j, i: (i, j)),
        out_shape=jax.ShapeDtypeStruct((M, n_cols), out_dtype),
        scratch_shapes=scratch,
        compiler_params=_cparams(2, 48),
        name=name,
    )(x, w)


def _mm_wt_kernel(x_ref, wt_ref, o_ref, wb):
    @pl.when(pl.program_id(1) == 0)
    def _():
        wb[...] = wt_ref[...].T.astype(BF16)

    o_ref[...] = _dot(x_ref[...], wb[...]).astype(o_ref.dtype)


def matmul_wt(x, wt, *, layer, row0, n_cols, tn, tm, out_dtype, name):
    M, K = x.shape
    tm = min(tm, M)
    assert n_cols % tn == 0
    if row0 % tn == 0:
        w_spec = pl.BlockSpec((None, tn, K), lambda j, i: (layer, row0 // tn + j, 0))
    else:
        w_spec = pl.BlockSpec((pl.Squeezed(), pl.Element(tn), pl.Element(K)),
                              lambda j, i: (layer, pl.multiple_of(row0 + j * tn, 8), 0))
    return pl.pallas_call(
        _mm_wt_kernel,
        grid=(n_cols // tn, M // tm),
        in_specs=[pl.BlockSpec((tm, K), lambda j, i: (i, 0)), w_spec],
        out_specs=pl.BlockSpec((tm, tn), lambda j, i: (i, j)),
        out_shape=jax.ShapeDtypeStruct((M, n_cols), out_dtype),
        scratch_shapes=[pltpu.VMEM((K, tn), BF16)],
        compiler_params=_cparams(2, 48),
        name=name,
    )(x, wt)


def _rope_key_weight_kernel(wt_ref, o_ref):
    half = MLA_ROPE // 2
    w = wt_ref[...]
    zero = jnp.zeros((LANES - MLA_ROPE, w.shape[1]), F32)
    rows = jnp.concatenate([w, zero, w[half:], w[:half], zero], axis=0)
    o_ref[...] = rows.T.astype(BF16)


def rope_key_weight(wt, layer, row0):
    K = wt.shape[2]
    assert row0 % MLA_ROPE == 0
    return pl.pallas_call(
        _rope_key_weight_kernel,
        grid=(1,),
        in_specs=[pl.BlockSpec((None, MLA_ROPE, K), lambda i: (layer, row0 // MLA_ROPE, 0))],
        out_specs=pl.BlockSpec((K, 2 * LANES), lambda i: (0, 0)),
        out_shape=jax.ShapeDtypeStruct((K, 2 * LANES), BF16),
        compiler_params=_cparams(1),
        name="rope_key_weight",
    )(wt)


def _hgrn_kernel(q_ref, f_ref, i_ref, g_ref, lb_ref, gn_ref, tri_ref, ones_ref, dmask_ref, o_ref,
                 st_ref, bk_scr):
    L, SUB, D, G = HG_CHUNK, HG_SUB, HG_D, SUBLANES

    @pl.when(pl.program_id(1) == 0)
    def _():
        st_ref[...] = jnp.zeros_like(st_ref)

    lb = lb_ref[...]
    f = lb + (1.0 - lb) * _sigmoid(f_ref[...])
    lf = jnp.log(jnp.maximum(f, MIN_FORGET)) * LOG2E
    kk = 1.0 - f
    tri = tri_ref[...]
    p1 = lf.astype(BF16)
    r1 = lf - p1.astype(F32)
    p2 = r1.astype(BF16)
    p3 = (r1 - p2.astype(F32)).astype(BF16)
    b = _dot(tri, p1) + _dot(tri, p2) + _dot(tri, p3)

    q = q_ref[...]
    v = i_ref[...]
    g = g_ref[...]
    gn = gn_ref[...]
    ones = ones_ref[...]
    bk = b - jnp.log2(jnp.maximum(kk, 0.0))
    bk_scr[...] = bk
    b_last = b[L - 1:L, :]
    q_in = (q * jnp.exp2(b)).astype(BF16)
    k_dec = jnp.exp2(b_last - bk).astype(BF16)
    e_last = jnp.exp2(b_last)
    row_all = lax.broadcasted_iota(jnp.int32, (L, D), 0)

    def first_matmuls(h):
        hs = slice(h * D, (h + 1) * D)
        bh, qh, bkh, vh = b[:, hs], q[:, hs], bk[:, hs], v[:, hs]
        vb = vh.astype(BF16)
        st = st_ref[h]
        inter = _dot_nt(q_in[:, hs], st.astype(BF16))
        st_ref[h] = e_last[:, hs] * st + lax.dot_general(
            vb, k_dec[:, hs], (((0,), (0,)), ((), ())), preferred_element_type=F32)
        ssums, off_parts = [], []
        for i in range(L // SUB):
            r0 = i * SUB
            bs, qs = bh[r0:r0 + SUB], qh[r0:r0 + SUB]
            terms = []
            for s in range(SUB):
                bk_s = bk_scr[r0 + s:r0 + s + 1, hs]
                for grp in range(s // G, SUB // G):
                    rows = slice(grp * G, (grp + 1) * G)
                    d = bs[rows] - bk_s
                    if grp == s // G and s % G:
                        d = d + dmask_ref[s % G]
                    terms.append(qs[rows] * jnp.exp2(d))
            ssums.append(_dot(jnp.concatenate(terms, axis=0).astype(BF16), ones))
            if i == 0:
                off_parts.append(jnp.zeros((SUB, L), F32))
            else:
                ref_row = bh[r0 - 1:r0]
                q_i = (qs * jnp.exp2(bs - ref_row)).astype(BF16)
                k_i = jnp.exp2(jnp.where(row_all < r0, ref_row - bkh, NEG_BIG)).astype(BF16)
                off_parts.append(_dot_nt(q_i, k_i))
        return inter, ssums, off_parts, vh, vb

    def finish(h, inter, ssums, off_parts, vh, vb):
        hs = slice(h * D, (h + 1) * D)
        diag_parts = []
        for i in range(L // SUB):
            acc = [None] * (SUB // G)
            t = 0
            for s in range(SUB):
                v_s = i_ref[i * SUB + s:i * SUB + s + 1, hs]
                for grp in range(s // G, SUB // G):
                    term = ssums[i][t * G:(t + 1) * G] * v_s
                    acc[grp] = term if acc[grp] is None else acc[grp] + term
                    t += 1
            diag_parts.extend(acc)
        s_off = jnp.concatenate(off_parts, axis=0).astype(BF16)
        o = inter + jnp.concatenate(diag_parts, axis=0) + _dot(s_off, vb)
        on = o * lax.rsqrt(jnp.mean(o * o, axis=-1, keepdims=True) + EPS) * gn[:, hs]
        gh = g[:, hs]
        o_ref[:, hs] = (on * (gh * _sigmoid(gh))).astype(o_ref.dtype)

    pending = None
    for h in range(HG_HEADS):
        cur = first_matmuls(h)
        if pending is not None:
            finish(h - 1, *pending)
        pending = cur
    finish(HG_HEADS - 1, *pending)


def hgrn_mix(z1, lb, hg_norm, batch, seq):
    T = batch * seq
    W = HG_HEADS * HG_D
    L = HG_CHUNK
    nc = seq // L
    tri = jnp.asarray(np.tril(np.ones((L, L), np.float32)), BF16)
    ones = jnp.ones((HG_D, HG_D), BF16)
    G = SUBLANES
    dmask_np = np.where(np.arange(G)[None, :, None] >= np.arange(G)[:, None, None], 0.0, NEG_BIG)
    dmask = jnp.asarray(np.broadcast_to(dmask_np, (G, G, HG_D)).astype(np.float32))

    def col(c):
        return pl.BlockSpec((L, W), lambda b, j, c=c: (b * nc + j, c))

    const = lambda shape: pl.BlockSpec(shape, lambda b, j: (0,) * len(shape))
    return pl.pallas_call(
        _hgrn_kernel,
        grid=(batch, nc),
        in_specs=[col(0), col(1), col(2), col(3), const((1, W)), const((1, W)),
                  const((L, L)), const((HG_D, HG_D)), const((G, G, HG_D))],
        out_specs=pl.BlockSpec((L, W), lambda b, j: (b * nc + j, 0)),
        out_shape=jax.ShapeDtypeStruct((T, W), BF16),
        scratch_shapes=[pltpu.VMEM((HG_HEADS, HG_D, HG_D), F32), pltpu.VMEM((L, W), F32)],
        compiler_params=_cparams(2, 48),
        name="hgrn2",
    )(z1, z1, z1, z1, lb.reshape(1, W).astype(F32),
      jnp.tile(hg_norm.astype(F32), HG_HEADS).reshape(1, W), tri, ones, dmask)


def _mla_prep_kernel(cq_ref, ckv_ref, kr_ref, cos_ref, sin_ref, gq_ref, gkv_ref,
                     wq1_ref, wq2_ref, wkv_ref, q_ref, kt_ref, v_ref):
    scale = (MLA_NOPE + MLA_ROPE) ** -0.5
    cos = cos_ref[...]
    sin = sin_ref[...]

    def norm(x, g):
        return (x * lax.rsqrt(jnp.mean(x * x, axis=-1, keepdims=True) + EPS) * g).astype(BF16)

    cn = norm(cq_ref[...], gq_ref[...])
    q1 = _dot(cn, wq1_ref[...])
    q2 = _dot(cn, wq2_ref[...])
    cvn = norm(ckv_ref[...], gkv_ref[...])
    kv = _dot(cvn, wkv_ref[...])
    kr = kr_ref[...].astype(F32)
    kr_rot_t = (kr[:, 0:LANES] * cos + kr[:, LANES:2 * LANES] * sin).T.astype(BF16)
    P = MLA_QK_PAD
    for h in range(MLA_HEADS):
        q_ref[:, h * P:h * P + LANES] = (q1[:, h * P:h * P + LANES] * scale).astype(BF16)
        q_ref[:, h * P + LANES:(h + 1) * P] = (
            (q1[:, h * P + LANES:(h + 1) * P] * cos + q2[:, h * LANES:(h + 1) * LANES] * sin) * scale
        ).astype(BF16)
        kt_ref[h * P:h * P + LANES, :] = kv[:, h * LANES:(h + 1) * LANES].T.astype(BF16)
        kt_ref[h * P + LANES:(h + 1) * P, :] = kr_rot_t
    nv = MLA_HEADS * MLA_V
    v_ref[...] = kv[:, nv:2 * nv].astype(BF16)


def mla_prep(z1, z3, cos_t, sin_t, gq, gkv, wq1, wq2, wkv, tm):
    T = z1.shape[0]
    R = MLA_RANK
    cq_blk = (4 * HG_HEADS * HG_D) // R
    row = lambda w: pl.BlockSpec((tm, w), lambda i: (i, 0))
    const = lambda a: pl.BlockSpec(a.shape, lambda i: (0, 0))
    qk_w = MLA_HEADS * MLA_QK_PAD
    return pl.pallas_call(
        _mla_prep_kernel,
        grid=(T // tm,),
        in_specs=[pl.BlockSpec((tm, R), lambda i: (i, cq_blk)),
                  pl.BlockSpec((tm, R), lambda i: (i, cq_blk + 1)),
                  row(2 * LANES), row(LANES), row(LANES),
                  pl.BlockSpec((1, R), lambda i: (0, 0)), pl.BlockSpec((1, R), lambda i: (0, 0)),
                  const(wq1), const(wq2), const(wkv)],
        out_specs=[row(qk_w), pl.BlockSpec((None, qk_w, tm), lambda i: (i, 0, 0)), row(MLA_HEADS * MLA_V)],
        out_shape=[jax.ShapeDtypeStruct((T, qk_w), BF16), jax.ShapeDtypeStruct((T // tm, qk_w, tm), BF16),
                   jax.ShapeDtypeStruct((T, MLA_HEADS * MLA_V), BF16)],
        compiler_params=_cparams(1, 48),
        name="mla_prep",
    )(z1, z1, z3, cos_t, sin_t, gq.reshape(1, R).astype(F32), gkv.reshape(1, R).astype(F32),
      wq1, wq2, wkv)


def mla_weights(w_q_up, w_kv_up):
    R, half = MLA_RANK, MLA_ROPE // 2
    wq = w_q_up.reshape(R, MLA_HEADS, MLA_NOPE + MLA_ROPE)
    nope, rope = wq[..., :MLA_NOPE], wq[..., MLA_NOPE:]
    rope_sw = jnp.concatenate([rope[..., half:], rope[..., :half]], axis=-1)
    z64 = jnp.zeros((R, MLA_HEADS, MLA_QK_PAD - MLA_NOPE - MLA_ROPE), w_q_up.dtype)
    wq1 = jnp.concatenate([nope, rope, z64], axis=-1).reshape(R, MLA_HEADS * MLA_QK_PAD)
    wq2 = jnp.concatenate([rope_sw, z64], axis=-1).reshape(R, MLA_HEADS * LANES)
    wkv = w_kv_up.reshape(R, MLA_HEADS, MLA_NOPE + MLA_V)
    wkv = jnp.concatenate([wkv[..., :MLA_NOPE].reshape(R, -1), wkv[..., MLA_NOPE:].reshape(R, -1)], axis=-1)
    return wq1.astype(BF16), wq2.astype(BF16), wkv.astype(BF16)


def _causal_attn_kernel(q_ref, kt_ref, v_ref, o_ref, m_ref, l_ref, acc_ref, *, tq):
    i = pl.program_id(1)
    H, P, DV = MLA_HEADS, MLA_QK_PAD, MLA_V
    n_c = tq // LANES
    m_ref[...] = jnp.full_like(m_ref, NEG_BIG)
    l_ref[...] = jnp.zeros_like(l_ref)
    acc_ref[...] = jnp.zeros_like(acc_ref)

    def scores(j, h):
        return _dot(q_ref[:, h * P:(h + 1) * P], kt_ref[j, h * P:(h + 1) * P, :])

    def softmax(h, s, masked):
        if masked:
            rows = lax.broadcasted_iota(jnp.int32, (tq, tq), 0)
            cols = lax.broadcasted_iota(jnp.int32, (tq, tq), 1)
            s = jnp.where(cols <= rows, s, NEG_BIG)
        chunks = [s[:, c * LANES:(c + 1) * LANES] for c in range(n_c)]
        m_old = m_ref[h]
        m_new = jnp.maximum(m_old, jnp.max(functools.reduce(jnp.maximum, chunks), axis=-1, keepdims=True))
        ps = [jnp.exp(c - m_new) for c in chunks]
        alpha = jnp.exp(m_old - m_new)
        l_ref[h] = alpha * l_ref[h] + jnp.sum(functools.reduce(jnp.add, ps), axis=-1, keepdims=True)
        m_ref[h] = m_new
        return jnp.concatenate([c.astype(BF16) for c in ps], axis=1), alpha

    def values(j, h, p, alpha):
        start = pl.multiple_of(j * tq, tq)
        acc_ref[h] = alpha * acc_ref[h] + _dot(p, v_ref[pl.ds(start, tq), h * DV:(h + 1) * DV])

    def block(j, masked):
        s, pa = {}, {}
        for step in range(H + 2):
            if step < H:
                s[step] = scores(j, step)
            if 1 <= step <= H:
                pa[step - 1] = softmax(step - 1, s.pop(step - 1), masked)
            if step >= 2:
                values(j, step - 2, *pa.pop(step - 2))

    def body(j, carry):
        block(j, False)
        return carry

    lax.fori_loop(0, i, body, 0)
    block(i, True)
    for h in range(H):
        o_ref[:, h * DV:(h + 1) * DV] = (acc_ref[h] / l_ref[h]).astype(o_ref.dtype)


def causal_attention(qp, kt, v, batch, seq, tq):
    T = batch * seq
    nq = seq // tq
    H, P, DV = MLA_HEADS, MLA_QK_PAD, MLA_V
    return pl.pallas_call(
        functools.partial(_causal_attn_kernel, tq=tq),
        grid=(batch, nq),
        in_specs=[pl.BlockSpec((tq, H * P), lambda b, i: (b * nq + i, 0)),
                  pl.BlockSpec((nq, H * P, tq), lambda b, i: (b, 0, 0)),
                  pl.BlockSpec((seq, H * DV), lambda b, i: (b, 0))],
        out_specs=pl.BlockSpec((tq, H * DV), lambda b, i: (b * nq + i, 0)),
        out_shape=jax.ShapeDtypeStruct((T, H * DV), BF16),
        scratch_shapes=[pltpu.VMEM((H, tq, LANES), F32), pltpu.VMEM((H, tq, LANES), F32),
                        pltpu.VMEM((H, tq, DV), F32)],
        compiler_params=_cparams(2, 48),
        name="mla_attention",
    )(qp, kt, v)


def _mem_attn_kernel(q_ref, k_ref, v_ref, o_ref):
    scale = MEM_HEAD_DIM ** -0.5
    Dh = MEM_HEAD_DIM
    for h in range(MEM_HEADS):
        hs = slice(h * Dh, (h + 1) * Dh)
        s = _dot_nt(q_ref[:, hs], k_ref[:, hs]) * scale
        m = jnp.max(s, axis=-1, keepdims=True)
        p = jnp.exp(s - m)
        l = jnp.sum(p, axis=-1, keepdims=True)
        o = _dot(p.astype(BF16), v_ref[:, hs])
        o_ref[:, hs] = (o / l).astype(o_ref.dtype)


def memory_attention(z2, mem_kv, batch, seq, n_mem, tq=512):
    T = batch * seq
    tq = min(tq, seq)
    nq = seq // tq
    W = MEM_HEADS * MEM_HEAD_DIM
    return pl.pallas_call(
        _mem_attn_kernel,
        grid=(batch, nq),
        in_specs=[pl.BlockSpec((tq, W), lambda b, i: (b * nq + i, 0)),
                  pl.BlockSpec((n_mem, W), lambda b, i: (b, 0)),
                  pl.BlockSpec((n_mem, W), lambda b, i: (b, 1))],
        out_specs=pl.BlockSpec((tq, W), lambda b, i: (b * nq + i, 0)),
        out_shape=jax.ShapeDtypeStruct((T, W), BF16),
        compiler_params=_cparams(2, 48),
        name="mem_attention",
    )(z2, mem_kv, mem_kv)


def _merge_kernel(y0_ref, y1_ref, y2_ref, g0_ref, g1_ref, g2_ref, w_ref, o_ref):
    acc = None
    for n, (y_ref, g_ref) in enumerate(((y0_ref, g0_ref), (y1_ref, g1_ref), (y2_ref, g2_ref))):
        term = _sigmoid(g_ref[...].astype(F32)) * _dot(y_ref[...], w_ref[n])
        acc = term if acc is None else acc + term
    o_ref[...] = acc.astype(o_ref.dtype)


def branch_merge(y_hg, y_mla, y_mem, z2, w_branch, tm=1024, tn=512):
    T, W = y_hg.shape
    D = w_branch.shape[-1]
    tm = min(tm, T)
    g_off = W // tn

    def gate(n):
        return pl.BlockSpec((tm, tn), lambda i, j, n=n: (i, g_off + n * (D // tn) + j))

    y_spec = pl.BlockSpec((tm, W), lambda i, j: (i, 0))
    return pl.pallas_call(
        _merge_kernel,
        grid=(T // tm, D // tn),
        in_specs=[y_spec, y_spec, y_spec, gate(0), gate(1), gate(2),
                  pl.BlockSpec((3, W, tn), lambda i, j: (0, 0, j))],
        out_specs=pl.BlockSpec((tm, tn), lambda i, j: (i, j)),
        out_shape=jax.ShapeDtypeStruct((T, D), BF16),
        compiler_params=_cparams(2, 48),
        name="branch_merge",
    )(y_hg, y_mla, y_mem, z2, z2, z2, w_branch)


def _proj_res_norm_kernel(x_ref, w_ref, h_ref, g_ref, hn_ref, a_ref):
    hn = h_ref[...] + _dot(x_ref[...], w_ref[...])
    hn_ref[...] = hn
    a = hn * lax.rsqrt(jnp.mean(hn * hn, axis=-1, keepdims=True) + EPS) * g_ref[...]
    a_ref[...] = a.astype(a_ref.dtype)


def proj_residual_norm(x, w, h, gain, a_dtype, tm=512):
    T, K = x.shape
    D = w.shape[1]
    tm = min(tm, T)
    row = lambda width: pl.BlockSpec((tm, width), lambda i: (i, 0))
    return pl.pallas_call(
        _proj_res_norm_kernel,
        grid=(T // tm,),
        in_specs=[row(K), pl.BlockSpec((K, D), lambda i: (0, 0)), row(D),
                  pl.BlockSpec((1, D), lambda i: (0, 0))],
        out_specs=[row(D), row(D)],
        out_shape=[jax.ShapeDtypeStruct((T, D), F32), jax.ShapeDtypeStruct((T, D), a_dtype)],
        compiler_params=_cparams(1, 48),
        name="proj_residual_norm",
    )(x, w, h, gain.reshape(1, D).astype(F32))


def _res_norm_kernel(y_ref, h_ref, g_ref, hn_ref, a_ref):
    hn = h_ref[...] + y_ref[...]
    hn_ref[...] = hn
    a = hn * lax.rsqrt(jnp.mean(hn * hn, axis=-1, keepdims=True) + EPS) * g_ref[...]
    a_ref[...] = a.astype(a_ref.dtype)


def residual_norm(y, h, gain, a_dtype, tm=512):
    T, D = y.shape
    tm = min(tm, T)
    row = pl.BlockSpec((tm, D), lambda i: (i, 0))
    return pl.pallas_call(
        _res_norm_kernel,
        grid=(T // tm,),
        in_specs=[row, row, pl.BlockSpec((1, D), lambda i: (0, 0))],
        out_specs=[row, row],
        out_shape=[jax.ShapeDtypeStruct((T, D), F32), jax.ShapeDtypeStruct((T, D), a_dtype)],
        compiler_params=_cparams(1),
        name="residual_norm",
    )(y, h, gain.reshape(1, D).astype(F32))


def _ffn_kernel(row0_ref, ex_ref, work_ref, nblk_ref, tail_ref,
                x_ref, wg_ref, wu_ref, wd_ref, o_ref, acc, wgb, wub, wdb, sem, *, n_f):
    v = pl.program_id(0)
    j = pl.program_id(1)
    rlo = 0
    rhi = nblk_ref[v]

    @pl.when(jnp.logical_and(rhi > 0, j == 0))
    def _():
        acc[...] = jnp.zeros_like(acc)

    def row_blocks(r, n):
        rows = [pl.ds(pl.multiple_of((r + t) * FFN_TM, FFN_TM), FFN_TM) for t in range(n)]
        hid = []
        for t in range(n):
            xb = x_ref[rows[t], :].astype(BF16)
            gate = _dot(xb, wgb[...])
            up = _dot(xb, wub[...])
            hid.append((gate * _sigmoid(gate) * up).astype(BF16))
        for t in range(n):
            acc[rows[t], :] += _dot(hid[t], wdb[...])

    @pl.when(rhi > rlo)
    def _():
        wgb[...] = wg_ref[...].astype(BF16)
        wub[...] = wu_ref[...].astype(BF16)
        wdb[...] = wd_ref[...].astype(BF16)
        row_blocks(rlo, 1)
        rest = rhi - rlo - 1

        def pair(t, carry):
            row_blocks(rlo + 1 + 2 * t, 2)
            return carry

        lax.fori_loop(0, lax.shift_right_logical(rest, 1), pair, 0)

        @pl.when((rest & 1) == 1)
        def _():
            row_blocks(rhi - 1, 1)

    last_j = j == n_f - 1
    row0 = pl.multiple_of(row0_ref[v], FFN_TM)

    def out_copy(t):
        return pltpu.make_async_copy(acc.at[pl.ds(pl.multiple_of(t * FFN_TM, FFN_TM), FFN_TM), :],
                                     o_ref.at[pl.ds(pl.multiple_of(row0 + t * FFN_TM, FFN_TM), FFN_TM), :], sem)

    def zero_copy(t):
        dst = pl.multiple_of((tail_ref[0] + t) * FFN_TM, FFN_TM)
        return pltpu.make_async_copy(acc.at[pl.ds(0, FFN_TM), :], o_ref.at[pl.ds(dst, FFN_TM), :], sem)

    def run_copies(make, n):
        def start(t, carry):
            make(t).start()
            return carry

        def wait(t, carry):
            make(t).wait()
            return carry

        lax.fori_loop(0, n, start, 0)
        lax.fori_loop(0, n, wait, 0)

    @pl.when(jnp.logical_and(rhi > 0, last_j))
    def _():
        run_copies(out_copy, rhi)

    @pl.when(jnp.logical_and(v == pl.num_programs(0) - 1, last_j))
    def _():
        acc[0:FFN_TM, :] = jnp.zeros((FFN_TM, acc.shape[1]), F32)
        run_copies(zero_copy, tail_ref[1])


def ffn_schedule(pad_start, pad_end, n_rows):
    E = pad_start.shape[0]
    kmax = n_rows // FFN_RB
    nv = min(kmax + E, kmax * E)
    nb = (pad_end - pad_start) // FFN_TM
    n_win = (nb + FFN_RB // FFN_TM - 1) // (FFN_RB // FFN_TM)
    per_win = (nb + jnp.maximum(n_win, 1) - 1) // jnp.maximum(n_win, 1)
    k = jnp.arange(kmax, dtype=jnp.int32)[None, :]
    start = pad_start[:, None] + k * per_win[:, None] * FFN_TM
    flat = (k < n_win[:, None]).reshape(-1)
    nblk = jnp.clip(nb[:, None] - k * per_win[:, None], 0, per_win[:, None]).reshape(-1)
    order = jnp.argsort(jnp.logical_not(flat), stable=True)[:nv].astype(jnp.int32)
    n_valid = jnp.sum(flat.astype(jnp.int32))
    valid = jnp.arange(nv, dtype=jnp.int32) < n_valid
    order = jnp.where(valid, order, order[jnp.maximum(n_valid - 1, 0)])
    row0 = start.reshape(-1)[order].astype(jnp.int32)
    ex = (order // kmax).astype(jnp.int32)
    return row0, ex, valid.astype(jnp.int32), jnp.where(valid, nblk[order], 0).astype(jnp.int32)


def grouped_ffn(x, w_gu, w_down, sched, n_out, rows_end):
    R, D = x.shape
    E, _, F2 = w_gu.shape
    F = F2 // 2
    n_f = F // FFN_TF
    row0, ex, work, nblk = sched
    nv = row0.shape[0]
    tail = jnp.stack([rows_end // FFN_TM, (n_out - rows_end) // FFN_TM]).astype(jnp.int32)

    def jf(j, work_ref, v):
        return jnp.where(work_ref[v] == 1, j, n_f - 1)

    grid_spec = pltpu.PrefetchScalarGridSpec(
        num_scalar_prefetch=5,
        grid=(nv, n_f),
        in_specs=[
            pl.BlockSpec((pl.Element(FFN_RB), pl.Element(D)),
                         lambda v, j, row0, ex, wk, nb, tl: (pl.multiple_of(row0[v], FFN_TM), 0),
                         pipeline_mode=pl.Buffered(1)),
            pl.BlockSpec((None, D, FFN_TF), lambda v, j, row0, ex, wk, nb, tl: (ex[v], 0, jf(j, wk, v))),
            pl.BlockSpec((None, D, FFN_TF), lambda v, j, row0, ex, wk, nb, tl: (ex[v], 0, n_f + jf(j, wk, v))),
            pl.BlockSpec((None, FFN_TF, D), lambda v, j, row0, ex, wk, nb, tl: (ex[v], jf(j, wk, v), 0)),
        ],
        out_specs=pl.BlockSpec(memory_space=pl.ANY),
        scratch_shapes=[pltpu.VMEM((FFN_RB, D), F32),
                        pltpu.VMEM((D, FFN_TF), BF16), pltpu.VMEM((D, FFN_TF), BF16),
                        pltpu.VMEM((FFN_TF, D), BF16), pltpu.SemaphoreType.DMA(())],
    )
    return pl.pallas_call(
        functools.partial(_ffn_kernel, n_f=n_f),
        grid_spec=grid_spec,
        out_shape=jax.ShapeDtypeStruct((n_out, D), F32),
        compiler_params=_cparams(2, 60),
        name="grouped_ffn",
    )(row0, ex, work, nblk, tail, x, w_gu, w_gu, w_down)


def _router_kernel(a_ref, w_ref, o_ref):
    logits = _dot(a_ref[...].astype(BF16), w_ref[...])
    lane = lax.broadcasted_iota(jnp.int32, logits.shape, 1)
    lg = jnp.where(lane < N_EXPERTS, logits, NEG_BIG)
    m1 = jnp.max(lg, axis=-1, keepdims=True)
    i1 = jnp.min(jnp.where(lg == m1, lane, LANES), axis=-1, keepdims=True)
    lg2 = jnp.where(lane == i1, NEG_BIG, lg)
    m2 = jnp.max(lg2, axis=-1, keepdims=True)
    i2 = jnp.min(jnp.where(lg2 == m2, lane, LANES), axis=-1, keepdims=True)
    e2 = jnp.exp(m2 - m1)
    w1 = 1.0 / (1.0 + e2)
    w2 = e2 / (1.0 + e2)
    out = jnp.where(lane == 0, i1.astype(F32),
                    jnp.where(lane == 1, i2.astype(F32),
                              jnp.where(lane == 2, w1, jnp.where(lane == 3, w2, 0.0))))
    o_ref[...] = out


def router(a, w_router, tm=1024):
    T, D = a.shape
    tm = min(tm, T)
    w = jnp.zeros((D, LANES), BF16).at[:, :N_EXPERTS].set(w_router.astype(BF16))
    return pl.pallas_call(
        _router_kernel,
        grid=(T // tm,),
        in_specs=[pl.BlockSpec((tm, D), lambda i: (i, 0)), pl.BlockSpec((D, LANES), lambda i: (0, 0))],
        out_specs=pl.BlockSpec((tm, LANES), lambda i: (i, 0)),
        out_shape=jax.ShapeDtypeStruct((T, LANES), F32),
        compiler_params=_cparams(1, 48),
        name="router",
    )(a, w)


ROW_DMA_UNROLL = 8


def _gather_kernel(tok_ref, a_ref, o_ref, buf, sem, *, tm, n_steps):
    i = pl.program_id(0)

    def issue(step, slot):
        base = step * tm

        def body(g, carry):
            for u in range(ROW_DMA_UNROLL):
                r = g * ROW_DMA_UNROLL + u
                pltpu.make_async_copy(a_ref.at[pl.ds(tok_ref[base + r], 1), :],
                                      buf.at[slot, pl.ds(r, 1), :], sem.at[slot]).start()
            return carry

        lax.fori_loop(0, tm // ROW_DMA_UNROLL, body, 0)

    @pl.when(i == 0)
    def _():
        issue(0, 0)

    @pl.when(i + 1 < n_steps)
    def _():
        issue(i + 1, (i + 1) & 1)

    slot = i & 1
    pltpu.make_async_copy(a_ref.at[pl.ds(0, tm), :], buf.at[slot], sem.at[slot]).wait()
    o_ref[...] = buf[slot].astype(o_ref.dtype)


def gather_rows(a, slot_tok, tm=512):
    T, D = a.shape
    R = slot_tok.shape[0]
    tm = min(tm, T)
    n_steps = R // tm
    grid_spec = pltpu.PrefetchScalarGridSpec(
        num_scalar_prefetch=1,
        grid=(n_steps,),
        in_specs=[pl.BlockSpec(memory_space=pl.ANY)],
        out_specs=pl.BlockSpec((tm, D), lambda i, tok: (i, 0)),
        scratch_shapes=[pltpu.VMEM((2, tm, D), a.dtype), pltpu.SemaphoreType.DMA((2,))],
    )
    return pl.pallas_call(
        functools.partial(_gather_kernel, tm=tm, n_steps=n_steps),
        grid_spec=grid_spec,
        out_shape=jax.ShapeDtypeStruct((R, D), BF16),
        compiler_params=_cparams(1, 48),
        name="gather_rows",
    )(slot_tok, a)


def _combine_kernel(dest_ref, y_ref, rw_ref, h_ref, g_ref, hn_ref, a_ref, buf, sem, *, tm, n_steps):
    i = pl.program_id(0)

    def issue(step, slot):
        base = step * tm

        def body(g, carry):
            for u in range(ROW_DMA_UNROLL // TOP_K):
                r = g * (ROW_DMA_UNROLL // TOP_K) + u
                for k in range(TOP_K):
                    src = dest_ref[(base + r) * TOP_K + k]
                    pltpu.make_async_copy(y_ref.at[pl.ds(src, 1), :], buf.at[slot, k, pl.ds(r, 1), :],
                                          sem.at[slot]).start()
            return carry

        lax.fori_loop(0, tm // (ROW_DMA_UNROLL // TOP_K), body, 0)

    @pl.when(i == 0)
    def _():
        issue(0, 0)

    @pl.when(i + 1 < n_steps)
    def _():
        issue(i + 1, (i + 1) & 1)

    slot = i & 1
    for k in range(TOP_K):
        pltpu.make_async_copy(y_ref.at[pl.ds(0, tm), :], buf.at[slot, k], sem.at[slot]).wait()
    rw = rw_ref[...]
    hn = h_ref[...] + rw[:, 2:3] * buf[slot, 0] + rw[:, 3:4] * buf[slot, 1]
    hn_ref[...] = hn
    a = hn * lax.rsqrt(jnp.mean(hn * hn, axis=-1, keepdims=True) + EPS) * g_ref[...]
    a_ref[...] = a.astype(a_ref.dtype)


def combine_residual_norm(y_sorted, dest, rw, h, gain, a_dtype, tm=256):
    T, D = h.shape
    tm = min(tm, T)
    n_steps = T // tm
    row = pl.BlockSpec((tm, D), lambda i, d: (i, 0))
    grid_spec = pltpu.PrefetchScalarGridSpec(
        num_scalar_prefetch=1,
        grid=(n_steps,),
        in_specs=[pl.BlockSpec(memory_space=pl.ANY),
                  pl.BlockSpec((tm, LANES), lambda i, d: (i, 0)),
                  row,
                  pl.BlockSpec((1, D), lambda i, d: (0, 0))],
        out_specs=[row, row],
        scratch_shapes=[pltpu.VMEM((2, TOP_K, tm, D), F32), pltpu.SemaphoreType.DMA((2,))],
    )
    return pl.pallas_call(
        functools.partial(_combine_kernel, tm=tm, n_steps=n_steps),
        grid_spec=grid_spec,
        out_shape=[jax.ShapeDtypeStruct((T, D), F32), jax.ShapeDtypeStruct((T, D), a_dtype)],
        compiler_params=_cparams(1, 48),
        name="combine_residual_norm",
    )(dest, y_sorted, rw, h, gain.reshape(1, D).astype(F32))


def mixer_layer(h, a, l, lb, mem_n, cos_t, sin_t, p, batch, seq, n_mem, next_gain, a_dtype):
    D = h.shape[1]
    n_a = 4 * HG_HEADS * HG_D + 2 * MLA_RANK
    kr0 = n_a
    q_mem0 = kr0 + MLA_ROPE
    z1 = matmul_wt(a, p["w_in_t"], layer=l, row0=0, n_cols=n_a, tn=1024, tm=1024, out_dtype=F32,
                   name="in_proj_a")
    n_b = MEM_HEADS * MEM_HEAD_DIM + 3 * D
    z2 = matmul_wt(a, p["w_in_t"], layer=l, row0=q_mem0, n_cols=n_b, tn=1024, tm=1024, out_dtype=BF16,
                   name="in_proj_b")
    wq1, wq2, wkv = mla_weights(p["w_q_up"][l], p["w_kv_up"][l])
    wkr = rope_key_weight(p["w_in_t"], l, kr0)
    z3 = matmul(a, wkr, n_cols=2 * LANES, tn=2 * LANES, tm=1024, out_dtype=F32, name="in_proj_kr")

    y_hg = hgrn_mix(z1, lb, p["hg_norm"][l], batch, seq)
    tq = min(ATTN_BLOCK, seq)
    qp, kt, v = mla_prep(z1, z3, cos_t, sin_t, p["mla_q_norm"][l], p["mla_kv_norm"][l], wq1, wq2, wkv, tq)
    y_mla = causal_attention(qp, kt, v, batch, seq, tq)
    mem_kv = matmul(mem_n, p["w_mem_kv"], layer=l, n_cols=2 * MEM_HEADS * MEM_HEAD_DIM, tn=512,
                    tm=1024, out_dtype=BF16, name="mem_kv")
    y_mem = memory_attention(z2, mem_kv, batch, seq, n_mem)
    merged = branch_merge(y_hg, y_mla, y_mem, z2, p["w_branch"][l].astype(BF16))
    return proj_residual_norm(merged, p["w_o"][l].astype(BF16), h, next_gain, a_dtype, tm=256)


def group_bounds(starts, ends, index, n_groups, n_rows):
    E = starts.shape[0]
    before = jnp.zeros((index * E,), jnp.int32)
    after = jnp.full(((n_groups - index - 1) * E,), n_rows, jnp.int32)
    return (jnp.concatenate([before, starts.astype(jnp.int32), after]),
            jnp.concatenate([before, ends.astype(jnp.int32), after]))


def moe_routing(rw, n_tokens):
    A = n_tokens * TOP_K
    flat_e = rw[:, :TOP_K].astype(jnp.int32).reshape(A)
    onehot = (flat_e[:, None] == jnp.arange(N_EXPERTS, dtype=jnp.int32)[None, :]).astype(jnp.int32)
    csum = jnp.cumsum(onehot, axis=0)
    rank = jnp.take_along_axis(csum, flat_e[:, None], axis=1)[:, 0] - 1
    counts = csum[-1]
    padded = (counts + FFN_TM - 1) // FFN_TM * FFN_TM
    pad_end = jnp.cumsum(padded).astype(jnp.int32)
    pad_start = pad_end - padded
    dest = (pad_start[flat_e] + rank).astype(jnp.int32)
    n_rows = (A + N_EXPERTS * FFN_TM + FFN_RB - 1) // FFN_RB * FFN_RB + FFN_RB
    filler = jnp.arange(n_rows, dtype=jnp.int32) % n_tokens
    slot_tok = filler.at[dest].set(jnp.arange(A, dtype=jnp.int32) // TOP_K)
    return dest, slot_tok, pad_start, pad_end, n_rows


def kernel(x, mem, positions, hg_lb_logits, attn_norm, w_in, hg_norm, mla_q_norm, w_q_up, mla_kv_norm,
           w_kv_up, mem_norm, w_mem_kv, w_branch, w_o, ffn_norm, w_dense_gu, w_dense_down, w_router,
           w_expert_gu, w_expert_down, final_norm):
    batch, seq, D = x.shape
    n_mem = mem.shape[1]
    depth = w_in.shape[0]
    T = batch * seq
    p = dict(w_in_t=jnp.swapaxes(w_in, 1, 2), hg_norm=hg_norm, mla_q_norm=mla_q_norm, w_q_up=w_q_up, mla_kv_norm=mla_kv_norm,
             w_kv_up=w_kv_up, w_mem_kv=w_mem_kv, w_branch=w_branch, w_o=w_o)

    cos_t, sin_t = rope_tables(positions.astype(jnp.int32))
    lb_sm = jax.nn.softmax(hg_lb_logits.astype(F32), axis=0)
    lb_all = jnp.clip(jnp.cumsum(lb_sm, axis=0) - lb_sm[0:1], 0.0, 1.0)
    mem_n = rmsnorm_cast(mem.reshape(batch * n_mem, D), mem_norm, BF16)

    n_dense, n_moe = w_dense_gu.shape[0], w_expert_gu.shape[0]
    w_moe_gu = w_expert_gu.reshape((n_moe * N_EXPERTS,) + w_expert_gu.shape[2:])
    w_moe_down = w_expert_down.reshape((n_moe * N_EXPERTS,) + w_expert_down.shape[2:])

    h = x.reshape(T, D)
    a = rmsnorm_cast(h, attn_norm[0], BF16)
    for l in range(depth):
        moe = l % 2 == 1
        last = l + 1 == depth
        next_gain = final_norm if last else attn_norm[l + 1]
        next_dtype = F32 if last else BF16
        h, a_ffn = mixer_layer(h, a, l, lb_all[l], mem_n, cos_t, sin_t, p, batch, seq, n_mem,
                               ffn_norm[l], F32 if moe else BF16)
        if not moe:
            zero = jnp.zeros((1,), jnp.int32)
            starts, ends = group_bounds(zero, zero + T, l // 2, n_dense, T)
            y = grouped_ffn(a_ffn, w_dense_gu, w_dense_down, ffn_schedule(starts, ends, T), T,
                            jnp.int32(T))
            h, a = residual_norm(y, h, next_gain, next_dtype)
        else:
            rw = router(a_ffn, w_router[l // 2])
            dest, slot_tok, pad_start, pad_end, n_rows = moe_routing(rw, T)
            xs = gather_rows(a_ffn, slot_tok)
            starts, ends = group_bounds(pad_start, pad_end, l // 2, n_moe, n_rows)
            ys = grouped_ffn(xs, w_moe_gu, w_moe_down, ffn_schedule(starts, ends, n_rows),
                             n_rows - FFN_RB, pad_end[-1])
            h, a = combine_residual_norm(ys, dest, rw, h, next_gain, next_dtype)
    return a.reshape(batch, seq, D)
```

```python
import functools
import math

import numpy as np
import jax
import jax.numpy as jnp
from jax import lax
from jax.experimental import pallas as pl
from jax.experimental.pallas import tpu as pltpu

F32 = jnp.float32
BF16 = jnp.bfloat16

EPS = 1e-6
NEG_BIG = -1e30
MIN_FORGET = 1e-20
ROPE_THETA = 10000.0

HG_HEADS = 8
HG_D = 128
HG_CHUNK = 64
HG_SUB = 16
MLA_HEADS = 8
MLA_RANK = 512
MLA_NOPE = 128
MLA_ROPE = 64
MLA_V = 128
MLA_QK_PAD = 256
MEM_HEADS = 4
MEM_HEAD_DIM = 256
N_EXPERTS = 8
TOP_K = 2

LOG2E = 1.4426950408889634

LANES = 128
SUBLANES = 8
ATTN_BLOCK = 512
FFN_TM = 256
FFN_RB = 2048
FFN_TF = 512


def _cparams(n_axes, vmem_mb=None):
    kw = dict(dimension_semantics=("arbitrary",) * n_axes)
    if vmem_mb is not None:
        kw["vmem_limit_bytes"] = vmem_mb * 1024 * 1024
    return pltpu.CompilerParams(**kw)


def _sigmoid(x):
    return 1.0 / (1.0 + jnp.exp(-x))


def _dot(a, b):
    return jnp.dot(a, b, preferred_element_type=F32)


def _dot_nt(a, b):
    return lax.dot_general(a, b, (((1,), (1,)), ((), ())), preferred_element_type=F32)


def _rope_kernel(pos_ref, c_ref, cos_ref, sin_ref):
    ang = pos_ref[...].astype(F32) * c_ref[0:1, :]
    cos_ref[...] = jnp.cos(ang) * c_ref[1:2, :]
    sin_ref[...] = jnp.sin(ang) * c_ref[2:3, :]


def rope_tables(positions, tm=1024):
    T = positions.size
    half = MLA_ROPE // 2
    inv = ROPE_THETA ** (-(np.arange(half, dtype=np.float32) / half))
    consts = np.zeros((8, LANES), np.float32)
    consts[0, :half] = inv
    consts[0, half:2 * half] = inv
    consts[1, :2 * half] = 1.0
    consts[2, :half] = -1.0
    consts[2, half:2 * half] = 1.0
    tm = min(tm, T)
    out = jax.ShapeDtypeStruct((T, LANES), F32)
    return pl.pallas_call(
        _rope_kernel,
        grid=(T // tm,),
        in_specs=[pl.BlockSpec((tm, 1), lambda i: (i, 0)),
                  pl.BlockSpec((8, LANES), lambda i: (0, 0))],
        out_specs=[pl.BlockSpec((tm, LANES), lambda i: (i, 0)),
                   pl.BlockSpec((tm, LANES), lambda i: (i, 0))],
        out_shape=[out, out],
        compiler_params=_cparams(1),
        name="rope_tables",
    )(positions.reshape(T, 1), jnp.asarray(consts))


def _rmsnorm_kernel(x_ref, g_ref, o_ref):
    x = x_ref[...].astype(F32)
    y = x * lax.rsqrt(jnp.mean(x * x, axis=-1, keepdims=True) + EPS)
    o_ref[...] = (y * g_ref[...]).astype(o_ref.dtype)


def rmsnorm_cast(x, g, out_dtype, tm=512):
    M, D = x.shape
    tm = min(tm, M)
    return pl.pallas_call(
        _rmsnorm_kernel,
        grid=(M // tm,),
        in_specs=[pl.BlockSpec((tm, D), lambda i: (i, 0)),
                  pl.BlockSpec((1, D), lambda i: (0, 0))],
        out_specs=pl.BlockSpec((tm, D), lambda i: (i, 0)),
        out_shape=jax.ShapeDtypeStruct((M, D), out_dtype),
        compiler_params=_cparams(1),
        name="rmsnorm",
    )(x, g.reshape(1, D).astype(F32))


def _mm_kernel(x_ref, w_ref, o_ref, *scratch, cast):
    if cast:
        (wb,) = scratch

        @pl.when(pl.program_id(1) == 0)
        def _():
            wb[...] = w_ref[...].astype(BF16)

        w = wb[...]
    else:
        w = w_ref[...]
    o_ref[...] = _dot(x_ref[...], w).astype(o_ref.dtype)


def matmul(x, w, *, n_cols, tn, tm, out_dtype, layer=None, name="matmul"):
    M, K = x.shape
    tm = min(tm, M)
    cast = w.dtype != BF16
    if layer is None:
        w_spec = pl.BlockSpec((K, tn), lambda j, i: (0, j))
    else:
        w_spec = pl.BlockSpec((None, K, tn), lambda j, i: (layer, 0, j))
    scratch = [pltpu.VMEM((K, tn), BF16)] if cast else []
    return pl.pallas_call(
        functools.partial(_mm_kernel, cast=cast),
        grid=(n_cols // tn, M // tm),
        in_specs=[pl.BlockSpec((tm, K), lambda j, i: (i, 0)), w_spec],
        out_specs=pl.BlockSpec((tm, tn), lambda j, i: (i, j)),
        out_shape=jax.ShapeDtypeStruct((M, n_cols), out_dtype),
        scratch_shapes=scratch,
        compiler_params=_cparams(2, 48),
        name=name,
    )(x, w)


def _mm_wt_kernel(x_ref, wt_ref, o_ref, wb):
    @pl.when(pl.program_id(1) == 0)
    def _():
        wb[...] = wt_ref[...].T.astype(BF16)

    o_ref[...] = _dot(x_ref[...], wb[...]).astype(o_ref.dtype)


def matmul_wt(x, wt, *, layer, row0, n_cols, tn, tm, out_dtype, name):
    M, K = x.shape
    tm = min(tm, M)
    assert n_cols % tn == 0
    if row0 % tn == 0:
        w_spec = pl.BlockSpec((None, tn, K), lambda j, i: (layer, row0 // tn + j, 0))
    else:
        w_spec = pl.BlockSpec((pl.Squeezed(), pl.Element(tn), pl.Element(K)),
                              lambda j, i: (layer, pl.multiple_of(row0 + j * tn, 8), 0))
    return pl.pallas_call(
        _mm_wt_kernel,
        grid=(n_cols // tn, M // tm),
        in_specs=[pl.BlockSpec((tm, K), lambda j, i: (i, 0)), w_spec],
        out_specs=pl.BlockSpec((tm, tn), lambda j, i: (i, j)),
        out_shape=jax.ShapeDtypeStruct((M, n_cols), out_dtype),
        scratch_shapes=[pltpu.VMEM((K, tn), BF16)],
        compiler_params=_cparams(2, 48),
        name=name,
    )(x, wt)


def _rope_key_weight_kernel(wt_ref, o_ref):
    half = MLA_ROPE // 2
    w = wt_ref[...]
    zero = jnp.zeros((LANES - MLA_ROPE, w.shape[1]), F32)
    rows = jnp.concatenate([w, zero, w[half:], w[:half], zero], axis=0)
    o_ref[...] = rows.T.astype(BF16)


def rope_key_weight(wt, layer, row0):
    K = wt.shape[2]
    assert row0 % MLA_ROPE == 0
    return pl.pallas_call(
        _rope_key_weight_kernel,
        grid=(1,),
        in_specs=[pl.BlockSpec((None, MLA_ROPE, K), lambda i: (layer, row0 // MLA_ROPE, 0))],
        out_specs=pl.BlockSpec((K, 2 * LANES), lambda i: (0, 0)),
        out_shape=jax.ShapeDtypeStruct((K, 2 * LANES), BF16),
        compiler_params=_cparams(1),
        name="rope_key_weight",
    )(wt)


def _hgrn_kernel(q_ref, f_ref, i_ref, g_ref, lb_ref, gn_ref, tri_ref, ones_ref, dmask_ref, o_ref,
                 st_ref, bk_scr):
    L, SUB, D, G = HG_CHUNK, HG_SUB, HG_D, SUBLANES

    @pl.when(pl.program_id(1) == 0)
    def _():
        st_ref[...] = jnp.zeros_like(st_ref)

    lb = lb_ref[...]
    f = lb + (1.0 - lb) * _sigmoid(f_ref[...])
    lf = jnp.log(jnp.maximum(f, MIN_FORGET)) * LOG2E
    kk = 1.0 - f
    tri = tri_ref[...]
    p1 = lf.astype(BF16)
    r1 = lf - p1.astype(F32)
    p2 = r1.astype(BF16)
    p3 = (r1 - p2.astype(F32)).astype(BF16)
    b = _dot(tri, p1) + _dot(tri, p2) + _dot(tri, p3)

    q = q_ref[...]
    v = i_ref[...]
    g = g_ref[...]
    gn = gn_ref[...]
    ones = ones_ref[...]
    bk = b - jnp.log2(jnp.maximum(kk, 0.0))
    bk_scr[...] = bk
    b_last = b[L - 1:L, :]
    q_in = (q * jnp.exp2(b)).astype(BF16)
    k_dec = jnp.exp2(b_last - bk).astype(BF16)
    e_last = jnp.exp2(b_last)
    row_all = lax.broadcasted_iota(jnp.int32, (L, D), 0)

    def first_matmuls(h):
        hs = slice(h * D, (h + 1) * D)
        bh, qh, bkh, vh = b[:, hs], q[:, hs], bk[:, hs], v[:, hs]
        vb = vh.astype(BF16)
        st = st_ref[h]
        inter = _dot_nt(q_in[:, hs], st.astype(BF16))
        st_ref[h] = e_last[:, hs] * st + lax.dot_general(
            vb, k_dec[:, hs], (((0,), (0,)), ((), ())), preferred_element_type=F32)
        ssums, off_parts = [], []
        for i in range(L // SUB):
            r0 = i * SUB
            bs, qs = bh[r0:r0 + SUB], qh[r0:r0 + SUB]
            terms = []
            for s in range(SUB):
                bk_s = bk_scr[r0 + s:r0 + s + 1, hs]
                for grp in range(s // G, SUB // G):
                    rows = slice(grp * G, (grp + 1) * G)
                    d = bs[rows] - bk_s
                    if grp == s // G and s % G:
                        d = d + dmask_ref[s % G]
                    terms.append(qs[rows] * jnp.exp2(d))
            ssums.append(_dot(jnp.concatenate(terms, axis=0).astype(BF16), ones))
            if i == 0:
                off_parts.append(jnp.zeros((SUB, L), F32))
            else:
                ref_row = bh[r0 - 1:r0]
                q_i = (qs * jnp.exp2(bs - ref_row)).astype(BF16)
                k_i = jnp.exp2(jnp.where(row_all < r0, ref_row - bkh, NEG_BIG)).astype(BF16)
                off_parts.append(_dot_nt(q_i, k_i))
        return inter, ssums, off_parts, vh, vb

    def finish(h, inter, ssums, off_parts, vh, vb):
        hs = slice(h * D, (h + 1) * D)
        diag_parts = []
        for i in range(L // SUB):
            acc = [None] * (SUB // G)
            t = 0
            for s in range(SUB):
                v_s = i_ref[i * SUB + s:i * SUB + s + 1, hs]
                for grp in range(s // G, SUB // G):
                    term = ssums[i][t * G:(t + 1) * G] * v_s
                    acc[grp] = term if acc[grp] is None else acc[grp] + term
                    t += 1
            diag_parts.extend(acc)
        s_off = jnp.concatenate(off_parts, axis=0).astype(BF16)
        o = inter + jnp.concatenate(diag_parts, axis=0) + _dot(s_off, vb)
        on = o * lax.rsqrt(jnp.mean(o * o, axis=-1, keepdims=True) + EPS) * gn[:, hs]
        gh = g[:, hs]
        o_ref[:, hs] = (on * (gh * _sigmoid(gh))).astype(o_ref.dtype)

    pending = None
    for h in range(HG_HEADS):
        cur = first_matmuls(h)
        if pending is not None:
            finish(h - 1, *pending)
        pending = cur
    finish(HG_HEADS - 1, *pending)


def hgrn_mix(z1, lb, hg_norm, batch, seq):
    T = batch * seq
    W = HG_HEADS * HG_D
    L = HG_CHUNK
    nc = seq // L
    tri = jnp.asarray(np.tril(np.ones((L, L), np.float32)), BF16)
    ones = jnp.ones((HG_D, HG_D), BF16)
    G = SUBLANES
    dmask_np = np.where(np.arange(G)[None, :, None] >= np.arange(G)[:, None, None], 0.0, NEG_BIG)
    dmask = jnp.asarray(np.broadcast_to(dmask_np, (G, G, HG_D)).astype(np.float32))

    def col(c):
        return pl.BlockSpec((L, W), lambda b, j, c=c: (b * nc + j, c))

    const = lambda shape: pl.BlockSpec(shape, lambda b, j: (0,) * len(shape))
    return pl.pallas_call(
        _hgrn_kernel,
        grid=(batch, nc),
        in_specs=[col(0), col(1), col(2), col(3), const((1, W)), const((1, W)),
                  const((L, L)), const((HG_D, HG_D)), const((G, G, HG_D))],
        out_specs=pl.BlockSpec((L, W), lambda b, j: (b * nc + j, 0)),
        out_shape=jax.ShapeDtypeStruct((T, W), BF16),
        scratch_shapes=[pltpu.VMEM((HG_HEADS, HG_D, HG_D), F32), pltpu.VMEM((L, W), F32)],
        compiler_params=_cparams(2, 48),
        name="hgrn2",
    )(z1, z1, z1, z1, lb.reshape(1, W).astype(F32),
      jnp.tile(hg_norm.astype(F32), HG_HEADS).reshape(1, W), tri, ones, dmask)


def _mla_prep_kernel(cq_ref, ckv_ref, a_ref, cos_ref, sin_ref, gq_ref, gkv_ref,
                     wq1_ref, wq2_ref, wkv_ref, wkr_ref, q_ref, kt_ref, v_ref):
    scale = (MLA_NOPE + MLA_ROPE) ** -0.5
    cos = cos_ref[...]
    sin = sin_ref[...]

    def norm(x, g):
        return (x * lax.rsqrt(jnp.mean(x * x, axis=-1, keepdims=True) + EPS) * g).astype(BF16)

    cn = norm(cq_ref[...], gq_ref[...])
    q1 = _dot(cn, wq1_ref[...])
    q2 = _dot(cn, wq2_ref[...])
    cvn = norm(ckv_ref[...], gkv_ref[...])
    kv = _dot(cvn, wkv_ref[...])
    kr = _dot(a_ref[...], wkr_ref[...])
    kr_rot_t = (kr[:, 0:LANES] * cos + kr[:, LANES:2 * LANES] * sin).T.astype(BF16)
    P = MLA_QK_PAD
    for h in range(MLA_HEADS):
        q_ref[:, h * P:h * P + LANES] = (q1[:, h * P:h * P + LANES] * scale).astype(BF16)
        q_ref[:, h * P + LANES:(h + 1) * P] = (
            (q1[:, h * P + LANES:(h + 1) * P] * cos + q2[:, h * LANES:(h + 1) * LANES] * sin) * scale
        ).astype(BF16)
        kt_ref[h * P:h * P + LANES, :] = kv[:, h * LANES:(h + 1) * LANES].T.astype(BF16)
        kt_ref[h * P + LANES:(h + 1) * P, :] = kr_rot_t
    nv = MLA_HEADS * MLA_V
    v_ref[...] = kv[:, nv:2 * nv].astype(BF16)


def mla_prep(z1, a, cos_t, sin_t, gq, gkv, wq1, wq2, wkv, wkr, tm):
    T = z1.shape[0]
    R = MLA_RANK
    cq_blk = (4 * HG_HEADS * HG_D) // R
    row = lambda w: pl.BlockSpec((tm, w), lambda i: (i, 0))
    const = lambda a: pl.BlockSpec(a.shape, lambda i: (0, 0))
    qk_w = MLA_HEADS * MLA_QK_PAD
    return pl.pallas_call(
        _mla_prep_kernel,
        grid=(T // tm,),
        in_specs=[pl.BlockSpec((tm, R), lambda i: (i, cq_blk)),
                  pl.BlockSpec((tm, R), lambda i: (i, cq_blk + 1)),
                  row(a.shape[1]), row(LANES), row(LANES),
                  pl.BlockSpec((1, R), lambda i: (0, 0)), pl.BlockSpec((1, R), lambda i: (0, 0)),
                  const(wq1), const(wq2), const(wkv), const(wkr)],
        out_specs=[row(qk_w), pl.BlockSpec((None, qk_w, tm), lambda i: (i, 0, 0)), row(MLA_HEADS * MLA_V)],
        out_shape=[jax.ShapeDtypeStruct((T, qk_w), BF16), jax.ShapeDtypeStruct((T // tm, qk_w, tm), BF16),
                   jax.ShapeDtypeStruct((T, MLA_HEADS * MLA_V), BF16)],
        compiler_params=_cparams(1, 48),
        name="mla_prep",
    )(z1, z1, a, cos_t, sin_t, gq.reshape(1, R).astype(F32), gkv.reshape(1, R).astype(F32),
      wq1, wq2, wkv, wkr)


def mla_weights(w_q_up, w_kv_up):
    R, half = MLA_RANK, MLA_ROPE // 2
    wq = w_q_up.reshape(R, MLA_HEADS, MLA_NOPE + MLA_ROPE)
    nope, rope = wq[..., :MLA_NOPE], wq[..., MLA_NOPE:]
    rope_sw = jnp.concatenate([rope[..., half:], rope[..., :half]], axis=-1)
    z64 = jnp.zeros((R, MLA_HEADS, MLA_QK_PAD - MLA_NOPE - MLA_ROPE), w_q_up.dtype)
    wq1 = jnp.concatenate([nope, rope, z64], axis=-1).reshape(R, MLA_HEADS * MLA_QK_PAD)
    wq2 = jnp.concatenate([rope_sw, z64], axis=-1).reshape(R, MLA_HEADS * LANES)
    wkv = w_kv_up.reshape(R, MLA_HEADS, MLA_NOPE + MLA_V)
    wkv = jnp.concatenate([wkv[..., :MLA_NOPE].reshape(R, -1), wkv[..., MLA_NOPE:].reshape(R, -1)], axis=-1)
    return wq1.astype(BF16), wq2.astype(BF16), wkv.astype(BF16)


def _causal_attn_kernel(q_ref, kt_ref, v_ref, o_ref, m_ref, l_ref, acc_ref, *, tq):
    i = pl.program_id(1)
    H, P, DV = MLA_HEADS, MLA_QK_PAD, MLA_V
    n_c = tq // LANES
    m_ref[...] = jnp.full_like(m_ref, NEG_BIG)
    l_ref[...] = jnp.zeros_like(l_ref)
    acc_ref[...] = jnp.zeros_like(acc_ref)

    def scores(j, h):
        return _dot(q_ref[:, h * P:(h + 1) * P], kt_ref[j, h * P:(h + 1) * P, :])

    def softmax(h, s, masked):
        if masked:
            rows = lax.broadcasted_iota(jnp.int32, (tq, tq), 0)
            cols = lax.broadcasted_iota(jnp.int32, (tq, tq), 1)
            s = jnp.where(cols <= rows, s, NEG_BIG)
        chunks = [s[:, c * LANES:(c + 1) * LANES] for c in range(n_c)]
        m_old = m_ref[h]
        m_new = jnp.maximum(m_old, jnp.max(functools.reduce(jnp.maximum, chunks), axis=-1, keepdims=True))
        ps = [jnp.exp(c - m_new) for c in chunks]
        alpha = jnp.exp(m_old - m_new)
        l_ref[h] = alpha * l_ref[h] + jnp.sum(functools.reduce(jnp.add, ps), axis=-1, keepdims=True)
        m_ref[h] = m_new
        return jnp.concatenate([c.astype(BF16) for c in ps], axis=1), alpha

    def values(j, h, p, alpha):
        start = pl.multiple_of(j * tq, tq)
        acc_ref[h] = alpha * acc_ref[h] + _dot(p, v_ref[pl.ds(start, tq), h * DV:(h + 1) * DV])

    def block(j, masked):
        s, pa = {}, {}
        for step in range(H + 2):
            if step < H:
                s[step] = scores(j, step)
            if 1 <= step <= H:
                pa[step - 1] = softmax(step - 1, s.pop(step - 1), masked)
            if step >= 2:
                values(j, step - 2, *pa.pop(step - 2))

    def body(j, carry):
        block(j, False)
        return carry

    lax.fori_loop(0, i, body, 0)
    block(i, True)
    for h in range(H):
        o_ref[:, h * DV:(h + 1) * DV] = (acc_ref[h] / l_ref[h]).astype(o_ref.dtype)


def causal_attention(qp, kt, v, batch, seq, tq):
    T = batch * seq
    nq = seq // tq
    H, P, DV = MLA_HEADS, MLA_QK_PAD, MLA_V
    return pl.pallas_call(
        functools.partial(_causal_attn_kernel, tq=tq),
        grid=(batch, nq),
        in_specs=[pl.BlockSpec((tq, H * P), lambda b, i: (b * nq + i, 0)),
                  pl.BlockSpec((nq, H * P, tq), lambda b, i: (b, 0, 0)),
                  pl.BlockSpec((seq, H * DV), lambda b, i: (b, 0))],
        out_specs=pl.BlockSpec((tq, H * DV), lambda b, i: (b * nq + i, 0)),
        out_shape=jax.ShapeDtypeStruct((T, H * DV), BF16),
        scratch_shapes=[pltpu.VMEM((H, tq, LANES), F32), pltpu.VMEM((H, tq, LANES), F32),
                        pltpu.VMEM((H, tq, DV), F32)],
        compiler_params=_cparams(2, 48),
        name="mla_attention",
    )(qp, kt, v)


def _mem_attn_kernel(q_ref, k_ref, v_ref, o_ref):
    scale = MEM_HEAD_DIM ** -0.5
    Dh = MEM_HEAD_DIM
    for h in range(MEM_HEADS):
        hs = slice(h * Dh, (h + 1) * Dh)
        s = _dot_nt(q_ref[:, hs], k_ref[:, hs]) * scale
        m = jnp.max(s, axis=-1, keepdims=True)
        p = jnp.exp(s - m)
        l = jnp.sum(p, axis=-1, keepdims=True)
        o = _dot(p.astype(BF16), v_ref[:, hs])
        o_ref[:, hs] = (o / l).astype(o_ref.dtype)


def memory_attention(z2, mem_kv, batch, seq, n_mem, tq=512):
    T = batch * seq
    tq = min(tq, seq)
    nq = seq // tq
    W = MEM_HEADS * MEM_HEAD_DIM
    return pl.pallas_call(
        _mem_attn_kernel,
        grid=(batch, nq),
        in_specs=[pl.BlockSpec((tq, W), lambda b, i: (b * nq + i, 0)),
                  pl.BlockSpec((n_mem, W), lambda b, i: (b, 0)),
                  pl.BlockSpec((n_mem, W), lambda b, i: (b, 1))],
        out_specs=pl.BlockSpec((tq, W), lambda b, i: (b * nq + i, 0)),
        out_shape=jax.ShapeDtypeStruct((T, W), BF16),
        compiler_params=_cparams(2, 48),
        name="mem_attention",
    )(z2, mem_kv, mem_kv)


def _merge_kernel(y0_ref, y1_ref, y2_ref, g0_ref, g1_ref, g2_ref, w_ref, o_ref):
    acc = None
    for n, (y_ref, g_ref) in enumerate(((y0_ref, g0_ref), (y1_ref, g1_ref), (y2_ref, g2_ref))):
        term = _sigmoid(g_ref[...].astype(F32)) * _dot(y_ref[...], w_ref[n])
        acc = term if acc is None else acc + term
    o_ref[...] = acc.astype(o_ref.dtype)


def branch_merge(y_hg, y_mla, y_mem, z2, w_branch, tm=1024, tn=512):
    T, W = y_hg.shape
    D = w_branch.shape[-1]
    tm = min(tm, T)
    g_off = W // tn

    def gate(n):
        return pl.BlockSpec((tm, tn), lambda i, j, n=n: (i, g_off + n * (D // tn) + j))

    y_spec = pl.BlockSpec((tm, W), lambda i, j: (i, 0))
    return pl.pallas_call(
        _merge_kernel,
        grid=(T // tm, D // tn),
        in_specs=[y_spec, y_spec, y_spec, gate(0), gate(1), gate(2),
                  pl.BlockSpec((3, W, tn), lambda i, j: (0, 0, j))],
        out_specs=pl.BlockSpec((tm, tn), lambda i, j: (i, j)),
        out_shape=jax.ShapeDtypeStruct((T, D), BF16),
        compiler_params=_cparams(2, 48),
        name="branch_merge",
    )(y_hg, y_mla, y_mem, z2, z2, z2, w_branch)


def _proj_res_norm_kernel(x_ref, w_ref, h_ref, g_ref, hn_ref, a_ref):
    hn = h_ref[...] + _dot(x_ref[...], w_ref[...])
    hn_ref[...] = hn
    a = hn * lax.rsqrt(jnp.mean(hn * hn, axis=-1, keepdims=True) + EPS) * g_ref[...]
    a_ref[...] = a.astype(a_ref.dtype)


def proj_residual_norm(x, w, h, gain, a_dtype, tm=512):
    T, K = x.shape
    D = w.shape[1]
    tm = min(tm, T)
    row = lambda width: pl.BlockSpec((tm, width), lambda i: (i, 0))
    return pl.pallas_call(
        _proj_res_norm_kernel,
        grid=(T // tm,),
        in_specs=[row(K), pl.BlockSpec((K, D), lambda i: (0, 0)), row(D),
                  pl.BlockSpec((1, D), lambda i: (0, 0))],
        out_specs=[row(D), row(D)],
        out_shape=[jax.ShapeDtypeStruct((T, D), F32), jax.ShapeDtypeStruct((T, D), a_dtype)],
        compiler_params=_cparams(1, 48),
        name="proj_residual_norm",
    )(x, w, h, gain.reshape(1, D).astype(F32))


def _res_norm_kernel(y_ref, h_ref, g_ref, hn_ref, a_ref):
    hn = h_ref[...] + y_ref[...]
    hn_ref[...] = hn
    a = hn * lax.rsqrt(jnp.mean(hn * hn, axis=-1, keepdims=True) + EPS) * g_ref[...]
    a_ref[...] = a.astype(a_ref.dtype)


def residual_norm(y, h, gain, a_dtype, tm=512):
    T, D = y.shape
    tm = min(tm, T)
    row = pl.BlockSpec((tm, D), lambda i: (i, 0))
    return pl.pallas_call(
        _res_norm_kernel,
        grid=(T // tm,),
        in_specs=[row, row, pl.BlockSpec((1, D), lambda i: (0, 0))],
        out_specs=[row, row],
        out_shape=[jax.ShapeDtypeStruct((T, D), F32), jax.ShapeDtypeStruct((T, D), a_dtype)],
        compiler_params=_cparams(1),
        name="residual_norm",
    )(y, h, gain.reshape(1, D).astype(F32))


def _ffn_kernel(row0_ref, ex_ref, work_ref, nblk_ref, tail_ref,
                x_ref, wg_ref, wu_ref, wd_ref, o_ref, acc, wgb, wub, wdb, sem, *, n_f):
    v = pl.program_id(0)
    j = pl.program_id(1)
    rlo = 0
    rhi = nblk_ref[v]

    last_j = j == n_f - 1
    row0 = pl.multiple_of(row0_ref[v], FFN_TM)

    @pl.when(jnp.logical_and(rhi > 0, j == 0))
    def _():
        acc[...] = jnp.zeros_like(acc)

    def out_copy(t):
        return pltpu.make_async_copy(acc.at[pl.ds(pl.multiple_of(t * FFN_TM, FFN_TM), FFN_TM), :],
                                     o_ref.at[pl.ds(pl.multiple_of(row0 + t * FFN_TM, FFN_TM), FFN_TM), :], sem)

    def row_blocks(r, n):
        rows = [pl.ds(pl.multiple_of((r + t) * FFN_TM, FFN_TM), FFN_TM) for t in range(n)]
        hid = []
        for t in range(n):
            xb = x_ref[rows[t], :].astype(BF16)
            gate = _dot(xb, wgb[...])
            up = _dot(xb, wub[...])
            hid.append((gate * _sigmoid(gate) * up).astype(BF16))
        for t in range(n):
            acc[rows[t], :] += _dot(hid[t], wdb[...])

        @pl.when(last_j)
        def _():
            for t in range(n):
                out_copy(r + t).start()

    @pl.when(rhi > rlo)
    def _():
        wgb[...] = wg_ref[...].astype(BF16)
        wub[...] = wu_ref[...].astype(BF16)
        wdb[...] = wd_ref[...].astype(BF16)
        row_blocks(rlo, 1)
        rest = rhi - rlo - 1

        def pair(t, carry):
            row_blocks(rlo + 1 + 2 * t, 2)
            return carry

        lax.fori_loop(0, lax.shift_right_logical(rest, 1), pair, 0)

        @pl.when((rest & 1) == 1)
        def _():
            row_blocks(rhi - 1, 1)

    def zero_copy(t):
        dst = pl.multiple_of((tail_ref[0] + t) * FFN_TM, FFN_TM)
        return pltpu.make_async_copy(acc.at[pl.ds(0, FFN_TM), :], o_ref.at[pl.ds(dst, FFN_TM), :], sem)

    def wait_copies(make, n):
        def wait(t, carry):
            make(t).wait()
            return carry

        lax.fori_loop(0, n, wait, 0)

    @pl.when(jnp.logical_and(rhi > 0, last_j))
    def _():
        wait_copies(out_copy, rhi)

    @pl.when(jnp.logical_and(v == pl.num_programs(0) - 1, last_j))
    def _():
        acc[0:FFN_TM, :] = jnp.zeros((FFN_TM, acc.shape[1]), F32)

        def start(t, carry):
            zero_copy(t).start()
            return carry

        lax.fori_loop(0, tail_ref[1], start, 0)
        wait_copies(zero_copy, tail_ref[1])


def ffn_schedule(pad_start, pad_end, n_rows):
    E = pad_start.shape[0]
    kmax = n_rows // FFN_RB
    nv = min(kmax + E, kmax * E)
    nb = (pad_end - pad_start) // FFN_TM
    n_win = (nb + FFN_RB // FFN_TM - 1) // (FFN_RB // FFN_TM)
    per_win = (nb + jnp.maximum(n_win, 1) - 1) // jnp.maximum(n_win, 1)
    k = jnp.arange(kmax, dtype=jnp.int32)[None, :]
    start = pad_start[:, None] + k * per_win[:, None] * FFN_TM
    flat = (k < n_win[:, None]).reshape(-1)
    nblk = jnp.clip(nb[:, None] - k * per_win[:, None], 0, per_win[:, None]).reshape(-1)
    order = jnp.argsort(jnp.logical_not(flat), stable=True)[:nv].astype(jnp.int32)
    n_valid = jnp.sum(flat.astype(jnp.int32))
    valid = jnp.arange(nv, dtype=jnp.int32) < n_valid
    order = jnp.where(valid, order, order[jnp.maximum(n_valid - 1, 0)])
    row0 = start.reshape(-1)[order].astype(jnp.int32)
    ex = (order // kmax).astype(jnp.int32)
    return row0, ex, valid.astype(jnp.int32), jnp.where(valid, nblk[order], 0).astype(jnp.int32)


def grouped_ffn(x, w_gu, w_down, sched, n_out, rows_end):
    R, D = x.shape
    E, _, F2 = w_gu.shape
    F = F2 // 2
    n_f = F // FFN_TF
    row0, ex, work, nblk = sched
    nv = row0.shape[0]
    tail = jnp.stack([rows_end // FFN_TM, (n_out - rows_end) // FFN_TM]).astype(jnp.int32)

    def jf(j, work_ref, v):
        return jnp.where(work_ref[v] == 1, j, n_f - 1)

    grid_spec = pltpu.PrefetchScalarGridSpec(
        num_scalar_prefetch=5,
        grid=(nv, n_f),
        in_specs=[
            pl.BlockSpec((pl.Element(FFN_RB), pl.Element(D)),
                         lambda v, j, row0, ex, wk, nb, tl: (pl.multiple_of(row0[v], FFN_TM), 0),
                         pipeline_mode=pl.Buffered(1)),
            pl.BlockSpec((None, D, FFN_TF), lambda v, j, row0, ex, wk, nb, tl: (ex[v], 0, jf(j, wk, v))),
            pl.BlockSpec((None, D, FFN_TF), lambda v, j, row0, ex, wk, nb, tl: (ex[v], 0, n_f + jf(j, wk, v))),
            pl.BlockSpec((None, FFN_TF, D), lambda v, j, row0, ex, wk, nb, tl: (ex[v], jf(j, wk, v), 0)),
        ],
        out_specs=pl.BlockSpec(memory_space=pl.ANY),
        scratch_shapes=[pltpu.VMEM((FFN_RB, D), F32),
                        pltpu.VMEM((D, FFN_TF), BF16), pltpu.VMEM((D, FFN_TF), BF16),
                        pltpu.VMEM((FFN_TF, D), BF16), pltpu.SemaphoreType.DMA(())],
    )
    return pl.pallas_call(
        functools.partial(_ffn_kernel, n_f=n_f),
        grid_spec=grid_spec,
        out_shape=jax.ShapeDtypeStruct((n_out, D), F32),
        compiler_params=_cparams(2, 60),
        name="grouped_ffn",
    )(row0, ex, work, nblk, tail, x, w_gu, w_gu, w_down)


def _router_kernel(a_ref, w_ref, o_ref):
    logits = _dot(a_ref[...].astype(BF16), w_ref[...])
    lane = lax.broadcasted_iota(jnp.int32, logits.shape, 1)
    lg = jnp.where(lane < N_EXPERTS, logits, NEG_BIG)
    m1 = jnp.max(lg, axis=-1, keepdims=True)
    i1 = jnp.min(jnp.where(lg == m1, lane, LANES), axis=-1, keepdims=True)
    lg2 = jnp.where(lane == i1, NEG_BIG, lg)
    m2 = jnp.max(lg2, axis=-1, keepdims=True)
    i2 = jnp.min(jnp.where(lg2 == m2, lane, LANES), axis=-1, keepdims=True)
    e2 = jnp.exp(m2 - m1)
    w1 = 1.0 / (1.0 + e2)
    w2 = e2 / (1.0 + e2)
    out = jnp.where(lane == 0, i1.astype(F32),
                    jnp.where(lane == 1, i2.astype(F32),
                              jnp.where(lane == 2, w1, jnp.where(lane == 3, w2, 0.0))))
    o_ref[...] = out


def router(a, w_router, tm=1024):
    T, D = a.shape
    tm = min(tm, T)
    w = jnp.zeros((D, LANES), BF16).at[:, :N_EXPERTS].set(w_router.astype(BF16))
    return pl.pallas_call(
        _router_kernel,
        grid=(T // tm,),
        in_specs=[pl.BlockSpec((tm, D), lambda i: (i, 0)), pl.BlockSpec((D, LANES), lambda i: (0, 0))],
        out_specs=pl.BlockSpec((tm, LANES), lambda i: (i, 0)),
        out_shape=jax.ShapeDtypeStruct((T, LANES), F32),
        compiler_params=_cparams(1, 48),
        name="router",
    )(a, w)


ROW_DMA_UNROLL = 8


def _gather_kernel(tok_ref, live_ref, a_ref, o_ref, buf, sem, *, tm):
    i = pl.program_id(0)
    n_live = live_ref[0]

    def issue(step, slot):
        base = step * tm

        def body(g, carry):
            for u in range(ROW_DMA_UNROLL):
                r = g * ROW_DMA_UNROLL + u
                pltpu.make_async_copy(a_ref.at[pl.ds(tok_ref[base + r], 1), :],
                                      buf.at[slot, pl.ds(r, 1), :], sem.at[slot]).start(priority=u % 2)
            return carry

        lax.fori_loop(0, tm // ROW_DMA_UNROLL, body, 0)

    @pl.when(i == 0)
    def _():
        issue(0, 0)

    @pl.when(i + 1 < n_live)
    def _():
        issue(i + 1, (i + 1) & 1)

    @pl.when(i < n_live)
    def _():
        slot = i & 1
        pltpu.make_async_copy(a_ref.at[pl.ds(0, tm), :], buf.at[slot], sem.at[slot]).wait()
        o_ref[...] = buf[slot].astype(o_ref.dtype)

    @pl.when(i >= n_live)
    def _():
        o_ref[...] = jnp.zeros_like(o_ref)


def gather_rows(a, slot_tok, rows_end, tm=512):
    T, D = a.shape
    R = slot_tok.shape[0]
    tm = min(tm, T)
    n_live = jnp.maximum((rows_end + tm - 1) // tm, 1).astype(jnp.int32).reshape(1)
    grid_spec = pltpu.PrefetchScalarGridSpec(
        num_scalar_prefetch=2,
        grid=(R // tm,),
        in_specs=[pl.BlockSpec(memory_space=pl.ANY)],
        out_specs=pl.BlockSpec((tm, D), lambda i, tok, live: (i, 0)),
        scratch_shapes=[pltpu.VMEM((2, tm, D), a.dtype), pltpu.SemaphoreType.DMA((2,))],
    )
    return pl.pallas_call(
        functools.partial(_gather_kernel, tm=tm),
        grid_spec=grid_spec,
        out_shape=jax.ShapeDtypeStruct((R, D), BF16),
        compiler_params=_cparams(1, 48),
        name="gather_rows",
    )(slot_tok, n_live, a)


def _combine_kernel(dest_ref, y_ref, rw_ref, h_ref, g_ref, hn_ref, a_ref, buf, sem, *, tm, n_steps):
    i = pl.program_id(0)

    def issue(step, slot):
        base = step * tm

        def body(g, carry):
            for u in range(ROW_DMA_UNROLL // TOP_K):
                r = g * (ROW_DMA_UNROLL // TOP_K) + u
                for k in range(TOP_K):
                    src = dest_ref[(base + r) * TOP_K + k]
                    pltpu.make_async_copy(y_ref.at[pl.ds(src, 1), :], buf.at[slot, k, pl.ds(r, 1), :],
                                          sem.at[slot]).start(priority=k % 2)
            return carry

        lax.fori_loop(0, tm // (ROW_DMA_UNROLL // TOP_K), body, 0)

    @pl.when(i == 0)
    def _():
        issue(0, 0)

    @pl.when(i + 1 < n_steps)
    def _():
        issue(i + 1, (i + 1) & 1)

    slot = i & 1
    for k in range(TOP_K):
        pltpu.make_async_copy(y_ref.at[pl.ds(0, tm), :], buf.at[slot, k], sem.at[slot]).wait()
    rw = rw_ref[...]
    hn = h_ref[...] + rw[:, 2:3] * buf[slot, 0] + rw[:, 3:4] * buf[slot, 1]
    hn_ref[...] = hn
    a = hn * lax.rsqrt(jnp.mean(hn * hn, axis=-1, keepdims=True) + EPS) * g_ref[...]
    a_ref[...] = a.astype(a_ref.dtype)


def combine_residual_norm(y_sorted, dest, rw, h, gain, a_dtype, tm=256):
    T, D = h.shape
    tm = min(tm, T)
    n_steps = T // tm
    row = pl.BlockSpec((tm, D), lambda i, d: (i, 0))
    grid_spec = pltpu.PrefetchScalarGridSpec(
        num_scalar_prefetch=1,
        grid=(n_steps,),
        in_specs=[pl.BlockSpec(memory_space=pl.ANY),
                  pl.BlockSpec((tm, LANES), lambda i, d: (i, 0)),
                  row,
                  pl.BlockSpec((1, D), lambda i, d: (0, 0))],
        out_specs=[row, row],
        scratch_shapes=[pltpu.VMEM((2, TOP_K, tm, D), F32), pltpu.SemaphoreType.DMA((2,))],
    )
    return pl.pallas_call(
        functools.partial(_combine_kernel, tm=tm, n_steps=n_steps),
        grid_spec=grid_spec,
        out_shape=[jax.ShapeDtypeStruct((T, D), F32), jax.ShapeDtypeStruct((T, D), a_dtype)],
        compiler_params=_cparams(1, 48),
        name="combine_residual_norm",
    )(dest, y_sorted, rw, h, gain.reshape(1, D).astype(F32))


def mixer_layer(h, a, l, lb, mem_n, cos_t, sin_t, p, batch, seq, n_mem, next_gain, a_dtype):
    D = h.shape[1]
    n_a = 4 * HG_HEADS * HG_D + 2 * MLA_RANK
    kr0 = n_a
    q_mem0 = kr0 + MLA_ROPE
    z1 = matmul_wt(a, p["w_in_t"], layer=l, row0=0, n_cols=n_a, tn=1024, tm=1024, out_dtype=F32,
                   name="in_proj_a")
    n_b = MEM_HEADS * MEM_HEAD_DIM + 3 * D
    z2 = matmul_wt(a, p["w_in_t"], layer=l, row0=q_mem0, n_cols=n_b, tn=1024, tm=1024, out_dtype=BF16,
                   name="in_proj_b")
    wq1, wq2, wkv = mla_weights(p["w_q_up"][l], p["w_kv_up"][l])
    wkr = rope_key_weight(p["w_in_t"], l, kr0)

    y_hg = hgrn_mix(z1, lb, p["hg_norm"][l], batch, seq)
    tq = min(ATTN_BLOCK, seq)
    qp, kt, v = mla_prep(z1, a, cos_t, sin_t, p["mla_q_norm"][l], p["mla_kv_norm"][l], wq1, wq2, wkv,
                         wkr, tq)
    y_mla = causal_attention(qp, kt, v, batch, seq, tq)
    mem_kv = matmul(mem_n, p["w_mem_kv"], layer=l, n_cols=2 * MEM_HEADS * MEM_HEAD_DIM, tn=512,
                    tm=1024, out_dtype=BF16, name="mem_kv")
    y_mem = memory_attention(z2, mem_kv, batch, seq, n_mem)
    merged = branch_merge(y_hg, y_mla, y_mem, z2, p["w_branch"][l].astype(BF16))
    return proj_residual_norm(merged, p["w_o"][l].astype(BF16), h, next_gain, a_dtype, tm=512)


def group_bounds(starts, ends, index, n_groups, n_rows):
    E = starts.shape[0]
    before = jnp.zeros((index * E,), jnp.int32)
    after = jnp.full(((n_groups - index - 1) * E,), n_rows, jnp.int32)
    return (jnp.concatenate([before, starts.astype(jnp.int32), after]),
            jnp.concatenate([before, ends.astype(jnp.int32), after]))


def moe_routing(rw, n_tokens):
    A = n_tokens * TOP_K
    flat_e = rw[:, :TOP_K].astype(jnp.int32).reshape(A)
    onehot = (flat_e[:, None] == jnp.arange(N_EXPERTS, dtype=jnp.int32)[None, :]).astype(jnp.int32)
    csum = jnp.cumsum(onehot, axis=0)
    rank = jnp.take_along_axis(csum, flat_e[:, None], axis=1)[:, 0] - 1
    counts = csum[-1]
    padded = (counts + FFN_TM - 1) // FFN_TM * FFN_TM
    pad_end = jnp.cumsum(padded).astype(jnp.int32)
    pad_start = pad_end - padded
    dest = (pad_start[flat_e] + rank).astype(jnp.int32)
    n_rows = (A + N_EXPERTS * FFN_TM + FFN_RB - 1) // FFN_RB * FFN_RB + FFN_RB
    filler = jnp.arange(n_rows, dtype=jnp.int32) % n_tokens
    slot_tok = filler.at[dest].set(jnp.arange(A, dtype=jnp.int32) // TOP_K)
    return dest, slot_tok, pad_start, pad_end, n_rows


def kernel(x, mem, positions, hg_lb_logits, attn_norm, w_in, hg_norm, mla_q_norm, w_q_up, mla_kv_norm,
           w_kv_up, mem_norm, w_mem_kv, w_branch, w_o, ffn_norm, w_dense_gu, w_dense_down, w_router,
           w_expert_gu, w_expert_down, final_norm):
    batch, seq, D = x.shape
    n_mem = mem.shape[1]
    depth = w_in.shape[0]
    T = batch * seq
    p = dict(w_in_t=jnp.swapaxes(w_in, 1, 2), hg_norm=hg_norm, mla_q_norm=mla_q_norm, w_q_up=w_q_up, mla_kv_norm=mla_kv_norm,
             w_kv_up=w_kv_up, w_mem_kv=w_mem_kv, w_branch=w_branch, w_o=w_o)

    cos_t, sin_t = rope_tables(positions.astype(jnp.int32))
    lb_sm = jax.nn.softmax(hg_lb_logits.astype(F32), axis=0)
    lb_all = jnp.clip(jnp.cumsum(lb_sm, axis=0) - lb_sm[0:1], 0.0, 1.0)
    mem_n = rmsnorm_cast(mem.reshape(batch * n_mem, D), mem_norm, BF16)

    n_dense, n_moe = w_dense_gu.shape[0], w_expert_gu.shape[0]
    w_moe_gu = w_expert_gu.reshape((n_moe * N_EXPERTS,) + w_expert_gu.shape[2:])
    w_moe_down = w_expert_down.reshape((n_moe * N_EXPERTS,) + w_expert_down.shape[2:])

    h = x.reshape(T, D)
    a = rmsnorm_cast(h, attn_norm[0], BF16)
    for l in range(depth):
        moe = l % 2 == 1
        last = l + 1 == depth
        next_gain = final_norm if last else attn_norm[l + 1]
        next_dtype = F32 if last else BF16
        h, a_ffn = mixer_layer(h, a, l, lb_all[l], mem_n, cos_t, sin_t, p, batch, seq, n_mem,
                               ffn_norm[l], F32 if moe else BF16)
        if not moe:
            zero = jnp.zeros((1,), jnp.int32)
            starts, ends = group_bounds(zero, zero + T, l // 2, n_dense, T)
            y = grouped_ffn(a_ffn, w_dense_gu, w_dense_down, ffn_schedule(starts, ends, T), T,
                            jnp.int32(T))
            h, a = residual_norm(y, h, next_gain, next_dtype)
        else:
            rw = router(a_ffn, w_router[l // 2])
            dest, slot_tok, pad_start, pad_end, n_rows = moe_routing(rw, T)
            xs = gather_rows(a_ffn, slot_tok, pad_end[-1])
            starts, ends = group_bounds(pad_start, pad_end, l // 2, n_moe, n_rows)
            ys = grouped_ffn(xs, w_moe_gu, w_moe_down, ffn_schedule(starts, ends, n_rows),
                             n_rows - FFN_RB, pad_end[-1])
            h, a = combine_residual_norm(ys, dest, rw, h, next_gain, next_dtype)
    return a.reshape(batch, seq, D)
```

```python
import functools
import math

import numpy as np
import jax
import jax.numpy as jnp
from jax import lax
from jax.experimental import pallas as pl
from jax.experimental.pallas import tpu as pltpu

F32 = jnp.float32
BF16 = jnp.bfloat16

EPS = 1e-6
NEG_BIG = -1e30
MIN_FORGET = 1e-20
ROPE_THETA = 10000.0

HG_HEADS = 8
HG_D = 128
HG_CHUNK = 64
HG_SUB = 16
MLA_HEADS = 8
MLA_RANK = 512
MLA_NOPE = 128
MLA_ROPE = 64
MLA_V = 128
MLA_QK_PAD = 256
MEM_HEADS = 4
MEM_HEAD_DIM = 256
N_EXPERTS = 8
TOP_K = 2

LOG2E = 1.4426950408889634

LANES = 128
SUBLANES = 8
ATTN_BLOCK = 512
ATTN_Q_ROWS = 512
FFN_TM = 256
FFN_RB = 2048
FFN_TF = 512


def _cparams(n_axes, vmem_mb=None):
    kw = dict(dimension_semantics=("arbitrary",) * n_axes)
    if vmem_mb is not None:
        kw["vmem_limit_bytes"] = vmem_mb * 1024 * 1024
    return pltpu.CompilerParams(**kw)


def _sigmoid(x):
    return 1.0 / (1.0 + jnp.exp(-x))


def _dot(a, b):
    return jnp.dot(a, b, preferred_element_type=F32)


def _dot_nt(a, b):
    return lax.dot_general(a, b, (((1,), (1,)), ((), ())), preferred_element_type=F32)


def _rope_kernel(pos_ref, c_ref, cos_ref, sin_ref):
    ang = pos_ref[...].astype(F32) * c_ref[0:1, :]
    cos_ref[...] = jnp.cos(ang) * c_ref[1:2, :]
    sin_ref[...] = jnp.sin(ang) * c_ref[2:3, :]


def rope_tables(positions, tm=1024):
    T = positions.size
    half = MLA_ROPE // 2
    inv = ROPE_THETA ** (-(np.arange(half, dtype=np.float32) / half))
    consts = np.zeros((8, LANES), np.float32)
    consts[0, :half] = inv
    consts[0, half:2 * half] = inv
    consts[1, :2 * half] = 1.0
    consts[2, :half] = -1.0
    consts[2, half:2 * half] = 1.0
    tm = min(tm, T)
    out = jax.ShapeDtypeStruct((T, LANES), F32)
    return pl.pallas_call(
        _rope_kernel,
        grid=(T // tm,),
        in_specs=[pl.BlockSpec((tm, 1), lambda i: (i, 0)),
                  pl.BlockSpec((8, LANES), lambda i: (0, 0))],
        out_specs=[pl.BlockSpec((tm, LANES), lambda i: (i, 0)),
                   pl.BlockSpec((tm, LANES), lambda i: (i, 0))],
        out_shape=[out, out],
        compiler_params=_cparams(1),
        name="rope_tables",
    )(positions.reshape(T, 1), jnp.asarray(consts))


def _rmsnorm_kernel(x_ref, g_ref, o_ref):
    x = x_ref[...].astype(F32)
    y = x * lax.rsqrt(jnp.mean(x * x, axis=-1, keepdims=True) + EPS)
    o_ref[...] = (y * g_ref[...]).astype(o_ref.dtype)


def rmsnorm_cast(x, g, out_dtype, tm=512):
    M, D = x.shape
    tm = min(tm, M)
    return pl.pallas_call(
        _rmsnorm_kernel,
        grid=(M // tm,),
        in_specs=[pl.BlockSpec((tm, D), lambda i: (i, 0)),
                  pl.BlockSpec((1, D), lambda i: (0, 0))],
        out_specs=pl.BlockSpec((tm, D), lambda i: (i, 0)),
        out_shape=jax.ShapeDtypeStruct((M, D), out_dtype),
        compiler_params=_cparams(1),
        name="rmsnorm",
    )(x, g.reshape(1, D).astype(F32))


def _mm_kernel(x_ref, w_ref, o_ref, *scratch, cast):
    if cast:
        (wb,) = scratch

        @pl.when(pl.program_id(1) == 0)
        def _():
            wb[...] = w_ref[...].astype(BF16)

        w = wb[...]
    else:
        w = w_ref[...]
    o_ref[...] = _dot(x_ref[...], w).astype(o_ref.dtype)


def matmul(x, w, *, n_cols, tn, tm, out_dtype, layer=None, name="matmul"):
    M, K = x.shape
    tm = min(tm, M)
    cast = w.dtype != BF16
    if layer is None:
        w_spec = pl.BlockSpec((K, tn), lambda j, i: (0, j))
    else:
        w_spec = pl.BlockSpec((None, K, tn), lambda j, i: (layer, 0, j))
    scratch = [pltpu.VMEM((K, tn), BF16)] if cast else []
    return pl.pallas_call(
        functools.partial(_mm_kernel, cast=cast),
        grid=(n_cols // tn, M // tm),
        in_specs=[pl.BlockSpec((tm, K), lambda j, i: (i, 0)), w_spec],
        out_specs=pl.BlockSpec((tm, tn), lambda j, i: (i, j)),
        out_shape=jax.ShapeDtypeStruct((M, n_cols), out_dtype),
        scratch_shapes=scratch,
        compiler_params=_cparams(2, 48),
        name=name,
    )(x, w)


def _mm_wt_kernel(x_ref, wt_ref, o_ref, wb):
    @pl.when(pl.program_id(1) == 0)
    def _():
        wb[...] = wt_ref[...].T.astype(BF16)

    o_ref[...] = _dot(x_ref[...], wb[...]).astype(o_ref.dtype)


def matmul_wt(x, wt, *, layer, row0, n_cols, tn, tm, out_dtype, name):
    M, K = x.shape
    tm = min(tm, M)
    assert n_cols % tn == 0
    if row0 % tn == 0:
        w_spec = pl.BlockSpec((None, tn, K), lambda j, i: (layer, row0 // tn + j, 0))
    else:
        w_spec = pl.BlockSpec((pl.Squeezed(), pl.Element(tn), pl.Element(K)),
                              lambda j, i: (layer, pl.multiple_of(row0 + j * tn, 8), 0))
    return pl.pallas_call(
        _mm_wt_kernel,
        grid=(n_cols // tn, M // tm),
        in_specs=[pl.BlockSpec((tm, K), lambda j, i: (i, 0)), w_spec],
        out_specs=pl.BlockSpec((tm, tn), lambda j, i: (i, j)),
        out_shape=jax.ShapeDtypeStruct((M, n_cols), out_dtype),
        scratch_shapes=[pltpu.VMEM((K, tn), BF16)],
        compiler_params=_cparams(2, 48),
        name=name,
    )(x, wt)


def _rope_key_weight_kernel(wt_ref, o_ref):
    half = MLA_ROPE // 2
    w = wt_ref[...]
    zero = jnp.zeros((LANES - MLA_ROPE, w.shape[1]), F32)
    rows = jnp.concatenate([w, zero, w[half:], w[:half], zero], axis=0)
    o_ref[...] = rows.T.astype(BF16)


def rope_key_weight(wt, layer, row0):
    K = wt.shape[2]
    assert row0 % MLA_ROPE == 0
    return pl.pallas_call(
        _rope_key_weight_kernel,
        grid=(1,),
        in_specs=[pl.BlockSpec((None, MLA_ROPE, K), lambda i: (layer, row0 // MLA_ROPE, 0))],
        out_specs=pl.BlockSpec((K, 2 * LANES), lambda i: (0, 0)),
        out_shape=jax.ShapeDtypeStruct((K, 2 * LANES), BF16),
        compiler_params=_cparams(1),
        name="rope_key_weight",
    )(wt)


def _hgrn_kernel(q_ref, f_ref, i_ref, g_ref, lb_ref, gn_ref, tri_ref, ones_ref, dmask_ref, o_ref,
                 st_ref, bk_scr):
    L, SUB, D, G = HG_CHUNK, HG_SUB, HG_D, SUBLANES

    @pl.when(pl.program_id(1) == 0)
    def _():
        st_ref[...] = jnp.zeros_like(st_ref)

    lb = lb_ref[...]
    f = lb + (1.0 - lb) * _sigmoid(f_ref[...])
    lf = jnp.log(jnp.maximum(f, MIN_FORGET)) * LOG2E
    kk = 1.0 - f
    tri = tri_ref[...]
    p1 = lf.astype(BF16)
    r1 = lf - p1.astype(F32)
    p2 = r1.astype(BF16)
    p3 = (r1 - p2.astype(F32)).astype(BF16)
    b = _dot(tri, p1) + _dot(tri, p2) + _dot(tri, p3)

    q = q_ref[...]
    v = i_ref[...]
    g = g_ref[...]
    gn = gn_ref[...]
    ones = ones_ref[...]
    bk = b - jnp.log2(jnp.maximum(kk, 0.0))
    bk_scr[...] = bk
    b_last = b[L - 1:L, :]
    q_in = (q * jnp.exp2(b)).astype(BF16)
    k_dec = jnp.exp2(b_last - bk).astype(BF16)
    e_last = jnp.exp2(b_last)

    def first_matmuls(h):
        hs = slice(h * D, (h + 1) * D)
        bh, qh, bkh, vh = b[:, hs], q[:, hs], bk[:, hs], v[:, hs]
        vb = vh.astype(BF16)
        st = st_ref[h]
        inter = _dot_nt(q_in[:, hs], st.astype(BF16))
        st_ref[h] = e_last[:, hs] * st + lax.dot_general(
            vb, k_dec[:, hs], (((0,), (0,)), ((), ())), preferred_element_type=F32)
        ssums, off_parts = [], []
        for i in range(L // SUB):
            r0 = i * SUB
            bs, qs = bh[r0:r0 + SUB], qh[r0:r0 + SUB]
            terms = []
            for s in range(SUB):
                bk_s = bk_scr[r0 + s:r0 + s + 1, hs]
                for grp in range(s // G, SUB // G):
                    rows = slice(grp * G, (grp + 1) * G)
                    d = bs[rows] - bk_s
                    if grp == s // G and s % G:
                        d = d + dmask_ref[s % G]
                    terms.append(qs[rows] * jnp.exp2(d))
            ssums.append(_dot(jnp.concatenate(terms, axis=0).astype(BF16), ones))
            if i == 0:
                off_parts.append(jnp.zeros((SUB, L), F32))
            else:
                ref_row = bh[r0 - 1:r0]
                q_i = (qs * jnp.exp2(bs - ref_row)).astype(BF16)
                k_i = jnp.concatenate([jnp.exp2(ref_row - bkh[0:r0]), jnp.zeros((L - r0, D), F32)],
                                      axis=0).astype(BF16)
                off_parts.append(_dot_nt(q_i, k_i))
        return inter, ssums, off_parts, vh, vb

    def finish(h, inter, ssums, off_parts, vh, vb):
        hs = slice(h * D, (h + 1) * D)
        diag_parts = []
        for i in range(L // SUB):
            acc = [None] * (SUB // G)
            t = 0
            for s in range(SUB):
                v_s = i_ref[i * SUB + s:i * SUB + s + 1, hs]
                for grp in range(s // G, SUB // G):
                    term = ssums[i][t * G:(t + 1) * G] * v_s
                    acc[grp] = term if acc[grp] is None else acc[grp] + term
                    t += 1
            diag_parts.extend(acc)
        s_off = jnp.concatenate(off_parts, axis=0).astype(BF16)
        o = inter + jnp.concatenate(diag_parts, axis=0) + _dot(s_off, vb)
        on = o * lax.rsqrt(jnp.mean(o * o, axis=-1, keepdims=True) + EPS) * gn[:, hs]
        gh = g[:, hs]
        o_ref[:, hs] = (on * (gh * _sigmoid(gh))).astype(o_ref.dtype)

    pending = None
    for h in range(HG_HEADS):
        cur = first_matmuls(h)
        if pending is not None:
            finish(h - 1, *pending)
        pending = cur
    finish(HG_HEADS - 1, *pending)


def hgrn_mix(z1, lb, hg_norm, batch, seq):
    T = batch * seq
    W = HG_HEADS * HG_D
    L = HG_CHUNK
    nc = seq // L
    tri = jnp.asarray(np.tril(np.ones((L, L), np.float32)), BF16)
    ones = jnp.ones((HG_D, HG_D), BF16)
    G = SUBLANES
    dmask_np = np.where(np.arange(G)[None, :, None] >= np.arange(G)[:, None, None], 0.0, NEG_BIG)
    dmask = jnp.asarray(np.broadcast_to(dmask_np, (G, G, HG_D)).astype(np.float32))

    def col(c):
        return pl.BlockSpec((L, W), lambda b, j, c=c: (b * nc + j, c))

    const = lambda shape: pl.BlockSpec(shape, lambda b, j: (0,) * len(shape))
    return pl.pallas_call(
        _hgrn_kernel,
        grid=(batch, nc),
        in_specs=[col(0), col(1), col(2), col(3), const((1, W)), const((1, W)),
                  const((L, L)), const((HG_D, HG_D)), const((G, G, HG_D))],
        out_specs=pl.BlockSpec((L, W), lambda b, j: (b * nc + j, 0)),
        out_shape=jax.ShapeDtypeStruct((T, W), BF16),
        scratch_shapes=[pltpu.VMEM((HG_HEADS, HG_D, HG_D), F32), pltpu.VMEM((L, W), F32)],
        compiler_params=_cparams(2, 48),
        name="hgrn2",
    )(z1, z1, z1, z1, lb.reshape(1, W).astype(F32),
      jnp.tile(hg_norm.astype(F32), HG_HEADS).reshape(1, W), tri, ones, dmask)


def _mla_prep_kernel(cq_ref, ckv_ref, a_ref, cos_ref, sin_ref, gq_ref, gkv_ref,
                     wq1_ref, wq2_ref, wkv_ref, wkr_ref, q_ref, kt_ref, v_ref):
    scale = (MLA_NOPE + MLA_ROPE) ** -0.5 * LOG2E
    cos = cos_ref[...]
    sin = sin_ref[...]

    def norm(x, g):
        return (x * lax.rsqrt(jnp.mean(x * x, axis=-1, keepdims=True) + EPS) * g).astype(BF16)

    cn = norm(cq_ref[...], gq_ref[...])
    q1 = _dot(cn, wq1_ref[...])
    q2 = _dot(cn, wq2_ref[...])
    cvn = norm(ckv_ref[...], gkv_ref[...])
    kv = _dot(cvn, wkv_ref[...])
    kr = _dot(a_ref[...], wkr_ref[...])
    kr_rot_t = (kr[:, 0:LANES] * cos + kr[:, LANES:2 * LANES] * sin).T.astype(BF16)
    P = MLA_QK_PAD
    for h in range(MLA_HEADS):
        q_ref[:, h * P:h * P + LANES] = (q1[:, h * P:h * P + LANES] * scale).astype(BF16)
        q_ref[:, h * P + LANES:(h + 1) * P] = (
            (q1[:, h * P + LANES:(h + 1) * P] * cos + q2[:, h * LANES:(h + 1) * LANES] * sin) * scale
        ).astype(BF16)
        kt_ref[h * P:h * P + LANES, :] = kv[:, h * LANES:(h + 1) * LANES].T.astype(BF16)
        kt_ref[h * P + LANES:(h + 1) * P, :] = kr_rot_t
    nv = MLA_HEADS * MLA_V
    v_ref[...] = kv[:, nv:2 * nv].astype(BF16)


def mla_prep(z1, a, cos_t, sin_t, gq, gkv, wq1, wq2, wkv, wkr, tm):
    T = z1.shape[0]
    R = MLA_RANK
    cq_blk = (4 * HG_HEADS * HG_D) // R
    row = lambda w: pl.BlockSpec((tm, w), lambda i: (i, 0))
    const = lambda a: pl.BlockSpec(a.shape, lambda i: (0, 0))
    qk_w = MLA_HEADS * MLA_QK_PAD
    return pl.pallas_call(
        _mla_prep_kernel,
        grid=(T // tm,),
        in_specs=[pl.BlockSpec((tm, R), lambda i: (i, cq_blk)),
                  pl.BlockSpec((tm, R), lambda i: (i, cq_blk + 1)),
                  row(a.shape[1]), row(LANES), row(LANES),
                  pl.BlockSpec((1, R), lambda i: (0, 0)), pl.BlockSpec((1, R), lambda i: (0, 0)),
                  const(wq1), const(wq2), const(wkv), const(wkr)],
        out_specs=[row(qk_w), pl.BlockSpec((None, qk_w, tm), lambda i: (i, 0, 0)), row(MLA_HEADS * MLA_V)],
        out_shape=[jax.ShapeDtypeStruct((T, qk_w), BF16), jax.ShapeDtypeStruct((T // tm, qk_w, tm), BF16),
                   jax.ShapeDtypeStruct((T, MLA_HEADS * MLA_V), BF16)],
        compiler_params=_cparams(1, 48),
        name="mla_prep",
    )(z1, z1, a, cos_t, sin_t, gq.reshape(1, R).astype(F32), gkv.reshape(1, R).astype(F32),
      wq1, wq2, wkv, wkr)


def mla_weights(w_q_up, w_kv_up):
    R, half = MLA_RANK, MLA_ROPE // 2
    wq = w_q_up.reshape(R, MLA_HEADS, MLA_NOPE + MLA_ROPE)
    nope, rope = wq[..., :MLA_NOPE], wq[..., MLA_NOPE:]
    rope_sw = jnp.concatenate([rope[..., half:], rope[..., :half]], axis=-1)
    z64 = jnp.zeros((R, MLA_HEADS, MLA_QK_PAD - MLA_NOPE - MLA_ROPE), w_q_up.dtype)
    wq1 = jnp.concatenate([nope, rope, z64], axis=-1).reshape(R, MLA_HEADS * MLA_QK_PAD)
    wq2 = jnp.concatenate([rope_sw, z64], axis=-1).reshape(R, MLA_HEADS * LANES)
    wkv = w_kv_up.reshape(R, MLA_HEADS, MLA_NOPE + MLA_V)
    wkv = jnp.concatenate([wkv[..., :MLA_NOPE].reshape(R, -1), wkv[..., MLA_NOPE:].reshape(R, -1)], axis=-1)
    return wq1.astype(BF16), wq2.astype(BF16), wkv.astype(BF16)


def _causal_attn_kernel(q_ref, kt_ref, v_ref, o_ref, m_ref, l_ref, acc_ref, *, tq, tk):
    i = pl.program_id(1)
    H, P, DV = MLA_HEADS, MLA_QK_PAD, MLA_V
    n_c = tk // LANES
    n_diag = tq // tk
    m_ref[...] = jnp.full_like(m_ref, NEG_BIG)
    l_ref[...] = jnp.zeros_like(l_ref)
    acc_ref[...] = jnp.zeros_like(acc_ref)

    def scores(j, h):
        return _dot(q_ref[:, h * P:(h + 1) * P], kt_ref[j, h * P:(h + 1) * P, :])

    def softmax(h, s, col_off):
        if col_off is not None:
            rows = lax.broadcasted_iota(jnp.int32, (tq, tk), 0)
            cols = lax.broadcasted_iota(jnp.int32, (tq, tk), 1) + col_off
            s = jnp.where(cols <= rows, s, NEG_BIG)
        chunks = [s[:, c * LANES:(c + 1) * LANES] for c in range(n_c)]
        m_old = m_ref[h]
        m_new = jnp.maximum(m_old, jnp.max(functools.reduce(jnp.maximum, chunks), axis=-1, keepdims=True))
        ps = [jnp.exp2(c - m_new) for c in chunks]
        alpha = jnp.exp2(m_old - m_new)
        l_ref[h] = alpha * l_ref[h] + jnp.sum(functools.reduce(jnp.add, ps), axis=-1, keepdims=True)
        m_ref[h] = m_new
        return jnp.concatenate([c.astype(BF16) for c in ps], axis=1), alpha

    def values(j, h, p, alpha):
        start = pl.multiple_of(j * tk, tk)
        acc_ref[h] = alpha * acc_ref[h] + _dot(p, v_ref[pl.ds(start, tk), h * DV:(h + 1) * DV])

    def block(j, col_off):
        s, pa = {}, {}
        for step in range(H + 2):
            if step < H:
                s[step] = scores(j, step)
            if 1 <= step <= H:
                pa[step - 1] = softmax(step - 1, s.pop(step - 1), col_off)
            if step >= 2:
                values(j, step - 2, *pa.pop(step - 2))

    def body(j, carry):
        block(j, None)
        return carry

    lax.fori_loop(0, i * n_diag, body, 0)
    for d in range(n_diag):
        block(i * n_diag + d, d * tk)
    for h in range(H):
        o_ref[:, h * DV:(h + 1) * DV] = (acc_ref[h] / l_ref[h]).astype(o_ref.dtype)


def causal_attention(qp, kt, v, batch, seq, tq, tk):
    T = batch * seq
    nq, nk = seq // tq, seq // tk
    H, P, DV = MLA_HEADS, MLA_QK_PAD, MLA_V
    return pl.pallas_call(
        functools.partial(_causal_attn_kernel, tq=tq, tk=tk),
        grid=(batch, nq),
        in_specs=[pl.BlockSpec((tq, H * P), lambda b, i: (b * nq + i, 0)),
                  pl.BlockSpec((nk, H * P, tk), lambda b, i: (b, 0, 0)),
                  pl.BlockSpec((seq, H * DV), lambda b, i: (b, 0))],
        out_specs=pl.BlockSpec((tq, H * DV), lambda b, i: (b * nq + i, 0)),
        out_shape=jax.ShapeDtypeStruct((T, H * DV), BF16),
        scratch_shapes=[pltpu.VMEM((H, tq, LANES), F32), pltpu.VMEM((H, tq, LANES), F32),
                        pltpu.VMEM((H, tq, DV), F32)],
        compiler_params=_cparams(2, 48),
        name="mla_attention",
    )(qp, kt, v)


def _mem_attn_kernel(q_ref, k_ref, v_ref, o_ref):
    scale = MEM_HEAD_DIM ** -0.5
    Dh = MEM_HEAD_DIM
    for h in range(MEM_HEADS):
        hs = slice(h * Dh, (h + 1) * Dh)
        s = _dot_nt(q_ref[:, hs], k_ref[:, hs]) * scale
        m = jnp.max(s, axis=-1, keepdims=True)
        p = jnp.exp(s - m)
        l = jnp.sum(p, axis=-1, keepdims=True)
        o = _dot(p.astype(BF16), v_ref[:, hs])
        o_ref[:, hs] = (o / l).astype(o_ref.dtype)


def memory_attention(z2, mem_kv, batch, seq, n_mem, tq=512):
    T = batch * seq
    tq = min(tq, seq)
    nq = seq // tq
    W = MEM_HEADS * MEM_HEAD_DIM
    return pl.pallas_call(
        _mem_attn_kernel,
        grid=(batch, nq),
        in_specs=[pl.BlockSpec((tq, W), lambda b, i: (b * nq + i, 0)),
                  pl.BlockSpec((n_mem, W), lambda b, i: (b, 0)),
                  pl.BlockSpec((n_mem, W), lambda b, i: (b, 1))],
        out_specs=pl.BlockSpec((tq, W), lambda b, i: (b * nq + i, 0)),
        out_shape=jax.ShapeDtypeStruct((T, W), BF16),
        compiler_params=_cparams(2, 48),
        name="mem_attention",
    )(z2, mem_kv, mem_kv)


def _merge_kernel(y0_ref, y1_ref, y2_ref, g0_ref, g1_ref, g2_ref, w_ref, o_ref):
    acc = None
    for n, (y_ref, g_ref) in enumerate(((y0_ref, g0_ref), (y1_ref, g1_ref), (y2_ref, g2_ref))):
        term = _sigmoid(g_ref[...].astype(F32)) * _dot(y_ref[...], w_ref[n])
        acc = term if acc is None else acc + term
    o_ref[...] = acc.astype(o_ref.dtype)


def branch_merge(y_hg, y_mla, y_mem, z2, w_branch, tm=1024, tn=512):
    T, W = y_hg.shape
    D = w_branch.shape[-1]
    tm = min(tm, T)
    g_off = W // tn

    def gate(n):
        return pl.BlockSpec((tm, tn), lambda i, j, n=n: (i, g_off + n * (D // tn) + j))

    y_spec = pl.BlockSpec((tm, W), lambda i, j: (i, 0))
    return pl.pallas_call(
        _merge_kernel,
        grid=(T // tm, D // tn),
        in_specs=[y_spec, y_spec, y_spec, gate(0), gate(1), gate(2),
                  pl.BlockSpec((3, W, tn), lambda i, j: (0, 0, j))],
        out_specs=pl.BlockSpec((tm, tn), lambda i, j: (i, j)),
        out_shape=jax.ShapeDtypeStruct((T, D), BF16),
        compiler_params=_cparams(2, 48),
        name="branch_merge",
    )(y_hg, y_mla, y_mem, z2, z2, z2, w_branch)


def _top2_route(logits):
    lane = lax.broadcasted_iota(jnp.int32, logits.shape, 1)
    lg = jnp.where(lane < N_EXPERTS, logits, NEG_BIG)
    m1 = jnp.max(lg, axis=-1, keepdims=True)
    i1 = jnp.min(jnp.where(lg == m1, lane, LANES), axis=-1, keepdims=True)
    lg2 = jnp.where(lane == i1, NEG_BIG, lg)
    m2 = jnp.max(lg2, axis=-1, keepdims=True)
    i2 = jnp.min(jnp.where(lg2 == m2, lane, LANES), axis=-1, keepdims=True)
    e2 = jnp.exp(m2 - m1)
    w1 = 1.0 / (1.0 + e2)
    w2 = e2 / (1.0 + e2)
    return jnp.where(lane == 0, i1.astype(F32),
                     jnp.where(lane == 1, i2.astype(F32),
                               jnp.where(lane == 2, w1, jnp.where(lane == 3, w2, 0.0))))


def _proj_res_norm_kernel(x_ref, w_ref, h_ref, g_ref, *rest, route):
    if route:
        wr_ref, hn_ref, a_ref, rw_ref = rest
    else:
        hn_ref, a_ref = rest
    hn = h_ref[...] + _dot(x_ref[...], w_ref[...])
    hn_ref[...] = hn
    a = hn * lax.rsqrt(jnp.mean(hn * hn, axis=-1, keepdims=True) + EPS) * g_ref[...]
    a_ref[...] = a.astype(a_ref.dtype)
    if route:
        rw_ref[...] = _top2_route(_dot(a.astype(BF16), wr_ref[...]))


def proj_residual_norm(x, w, h, gain, a_dtype, w_router=None, tm=512):
    T, K = x.shape
    D = w.shape[1]
    tm = min(tm, T)
    route = w_router is not None
    row = lambda width: pl.BlockSpec((tm, width), lambda i: (i, 0))
    const = lambda r, c: pl.BlockSpec((r, c), lambda i: (0, 0))
    in_specs = [row(K), const(K, D), row(D), const(1, D)]
    out_specs = [row(D), row(D)]
    out_shape = [jax.ShapeDtypeStruct((T, D), F32), jax.ShapeDtypeStruct((T, D), a_dtype)]
    args = [x, w, h, gain.reshape(1, D).astype(F32)]
    if route:
        in_specs.append(const(D, LANES))
        out_specs.append(row(LANES))
        out_shape.append(jax.ShapeDtypeStruct((T, LANES), F32))
        args.append(jnp.zeros((D, LANES), BF16).at[:, :N_EXPERTS].set(w_router.astype(BF16)))
    return pl.pallas_call(
        functools.partial(_proj_res_norm_kernel, route=route),
        grid=(T // tm,),
        in_specs=in_specs,
        out_specs=out_specs,
        out_shape=out_shape,
        compiler_params=_cparams(1, 48),
        name="proj_residual_norm",
    )(*args)


def _res_norm_kernel(y_ref, h_ref, g_ref, hn_ref, a_ref):
    hn = h_ref[...] + y_ref[...]
    hn_ref[...] = hn
    a = hn * lax.rsqrt(jnp.mean(hn * hn, axis=-1, keepdims=True) + EPS) * g_ref[...]
    a_ref[...] = a.astype(a_ref.dtype)


def residual_norm(y, h, gain, a_dtype, tm=512):
    T, D = y.shape
    tm = min(tm, T)
    row = pl.BlockSpec((tm, D), lambda i: (i, 0))
    return pl.pallas_call(
        _res_norm_kernel,
        grid=(T // tm,),
        in_specs=[row, row, pl.BlockSpec((1, D), lambda i: (0, 0))],
        out_specs=[row, row],
        out_shape=[jax.ShapeDtypeStruct((T, D), F32), jax.ShapeDtypeStruct((T, D), a_dtype)],
        compiler_params=_cparams(1),
        name="residual_norm",
    )(y, h, gain.reshape(1, D).astype(F32))


def _ffn_kernel(row0_ref, ex_ref, work_ref, nblk_ref, tail_ref,
                x_ref, wg_ref, wu_ref, wd_ref, o_ref, acc, wgb, wub, wdb, sem, *, n_f):
    v = pl.program_id(0)
    j = pl.program_id(1)
    rlo = 0
    rhi = nblk_ref[v]

    last_j = j == n_f - 1
    row0 = pl.multiple_of(row0_ref[v], FFN_TM)

    @pl.when(jnp.logical_and(rhi > 0, j == 0))
    def _():
        acc[...] = jnp.zeros_like(acc)

    def out_copy(t):
        return pltpu.make_async_copy(acc.at[pl.ds(pl.multiple_of(t * FFN_TM, FFN_TM), FFN_TM), :],
                                     o_ref.at[pl.ds(pl.multiple_of(row0 + t * FFN_TM, FFN_TM), FFN_TM), :], sem)

    def row_blocks(r, n):
        rows = [pl.ds(pl.multiple_of((r + t) * FFN_TM, FFN_TM), FFN_TM) for t in range(n)]
        hid = []
        for t in range(n):
            xb = x_ref[rows[t], :].astype(BF16)
            gate = _dot(xb, wgb[...])
            up = _dot(xb, wub[...])
            hid.append((gate * _sigmoid(gate) * up).astype(BF16))
        for t in range(n):
            acc[rows[t], :] += _dot(hid[t], wdb[...])

        @pl.when(last_j)
        def _():
            for t in range(n):
                out_copy(r + t).start()

    @pl.when(rhi > rlo)
    def _():
        wgb[...] = wg_ref[...].astype(BF16)
        wub[...] = wu_ref[...].astype(BF16)
        wdb[...] = wd_ref[...].astype(BF16)
        row_blocks(rlo, 1)
        rest = rhi - rlo - 1

        def pair(t, carry):
            row_blocks(rlo + 1 + 2 * t, 2)
            return carry

        lax.fori_loop(0, lax.shift_right_logical(rest, 1), pair, 0)

        @pl.when((rest & 1) == 1)
        def _():
            row_blocks(rhi - 1, 1)

    def zero_copy(t):
        dst = pl.multiple_of((tail_ref[0] + t) * FFN_TM, FFN_TM)
        return pltpu.make_async_copy(acc.at[pl.ds(0, FFN_TM), :], o_ref.at[pl.ds(dst, FFN_TM), :], sem)

    def wait_copies(make, n):
        def wait(t, carry):
            make(t).wait()
            return carry

        lax.fori_loop(0, n, wait, 0)

    @pl.when(jnp.logical_and(rhi > 0, last_j))
    def _():
        wait_copies(out_copy, rhi)

    @pl.when(jnp.logical_and(v == pl.num_programs(0) - 1, last_j))
    def _():
        acc[0:FFN_TM, :] = jnp.zeros((FFN_TM, acc.shape[1]), F32)

        def start(t, carry):
            zero_copy(t).start()
            return carry

        lax.fori_loop(0, tail_ref[1], start, 0)
        wait_copies(zero_copy, tail_ref[1])


def ffn_schedule(pad_start, pad_end, n_rows):
    E = pad_start.shape[0]
    kmax = n_rows // FFN_RB
    nv = min(kmax + E, kmax * E)
    nb = (pad_end - pad_start) // FFN_TM
    n_win = (nb + FFN_RB // FFN_TM - 1) // (FFN_RB // FFN_TM)
    per_win = (nb + jnp.maximum(n_win, 1) - 1) // jnp.maximum(n_win, 1)
    k = jnp.arange(kmax, dtype=jnp.int32)[None, :]
    start = pad_start[:, None] + k * per_win[:, None] * FFN_TM
    flat = (k < n_win[:, None]).reshape(-1)
    nblk = jnp.clip(nb[:, None] - k * per_win[:, None], 0, per_win[:, None]).reshape(-1)
    order = jnp.argsort(jnp.logical_not(flat), stable=True)[:nv].astype(jnp.int32)
    n_valid = jnp.sum(flat.astype(jnp.int32))
    valid = jnp.arange(nv, dtype=jnp.int32) < n_valid
    order = jnp.where(valid, order, order[jnp.maximum(n_valid - 1, 0)])
    row0 = start.reshape(-1)[order].astype(jnp.int32)
    ex = (order // kmax).astype(jnp.int32)
    return row0, ex, valid.astype(jnp.int32), jnp.where(valid, nblk[order], 0).astype(jnp.int32)


def grouped_ffn(x, w_gu, w_down, sched, n_out, rows_end):
    R, D = x.shape
    E, _, F2 = w_gu.shape
    F = F2 // 2
    n_f = F // FFN_TF
    row0, ex, work, nblk = sched
    nv = row0.shape[0]
    tail = jnp.stack([rows_end // FFN_TM, (n_out - rows_end) // FFN_TM]).astype(jnp.int32)

    def jf(j, work_ref, v):
        return jnp.where(work_ref[v] == 1, j, n_f - 1)

    grid_spec = pltpu.PrefetchScalarGridSpec(
        num_scalar_prefetch=5,
        grid=(nv, n_f),
        in_specs=[
            pl.BlockSpec((pl.Element(FFN_RB), pl.Element(D)),
                         lambda v, j, row0, ex, wk, nb, tl: (pl.multiple_of(row0[v], FFN_TM), 0),
                         pipeline_mode=pl.Buffered(1)),
            pl.BlockSpec((None, D, FFN_TF), lambda v, j, row0, ex, wk, nb, tl: (ex[v], 0, jf(j, wk, v))),
            pl.BlockSpec((None, D, FFN_TF), lambda v, j, row0, ex, wk, nb, tl: (ex[v], 0, n_f + jf(j, wk, v))),
            pl.BlockSpec((None, FFN_TF, D), lambda v, j, row0, ex, wk, nb, tl: (ex[v], jf(j, wk, v), 0)),
        ],
        out_specs=pl.BlockSpec(memory_space=pl.ANY),
        scratch_shapes=[pltpu.VMEM((FFN_RB, D), F32),
                        pltpu.VMEM((D, FFN_TF), BF16), pltpu.VMEM((D, FFN_TF), BF16),
                        pltpu.VMEM((FFN_TF, D), BF16), pltpu.SemaphoreType.DMA(())],
    )
    return pl.pallas_call(
        functools.partial(_ffn_kernel, n_f=n_f),
        grid_spec=grid_spec,
        out_shape=jax.ShapeDtypeStruct((n_out, D), F32),
        compiler_params=_cparams(2, 60),
        name="grouped_ffn",
    )(row0, ex, work, nblk, tail, x, w_gu, w_gu, w_down)


ROW_DMA_UNROLL = 8


def _gather_kernel(tok_ref, live_ref, a_ref, o_ref, buf, sem, *, tm):
    i = pl.program_id(0)
    n_live = live_ref[0]

    def issue(step, slot):
        base = step * tm

        def body(g, carry):
            for u in range(ROW_DMA_UNROLL):
                r = g * ROW_DMA_UNROLL + u
                pltpu.make_async_copy(a_ref.at[pl.ds(tok_ref[base + r], 1), :],
                                      buf.at[slot, pl.ds(r, 1), :], sem.at[slot]).start(priority=u % 2)
            return carry

        lax.fori_loop(0, tm // ROW_DMA_UNROLL, body, 0)

    @pl.when(i == 0)
    def _():
        issue(0, 0)

    @pl.when(i + 1 < n_live)
    def _():
        issue(i + 1, (i + 1) & 1)

    @pl.when(i < n_live)
    def _():
        slot = i & 1
        pltpu.make_async_copy(a_ref.at[pl.ds(0, tm), :], buf.at[slot], sem.at[slot]).wait()
        o_ref[...] = buf[slot].astype(o_ref.dtype)

    @pl.when(i >= n_live)
    def _():
        o_ref[...] = jnp.zeros_like(o_ref)


def gather_rows(a, slot_tok, rows_end, tm=512):
    T, D = a.shape
    R = slot_tok.shape[0]
    tm = min(tm, T)
    n_live = jnp.maximum((rows_end + tm - 1) // tm, 1).astype(jnp.int32).reshape(1)
    grid_spec = pltpu.PrefetchScalarGridSpec(
        num_scalar_prefetch=2,
        grid=(R // tm,),
        in_specs=[pl.BlockSpec(memory_space=pl.ANY)],
        out_specs=pl.BlockSpec((tm, D), lambda i, tok, live: (i, 0)),
        scratch_shapes=[pltpu.VMEM((2, tm, D), a.dtype), pltpu.SemaphoreType.DMA((2,))],
    )
    return pl.pallas_call(
        functools.partial(_gather_kernel, tm=tm),
        grid_spec=grid_spec,
        out_shape=jax.ShapeDtypeStruct((R, D), BF16),
        compiler_params=_cparams(1, 48),
        name="gather_rows",
    )(slot_tok, n_live, a)


def _combine_kernel(dest_ref, y_ref, rw_ref, h_ref, g_ref, hn_ref, a_ref, buf, sem, *, tm, n_steps):
    i = pl.program_id(0)

    def issue(step, slot):
        base = step * tm

        def body(g, carry):
            for u in range(ROW_DMA_UNROLL // TOP_K):
                r = g * (ROW_DMA_UNROLL // TOP_K) + u
                for k in range(TOP_K):
                    src = dest_ref[(base + r) * TOP_K + k]
                    pltpu.make_async_copy(y_ref.at[pl.ds(src, 1), :], buf.at[slot, k, pl.ds(r, 1), :],
                                          sem.at[slot]).start(priority=k % 2)
            return carry

        lax.fori_loop(0, tm // (ROW_DMA_UNROLL // TOP_K), body, 0)

    @pl.when(i == 0)
    def _():
        issue(0, 0)

    @pl.when(i + 1 < n_steps)
    def _():
        issue(i + 1, (i + 1) & 1)

    slot = i & 1
    for k in range(TOP_K):
        pltpu.make_async_copy(y_ref.at[pl.ds(0, tm), :], buf.at[slot, k], sem.at[slot]).wait()
    rw = rw_ref[...]
    hn = h_ref[...] + rw[:, 2:3] * buf[slot, 0] + rw[:, 3:4] * buf[slot, 1]
    hn_ref[...] = hn
    a = hn * lax.rsqrt(jnp.mean(hn * hn, axis=-1, keepdims=True) + EPS) * g_ref[...]
    a_ref[...] = a.astype(a_ref.dtype)


def combine_residual_norm(y_sorted, dest, rw, h, gain, a_dtype, tm=256):
    T, D = h.shape
    tm = min(tm, T)
    n_steps = T // tm
    row = pl.BlockSpec((tm, D), lambda i, d: (i, 0))
    grid_spec = pltpu.PrefetchScalarGridSpec(
        num_scalar_prefetch=1,
        grid=(n_steps,),
        in_specs=[pl.BlockSpec(memory_space=pl.ANY),
                  pl.BlockSpec((tm, LANES), lambda i, d: (i, 0)),
                  row,
                  pl.BlockSpec((1, D), lambda i, d: (0, 0))],
        out_specs=[row, row],
        scratch_shapes=[pltpu.VMEM((2, TOP_K, tm, D), F32), pltpu.SemaphoreType.DMA((2,))],
    )
    return pl.pallas_call(
        functools.partial(_combine_kernel, tm=tm, n_steps=n_steps),
        grid_spec=grid_spec,
        out_shape=[jax.ShapeDtypeStruct((T, D), F32), jax.ShapeDtypeStruct((T, D), a_dtype)],
        compiler_params=_cparams(1, 48),
        name="combine_residual_norm",
    )(dest, y_sorted, rw, h, gain.reshape(1, D).astype(F32))


def mixer_layer(h, a, l, lb, mem_n, cos_t, sin_t, p, batch, seq, n_mem, next_gain, a_dtype, w_router):
    D = h.shape[1]
    n_a = 4 * HG_HEADS * HG_D + 2 * MLA_RANK
    kr0 = n_a
    q_mem0 = kr0 + MLA_ROPE
    z1 = matmul_wt(a, p["w_in_t"], layer=l, row0=0, n_cols=n_a, tn=1024, tm=1024, out_dtype=F32,
                   name="in_proj_a")
    n_b = MEM_HEADS * MEM_HEAD_DIM + 3 * D
    z2 = matmul_wt(a, p["w_in_t"], layer=l, row0=q_mem0, n_cols=n_b, tn=1024, tm=1024, out_dtype=BF16,
                   name="in_proj_b")
    wq1, wq2, wkv = mla_weights(p["w_q_up"][l], p["w_kv_up"][l])
    wkr = rope_key_weight(p["w_in_t"], l, kr0)

    y_hg = hgrn_mix(z1, lb, p["hg_norm"][l], batch, seq)
    tq = min(ATTN_Q_ROWS, seq)
    tk = min(ATTN_BLOCK, tq)
    qp, kt, v = mla_prep(z1, a, cos_t, sin_t, p["mla_q_norm"][l], p["mla_kv_norm"][l], wq1, wq2, wkv,
                         wkr, tk)
    y_mla = causal_attention(qp, kt, v, batch, seq, tq, tk)
    mem_kv = matmul(mem_n, p["w_mem_kv"], layer=l, n_cols=2 * MEM_HEADS * MEM_HEAD_DIM, tn=512,
                    tm=1024, out_dtype=BF16, name="mem_kv")
    y_mem = memory_attention(z2, mem_kv, batch, seq, n_mem)
    merged = branch_merge(y_hg, y_mla, y_mem, z2, p["w_branch"][l].astype(BF16))
    return proj_residual_norm(merged, p["w_o"][l].astype(BF16), h, next_gain, a_dtype, w_router)


def group_bounds(starts, ends, index, n_groups, n_rows):
    E = starts.shape[0]
    before = jnp.zeros((index * E,), jnp.int32)
    after = jnp.full(((n_groups - index - 1) * E,), n_rows, jnp.int32)
    return (jnp.concatenate([before, starts.astype(jnp.int32), after]),
            jnp.concatenate([before, ends.astype(jnp.int32), after]))


def moe_routing(rw, n_tokens):
    A = n_tokens * TOP_K
    flat_e = rw[:, :TOP_K].astype(jnp.int32).reshape(A)
    onehot = (flat_e[:, None] == jnp.arange(N_EXPERTS, dtype=jnp.int32)[None, :]).astype(jnp.int32)
    csum = jnp.cumsum(onehot, axis=0)
    rank = jnp.take_along_axis(csum, flat_e[:, None], axis=1)[:, 0] - 1
    counts = csum[-1]
    padded = (counts + FFN_TM - 1) // FFN_TM * FFN_TM
    pad_end = jnp.cumsum(padded).astype(jnp.int32)
    pad_start = pad_end - padded
    dest = (pad_start[flat_e] + rank).astype(jnp.int32)
    n_rows = (A + N_EXPERTS * FFN_TM + FFN_RB - 1) // FFN_RB * FFN_RB + FFN_RB
    filler = jnp.arange(n_rows, dtype=jnp.int32) % n_tokens
    slot_tok = filler.at[dest].set(jnp.arange(A, dtype=jnp.int32) // TOP_K)
    return dest, slot_tok, pad_start, pad_end, n_rows


def kernel(x, mem, positions, hg_lb_logits, attn_norm, w_in, hg_norm, mla_q_norm, w_q_up, mla_kv_norm,
           w_kv_up, mem_norm, w_mem_kv, w_branch, w_o, ffn_norm, w_dense_gu, w_dense_down, w_router,
           w_expert_gu, w_expert_down, final_norm):
    batch, seq, D = x.shape
    n_mem = mem.shape[1]
    depth = w_in.shape[0]
    T = batch * seq
    p = dict(w_in_t=jnp.swapaxes(w_in, 1, 2), hg_norm=hg_norm, mla_q_norm=mla_q_norm, w_q_up=w_q_up, mla_kv_norm=mla_kv_norm,
             w_kv_up=w_kv_up, w_mem_kv=w_mem_kv, w_branch=w_branch, w_o=w_o)

    cos_t, sin_t = rope_tables(positions.astype(jnp.int32))
    lb_sm = jax.nn.softmax(hg_lb_logits.astype(F32), axis=0)
    lb_all = jnp.clip(jnp.cumsum(lb_sm, axis=0) - lb_sm[0:1], 0.0, 1.0)
    mem_n = rmsnorm_cast(mem.reshape(batch * n_mem, D), mem_norm, BF16)

    n_dense, n_moe = w_dense_gu.shape[0], w_expert_gu.shape[0]
    w_moe_gu = w_expert_gu.reshape((n_moe * N_EXPERTS,) + w_expert_gu.shape[2:])
    w_moe_down = w_expert_down.reshape((n_moe * N_EXPERTS,) + w_expert_down.shape[2:])

    h = x.reshape(T, D)
    a = rmsnorm_cast(h, attn_norm[0], BF16)
    for l in range(depth):
        moe = l % 2 == 1
        last = l + 1 == depth
        next_gain = final_norm if last else attn_norm[l + 1]
        next_dtype = F32 if last else BF16
        h, a_ffn, *routing = mixer_layer(h, a, l, lb_all[l], mem_n, cos_t, sin_t, p, batch, seq, n_mem,
                                         ffn_norm[l], F32 if moe else BF16,
                                         w_router[l // 2] if moe else None)
        if not moe:
            zero = jnp.zeros((1,), jnp.int32)
            starts, ends = group_bounds(zero, zero + T, l // 2, n_dense, T)
            y = grouped_ffn(a_ffn, w_dense_gu, w_dense_down, ffn_schedule(starts, ends, T), T,
                            jnp.int32(T))
            h, a = residual_norm(y, h, next_gain, next_dtype)
        else:
            (rw,) = routing
            dest, slot_tok, pad_start, pad_end, n_rows = moe_routing(rw, T)
            xs = gather_rows(a_ffn, slot_tok, pad_end[-1])
            starts, ends = group_bounds(pad_start, pad_end, l // 2, n_moe, n_rows)
            ys = grouped_ffn(xs, w_moe_gu, w_moe_down, ffn_schedule(starts, ends, n_rows),
                             n_rows - FFN_RB, pad_end[-1])
            h, a = combine_residual_norm(ys, dest, rw, h, next_gain, next_dtype)
    return a.reshape(batch, seq, D)
```

```python
import functools

import numpy as np
import jax
import jax.numpy as jnp
from jax import lax
from jax.experimental import pallas as pl
from jax.experimental.pallas import tpu as pltpu

F32 = jnp.float32
BF16 = jnp.bfloat16

EPS = 1e-6
NEG_BIG = -1e30
MIN_FORGET = 1e-20
ROPE_THETA = 10000.0

HG_HEADS = 8
HG_D = 128
HG_CHUNK = 64
HG_SUB = 16
MLA_HEADS = 8
MLA_RANK = 512
MLA_NOPE = 128
MLA_ROPE = 64
MLA_V = 128
MLA_QK_PAD = 256
MEM_HEADS = 4
MEM_HEAD_DIM = 256
N_EXPERTS = 8
TOP_K = 2

LOG2E = 1.4426950408889634

LANES = 128
SUBLANES = 8
ATTN_BLOCK = 512
ATTN_Q_ROWS = 512
FFN_TM = 256
FFN_RB = 2048
FFN_TF = 512
ROW_DMA_UNROLL = 8
VMEM_MB = 48
VMEM_FFN_MB = 60


def _cparams(n_axes, vmem_mb=None):
    kw = dict(dimension_semantics=("arbitrary",) * n_axes)
    if vmem_mb is not None:
        kw["vmem_limit_bytes"] = vmem_mb * 1024 * 1024
    return pltpu.CompilerParams(**kw)


def _sigmoid(x):
    return 1.0 / (1.0 + jnp.exp(-x))


def _dot(a, b):
    return jnp.dot(a, b, preferred_element_type=F32)


def _dot_nt(a, b):
    return lax.dot_general(a, b, (((1,), (1,)), ((), ())), preferred_element_type=F32)


def _rope_kernel(pos_ref, c_ref, cos_ref, sin_ref):
    ang = pos_ref[...].astype(F32) * c_ref[0:1, :]
    cos_ref[...] = jnp.cos(ang) * c_ref[1:2, :]
    sin_ref[...] = jnp.sin(ang) * c_ref[2:3, :]


def rope_tables(positions, tm=1024):
    T = positions.size
    half = MLA_ROPE // 2
    inv = ROPE_THETA ** (-(np.arange(half, dtype=np.float32) / half))
    consts = np.zeros((8, LANES), np.float32)
    consts[0, :half] = inv
    consts[0, half:2 * half] = inv
    consts[1, :2 * half] = 1.0
    consts[2, :half] = -1.0
    consts[2, half:2 * half] = 1.0
    tm = min(tm, T)
    out = jax.ShapeDtypeStruct((T, LANES), F32)
    return pl.pallas_call(
        _rope_kernel,
        grid=(T // tm,),
        in_specs=[pl.BlockSpec((tm, 1), lambda i: (i, 0)),
                  pl.BlockSpec((8, LANES), lambda i: (0, 0))],
        out_specs=[pl.BlockSpec((tm, LANES), lambda i: (i, 0)),
                   pl.BlockSpec((tm, LANES), lambda i: (i, 0))],
        out_shape=[out, out],
        compiler_params=_cparams(1),
        name="rope_tables",
    )(positions.reshape(T, 1), jnp.asarray(consts))


def _rmsnorm_kernel(x_ref, g_ref, o_ref):
    x = x_ref[...].astype(F32)
    y = x * lax.rsqrt(jnp.mean(x * x, axis=-1, keepdims=True) + EPS)
    o_ref[...] = (y * g_ref[...]).astype(o_ref.dtype)


def rmsnorm_cast(x, g, out_dtype, tm=512):
    M, D = x.shape
    tm = min(tm, M)
    return pl.pallas_call(
        _rmsnorm_kernel,
        grid=(M // tm,),
        in_specs=[pl.BlockSpec((tm, D), lambda i: (i, 0)),
                  pl.BlockSpec((1, D), lambda i: (0, 0))],
        out_specs=pl.BlockSpec((tm, D), lambda i: (i, 0)),
        out_shape=jax.ShapeDtypeStruct((M, D), out_dtype),
        compiler_params=_cparams(1),
        name="rmsnorm",
    )(x, g.reshape(1, D).astype(F32))


def _mm_kernel(x_ref, w_ref, o_ref, *scratch, cast):
    if cast:
        (wb,) = scratch

        @pl.when(pl.program_id(1) == 0)
        def _():
            wb[...] = w_ref[...].astype(BF16)

        w = wb[...]
    else:
        w = w_ref[...]
    o_ref[...] = _dot(x_ref[...], w).astype(o_ref.dtype)


def matmul(x, w, *, n_cols, tn, tm, out_dtype, layer=None, name="matmul"):
    M, K = x.shape
    tm = min(tm, M)
    cast = w.dtype != BF16
    if layer is None:
        w_spec = pl.BlockSpec((K, tn), lambda j, i: (0, j))
    else:
        w_spec = pl.BlockSpec((None, K, tn), lambda j, i: (layer, 0, j))
    scratch = [pltpu.VMEM((K, tn), BF16)] if cast else []
    return pl.pallas_call(
        functools.partial(_mm_kernel, cast=cast),
        grid=(n_cols // tn, M // tm),
        in_specs=[pl.BlockSpec((tm, K), lambda j, i: (i, 0)), w_spec],
        out_specs=pl.BlockSpec((tm, tn), lambda j, i: (i, j)),
        out_shape=jax.ShapeDtypeStruct((M, n_cols), out_dtype),
        scratch_shapes=scratch,
        compiler_params=_cparams(2, VMEM_MB),
        name=name,
    )(x, w)


def _mm_wt_kernel(x_ref, wt_ref, o_ref, wb):
    @pl.when(pl.program_id(1) == 0)
    def _():
        wb[...] = wt_ref[...].T.astype(BF16)

    o_ref[...] = _dot(x_ref[...], wb[...]).astype(o_ref.dtype)


def matmul_wt(x, wt, *, layer, row0, n_cols, tn, tm, out_dtype, name):
    M, K = x.shape
    tm = min(tm, M)
    assert n_cols % tn == 0
    if row0 % tn == 0:
        w_spec = pl.BlockSpec((None, tn, K), lambda j, i: (layer, row0 // tn + j, 0))
    else:
        w_spec = pl.BlockSpec((pl.Squeezed(), pl.Element(tn), pl.Element(K)),
                              lambda j, i: (layer, pl.multiple_of(row0 + j * tn, 8), 0))
    return pl.pallas_call(
        _mm_wt_kernel,
        grid=(n_cols // tn, M // tm),
        in_specs=[pl.BlockSpec((tm, K), lambda j, i: (i, 0)), w_spec],
        out_specs=pl.BlockSpec((tm, tn), lambda j, i: (i, j)),
        out_shape=jax.ShapeDtypeStruct((M, n_cols), out_dtype),
        scratch_shapes=[pltpu.VMEM((K, tn), BF16)],
        compiler_params=_cparams(2, VMEM_MB),
        name=name,
    )(x, wt)


def _rope_key_weight_kernel(wt_ref, o_ref):
    half = MLA_ROPE // 2
    w = wt_ref[...]
    zero = jnp.zeros((LANES - MLA_ROPE, w.shape[1]), F32)
    rows = jnp.concatenate([w, zero, w[half:], w[:half], zero], axis=0)
    o_ref[...] = rows.T.astype(BF16)


def rope_key_weight(wt, layer, row0):
    K = wt.shape[2]
    assert row0 % MLA_ROPE == 0
    return pl.pallas_call(
        _rope_key_weight_kernel,
        grid=(1,),
        in_specs=[pl.BlockSpec((None, MLA_ROPE, K), lambda i: (layer, row0 // MLA_ROPE, 0))],
        out_specs=pl.BlockSpec((K, 2 * LANES), lambda i: (0, 0)),
        out_shape=jax.ShapeDtypeStruct((K, 2 * LANES), BF16),
        compiler_params=_cparams(1),
        name="rope_key_weight",
    )(wt)


def _hgrn_kernel(q_ref, f_ref, i_ref, g_ref, lb_ref, gn_ref, tri_ref, ones_ref, dmask_ref, o_ref,
                 st_ref, bk_scr):
    L, SUB, D, G = HG_CHUNK, HG_SUB, HG_D, SUBLANES

    @pl.when(pl.program_id(1) == 0)
    def _():
        st_ref[...] = jnp.zeros_like(st_ref)

    lb = lb_ref[...]
    f = lb + (1.0 - lb) * _sigmoid(f_ref[...])
    lf = jnp.log(jnp.maximum(f, MIN_FORGET)) * LOG2E
    kk = 1.0 - f
    tri = tri_ref[...]
    p1 = lf.astype(BF16)
    r1 = lf - p1.astype(F32)
    p2 = r1.astype(BF16)
    p3 = (r1 - p2.astype(F32)).astype(BF16)
    b = _dot(tri, p1) + _dot(tri, p2) + _dot(tri, p3)

    q = q_ref[...]
    v = i_ref[...]
    g = g_ref[...]
    gn = gn_ref[...]
    ones = ones_ref[...]
    bk = b - jnp.log2(jnp.maximum(kk, 0.0))
    bk_scr[...] = bk
    b_last = b[L - 1:L, :]
    q_in = (q * jnp.exp2(b)).astype(BF16)
    k_dec = jnp.exp2(b_last - bk).astype(BF16)
    e_last = jnp.exp2(b_last)

    def first_matmuls(h):
        hs = slice(h * D, (h + 1) * D)
        bh, qh, bkh, vh = b[:, hs], q[:, hs], bk[:, hs], v[:, hs]
        vb = vh.astype(BF16)
        st = st_ref[h]
        inter = _dot_nt(q_in[:, hs], st.astype(BF16))
        st_ref[h] = e_last[:, hs] * st + lax.dot_general(
            vb, k_dec[:, hs], (((0,), (0,)), ((), ())), preferred_element_type=F32)
        ssums, off_parts = [], []
        for i in range(L // SUB):
            r0 = i * SUB
            bs, qs = bh[r0:r0 + SUB], qh[r0:r0 + SUB]
            terms = []
            for s in range(SUB):
                bk_s = bk_scr[r0 + s:r0 + s + 1, hs]
                for grp in range(s // G, SUB // G):
                    rows = slice(grp * G, (grp + 1) * G)
                    d = bs[rows] - bk_s
                    if grp == s // G and s % G:
                        d = d + dmask_ref[s % G]
                    terms.append(qs[rows] * jnp.exp2(d))
            ssums.append(_dot(jnp.concatenate(terms, axis=0).astype(BF16), ones))
            if i == 0:
                off_parts.append(jnp.zeros((SUB, L), F32))
            else:
                ref_row = bh[r0 - 1:r0]
                q_i = (qs * jnp.exp2(bs - ref_row)).astype(BF16)
                k_i = jnp.concatenate([jnp.exp2(ref_row - bkh[0:r0]), jnp.zeros((L - r0, D), F32)],
                                      axis=0).astype(BF16)
                off_parts.append(_dot_nt(q_i, k_i))
        return inter, ssums, off_parts, vh, vb

    def finish(h, inter, ssums, off_parts, vh, vb):
        hs = slice(h * D, (h + 1) * D)
        diag_parts = []
        for i in range(L // SUB):
            acc = [None] * (SUB // G)
            t = 0
            for s in range(SUB):
                v_s = i_ref[i * SUB + s:i * SUB + s + 1, hs]
                for grp in range(s // G, SUB // G):
                    term = ssums[i][t * G:(t + 1) * G] * v_s
                    acc[grp] = term if acc[grp] is None else acc[grp] + term
                    t += 1
            diag_parts.extend(acc)
        s_off = jnp.concatenate(off_parts, axis=0).astype(BF16)
        o = inter + jnp.concatenate(diag_parts, axis=0) + _dot(s_off, vb)
        on = o * lax.rsqrt(jnp.mean(o * o, axis=-1, keepdims=True) + EPS) * gn[:, hs]
        gh = g[:, hs]
        o_ref[:, hs] = (on * (gh * _sigmoid(gh))).astype(o_ref.dtype)

    pending = None
    for h in range(HG_HEADS):
        cur = first_matmuls(h)
        if pending is not None:
            finish(h - 1, *pending)
        pending = cur
    finish(HG_HEADS - 1, *pending)


def hgrn_mix(z1, lb, hg_norm, batch, seq):
    T = batch * seq
    W = HG_HEADS * HG_D
    L = HG_CHUNK
    nc = seq // L
    tri = jnp.asarray(np.tril(np.ones((L, L), np.float32)), BF16)
    ones = jnp.ones((HG_D, HG_D), BF16)
    G = SUBLANES
    dmask_np = np.where(np.arange(G)[None, :, None] >= np.arange(G)[:, None, None], 0.0, NEG_BIG)
    dmask = jnp.asarray(np.broadcast_to(dmask_np, (G, G, HG_D)).astype(np.float32))

    def col(c):
        return pl.BlockSpec((L, W), lambda b, j, c=c: (b * nc + j, c))

    const = lambda shape: pl.BlockSpec(shape, lambda b, j: (0,) * len(shape))
    return pl.pallas_call(
        _hgrn_kernel,
        grid=(batch, nc),
        in_specs=[col(0), col(1), col(2), col(3), const((1, W)), const((1, W)),
                  const((L, L)), const((HG_D, HG_D)), const((G, G, HG_D))],
        out_specs=pl.BlockSpec((L, W), lambda b, j: (b * nc + j, 0)),
        out_shape=jax.ShapeDtypeStruct((T, W), BF16),
        scratch_shapes=[pltpu.VMEM((HG_HEADS, HG_D, HG_D), F32), pltpu.VMEM((L, W), F32)],
        compiler_params=_cparams(2, VMEM_MB),
        name="hgrn2",
    )(z1, z1, z1, z1, lb.reshape(1, W).astype(F32),
      jnp.tile(hg_norm.astype(F32), HG_HEADS).reshape(1, W), tri, ones, dmask)


def _mla_prep_kernel(cq_ref, ckv_ref, a_ref, cos_ref, sin_ref, gq_ref, gkv_ref,
                     wq1_ref, wq2_ref, wkv_ref, wkr_ref, q_ref, kt_ref, v_ref):
    scale = (MLA_NOPE + MLA_ROPE) ** -0.5 * LOG2E
    cos = cos_ref[...]
    sin = sin_ref[...]

    def norm(x, g):
        return (x * lax.rsqrt(jnp.mean(x * x, axis=-1, keepdims=True) + EPS) * g).astype(BF16)

    cn = norm(cq_ref[...], gq_ref[...])
    q1 = _dot(cn, wq1_ref[...])
    q2 = _dot(cn, wq2_ref[...])
    cvn = norm(ckv_ref[...], gkv_ref[...])
    kv = _dot(cvn, wkv_ref[...])
    kr = _dot(a_ref[...], wkr_ref[...])
    kr_rot_t = (kr[:, 0:LANES] * cos + kr[:, LANES:2 * LANES] * sin).T.astype(BF16)
    P = MLA_QK_PAD
    for h in range(MLA_HEADS):
        q_ref[:, h * P:h * P + LANES] = (q1[:, h * P:h * P + LANES] * scale).astype(BF16)
        q_ref[:, h * P + LANES:(h + 1) * P] = (
            (q1[:, h * P + LANES:(h + 1) * P] * cos + q2[:, h * LANES:(h + 1) * LANES] * sin) * scale
        ).astype(BF16)
        kt_ref[h * P:h * P + LANES, :] = kv[:, h * LANES:(h + 1) * LANES].T.astype(BF16)
        kt_ref[h * P + LANES:(h + 1) * P, :] = kr_rot_t
    nv = MLA_HEADS * MLA_V
    v_ref[...] = kv[:, nv:2 * nv].astype(BF16)


def mla_prep(z1, a, cos_t, sin_t, gq, gkv, wq1, wq2, wkv, wkr, tm):
    T = z1.shape[0]
    R = MLA_RANK
    cq_blk = (4 * HG_HEADS * HG_D) // R
    row = lambda w: pl.BlockSpec((tm, w), lambda i: (i, 0))
    const = lambda a: pl.BlockSpec(a.shape, lambda i: (0, 0))
    qk_w = MLA_HEADS * MLA_QK_PAD
    return pl.pallas_call(
        _mla_prep_kernel,
        grid=(T // tm,),
        in_specs=[pl.BlockSpec((tm, R), lambda i: (i, cq_blk)),
                  pl.BlockSpec((tm, R), lambda i: (i, cq_blk + 1)),
                  row(a.shape[1]), row(LANES), row(LANES),
                  pl.BlockSpec((1, R), lambda i: (0, 0)), pl.BlockSpec((1, R), lambda i: (0, 0)),
                  const(wq1), const(wq2), const(wkv), const(wkr)],
        out_specs=[row(qk_w), pl.BlockSpec((None, qk_w, tm), lambda i: (i, 0, 0)), row(MLA_HEADS * MLA_V)],
        out_shape=[jax.ShapeDtypeStruct((T, qk_w), BF16), jax.ShapeDtypeStruct((T // tm, qk_w, tm), BF16),
                   jax.ShapeDtypeStruct((T, MLA_HEADS * MLA_V), BF16)],
        compiler_params=_cparams(1, VMEM_MB),
        name="mla_prep",
    )(z1, z1, a, cos_t, sin_t, gq.reshape(1, R).astype(F32), gkv.reshape(1, R).astype(F32),
      wq1, wq2, wkv, wkr)


def mla_weights(w_q_up, w_kv_up):
    R, half = MLA_RANK, MLA_ROPE // 2
    wq = w_q_up.reshape(R, MLA_HEADS, MLA_NOPE + MLA_ROPE)
    nope, rope = wq[..., :MLA_NOPE], wq[..., MLA_NOPE:]
    rope_sw = jnp.concatenate([rope[..., half:], rope[..., :half]], axis=-1)
    z64 = jnp.zeros((R, MLA_HEADS, MLA_QK_PAD - MLA_NOPE - MLA_ROPE), w_q_up.dtype)
    wq1 = jnp.concatenate([nope, rope, z64], axis=-1).reshape(R, MLA_HEADS * MLA_QK_PAD)
    wq2 = jnp.concatenate([rope_sw, z64], axis=-1).reshape(R, MLA_HEADS * LANES)
    wkv = w_kv_up.reshape(R, MLA_HEADS, MLA_NOPE + MLA_V)
    wkv = jnp.concatenate([wkv[..., :MLA_NOPE].reshape(R, -1), wkv[..., MLA_NOPE:].reshape(R, -1)], axis=-1)
    return wq1.astype(BF16), wq2.astype(BF16), wkv.astype(BF16)


def _causal_attn_kernel(q_ref, kt_ref, v_ref, o_ref, m_ref, l_ref, acc_ref, *, tq, tk):
    i = pl.program_id(1)
    H, P, DV = MLA_HEADS, MLA_QK_PAD, MLA_V
    n_c = tk // LANES
    n_diag = tq // tk
    m_ref[...] = jnp.full_like(m_ref, NEG_BIG)
    l_ref[...] = jnp.zeros_like(l_ref)
    acc_ref[...] = jnp.zeros_like(acc_ref)

    def scores(j, h):
        return _dot(q_ref[:, h * P:(h + 1) * P], kt_ref[j, h * P:(h + 1) * P, :])

    def softmax(h, s, col_off):
        if col_off is not None:
            rows = lax.broadcasted_iota(jnp.int32, (tq, tk), 0)
            cols = lax.broadcasted_iota(jnp.int32, (tq, tk), 1) + col_off
            s = jnp.where(cols <= rows, s, NEG_BIG)
        chunks = [s[:, c * LANES:(c + 1) * LANES] for c in range(n_c)]
        m_old = m_ref[h]
        m_new = jnp.maximum(m_old, jnp.max(functools.reduce(jnp.maximum, chunks), axis=-1, keepdims=True))
        ps = [jnp.exp2(c - m_new) for c in chunks]
        alpha = jnp.exp2(m_old - m_new)
        l_ref[h] = alpha * l_ref[h] + jnp.sum(functools.reduce(jnp.add, ps), axis=-1, keepdims=True)
        m_ref[h] = m_new
        return jnp.concatenate([c.astype(BF16) for c in ps], axis=1), alpha

    def values(j, h, p, alpha):
        start = pl.multiple_of(j * tk, tk)
        acc_ref[h] = alpha * acc_ref[h] + _dot(p, v_ref[pl.ds(start, tk), h * DV:(h + 1) * DV])

    def block(j, col_off):
        s, pa = {}, {}
        for step in range(H + 2):
            if step < H:
                s[step] = scores(j, step)
            if 1 <= step <= H:
                pa[step - 1] = softmax(step - 1, s.pop(step - 1), col_off)
            if step >= 2:
                values(j, step - 2, *pa.pop(step - 2))

    def body(j, carry):
        block(j, None)
        return carry

    lax.fori_loop(0, i * n_diag, body, 0)
    for d in range(n_diag):
        block(i * n_diag + d, d * tk)
    for h in range(H):
        o_ref[:, h * DV:(h + 1) * DV] = (acc_ref[h] / l_ref[h]).astype(o_ref.dtype)


def causal_attention(qp, kt, v, batch, seq, tq, tk):
    T = batch * seq
    nq, nk = seq // tq, seq // tk
    H, P, DV = MLA_HEADS, MLA_QK_PAD, MLA_V
    return pl.pallas_call(
        functools.partial(_causal_attn_kernel, tq=tq, tk=tk),
        grid=(batch, nq),
        in_specs=[pl.BlockSpec((tq, H * P), lambda b, i: (b * nq + i, 0)),
                  pl.BlockSpec((nk, H * P, tk), lambda b, i: (b, 0, 0)),
                  pl.BlockSpec((seq, H * DV), lambda b, i: (b, 0))],
        out_specs=pl.BlockSpec((tq, H * DV), lambda b, i: (b * nq + i, 0)),
        out_shape=jax.ShapeDtypeStruct((T, H * DV), BF16),
        scratch_shapes=[pltpu.VMEM((H, tq, LANES), F32), pltpu.VMEM((H, tq, LANES), F32),
                        pltpu.VMEM((H, tq, DV), F32)],
        compiler_params=_cparams(2, VMEM_MB),
        name="mla_attention",
    )(qp, kt, v)


def _mem_attn_kernel(q_ref, k_ref, v_ref, o_ref):
    scale = MEM_HEAD_DIM ** -0.5
    Dh = MEM_HEAD_DIM
    for h in range(MEM_HEADS):
        hs = slice(h * Dh, (h + 1) * Dh)
        s = _dot_nt(q_ref[:, hs], k_ref[:, hs]) * scale
        m = jnp.max(s, axis=-1, keepdims=True)
        p = jnp.exp(s - m)
        l = jnp.sum(p, axis=-1, keepdims=True)
        o = _dot(p.astype(BF16), v_ref[:, hs])
        o_ref[:, hs] = (o / l).astype(o_ref.dtype)


def memory_attention(z2, mem_kv, batch, seq, n_mem, tq=512):
    T = batch * seq
    tq = min(tq, seq)
    nq = seq // tq
    W = MEM_HEADS * MEM_HEAD_DIM
    return pl.pallas_call(
        _mem_attn_kernel,
        grid=(batch, nq),
        in_specs=[pl.BlockSpec((tq, W), lambda b, i: (b * nq + i, 0)),
                  pl.BlockSpec((n_mem, W), lambda b, i: (b, 0)),
                  pl.BlockSpec((n_mem, W), lambda b, i: (b, 1))],
        out_specs=pl.BlockSpec((tq, W), lambda b, i: (b * nq + i, 0)),
        out_shape=jax.ShapeDtypeStruct((T, W), BF16),
        compiler_params=_cparams(2, VMEM_MB),
        name="mem_attention",
    )(z2, mem_kv, mem_kv)


def _merge_kernel(y0_ref, y1_ref, y2_ref, g0_ref, g1_ref, g2_ref, w_ref, o_ref):
    acc = None
    for n, (y_ref, g_ref) in enumerate(((y0_ref, g0_ref), (y1_ref, g1_ref), (y2_ref, g2_ref))):
        term = _sigmoid(g_ref[...].astype(F32)) * _dot(y_ref[...], w_ref[n])
        acc = term if acc is None else acc + term
    o_ref[...] = acc.astype(o_ref.dtype)


def branch_merge(y_hg, y_mla, y_mem, z2, w_branch, layer, tm=1024, tn=512):
    T, W = y_hg.shape
    D = w_branch.shape[-1]
    tm = min(tm, T)
    g_off = W // tn

    def gate(n):
        return pl.BlockSpec((tm, tn), lambda i, j, n=n: (i, g_off + n * (D // tn) + j))

    y_spec = pl.BlockSpec((tm, W), lambda i, j: (i, 0))
    return pl.pallas_call(
        _merge_kernel,
        grid=(T // tm, D // tn),
        in_specs=[y_spec, y_spec, y_spec, gate(0), gate(1), gate(2),
                  pl.BlockSpec((None, 3, W, tn), lambda i, j: (layer, 0, 0, j))],
        out_specs=pl.BlockSpec((tm, tn), lambda i, j: (i, j)),
        out_shape=jax.ShapeDtypeStruct((T, D), BF16),
        compiler_params=_cparams(2, VMEM_MB),
        name="branch_merge",
    )(y_hg, y_mla, y_mem, z2, z2, z2, w_branch)


def _top2_route(logits):
    lane = lax.broadcasted_iota(jnp.int32, logits.shape, 1)
    lg = jnp.where(lane < N_EXPERTS, logits, NEG_BIG)
    m1 = jnp.max(lg, axis=-1, keepdims=True)
    i1 = jnp.min(jnp.where(lg == m1, lane, LANES), axis=-1, keepdims=True)
    lg2 = jnp.where(lane == i1, NEG_BIG, lg)
    m2 = jnp.max(lg2, axis=-1, keepdims=True)
    i2 = jnp.min(jnp.where(lg2 == m2, lane, LANES), axis=-1, keepdims=True)
    e2 = jnp.exp(m2 - m1)
    w1 = 1.0 / (1.0 + e2)
    w2 = e2 / (1.0 + e2)
    return jnp.where(lane == 0, i1.astype(F32),
                     jnp.where(lane == 1, i2.astype(F32),
                               jnp.where(lane == 2, w1, jnp.where(lane == 3, w2, 0.0))))


def _proj_res_norm_kernel(x_ref, w_ref, h_ref, g_ref, *rest, route):
    if route:
        wr_ref, hn_ref, a_ref, rw_ref = rest
    else:
        hn_ref, a_ref = rest
    hn = h_ref[...] + _dot(x_ref[...], w_ref[...])
    hn_ref[...] = hn
    a = hn * lax.rsqrt(jnp.mean(hn * hn, axis=-1, keepdims=True) + EPS) * g_ref[...]
    a_ref[...] = a.astype(a_ref.dtype)
    if route:
        rw_ref[...] = _top2_route(_dot(a.astype(BF16), wr_ref[...]))


def proj_residual_norm(x, w, layer, h, gain, a_dtype, w_router=None, tm=512):
    T, K = x.shape
    D = w.shape[2]
    tm = min(tm, T)
    route = w_router is not None
    row = lambda width: pl.BlockSpec((tm, width), lambda i: (i, 0))
    const = lambda r, c: pl.BlockSpec((r, c), lambda i: (0, 0))
    in_specs = [row(K), pl.BlockSpec((None, K, D), lambda i: (layer, 0, 0)), row(D), const(1, D)]
    out_specs = [row(D), row(D)]
    out_shape = [jax.ShapeDtypeStruct((T, D), F32), jax.ShapeDtypeStruct((T, D), a_dtype)]
    args = [x, w, h, gain.reshape(1, D).astype(F32)]
    if route:
        in_specs.append(const(D, LANES))
        out_specs.append(row(LANES))
        out_shape.append(jax.ShapeDtypeStruct((T, LANES), F32))
        args.append(jnp.zeros((D, LANES), BF16).at[:, :N_EXPERTS].set(w_router.astype(BF16)))
    return pl.pallas_call(
        functools.partial(_proj_res_norm_kernel, route=route),
        grid=(T // tm,),
        in_specs=in_specs,
        out_specs=out_specs,
        out_shape=out_shape,
        compiler_params=_cparams(1, VMEM_MB),
        name="proj_residual_norm",
    )(*args)


def _res_norm_kernel(y_ref, h_ref, g_ref, hn_ref, a_ref):
    hn = h_ref[...] + y_ref[...]
    hn_ref[...] = hn
    a = hn * lax.rsqrt(jnp.mean(hn * hn, axis=-1, keepdims=True) + EPS) * g_ref[...]
    a_ref[...] = a.astype(a_ref.dtype)


def residual_norm(y, h, gain, a_dtype, tm=512):
    T, D = y.shape
    tm = min(tm, T)
    row = pl.BlockSpec((tm, D), lambda i: (i, 0))
    return pl.pallas_call(
        _res_norm_kernel,
        grid=(T // tm,),
        in_specs=[row, row, pl.BlockSpec((1, D), lambda i: (0, 0))],
        out_specs=[row, row],
        out_shape=[jax.ShapeDtypeStruct((T, D), F32), jax.ShapeDtypeStruct((T, D), a_dtype)],
        compiler_params=_cparams(1),
        name="residual_norm",
    )(y, h, gain.reshape(1, D).astype(F32))


def _ffn_kernel(row0_ref, ex_ref, work_ref, nblk_ref, tail_ref,
                x_ref, wg_ref, wu_ref, wd_ref, o_ref, acc, wgb, wub, wdb, sem, *, n_f):
    v = pl.program_id(0)
    j = pl.program_id(1)
    rlo = 0
    rhi = nblk_ref[v]

    last_j = j == n_f - 1
    row0 = pl.multiple_of(row0_ref[v], FFN_TM)

    @pl.when(jnp.logical_and(rhi > 0, j == 0))
    def _():
        acc[...] = jnp.zeros_like(acc)

    def out_copy(t):
        return pltpu.make_async_copy(acc.at[pl.ds(pl.multiple_of(t * FFN_TM, FFN_TM), FFN_TM), :],
                                     o_ref.at[pl.ds(pl.multiple_of(row0 + t * FFN_TM, FFN_TM), FFN_TM), :], sem)

    def row_blocks(r, n):
        rows = [pl.ds(pl.multiple_of((r + t) * FFN_TM, FFN_TM), FFN_TM) for t in range(n)]
        hid = []
        for t in range(n):
            xb = x_ref[rows[t], :].astype(BF16)
            gate = _dot(xb, wgb[...])
            up = _dot(xb, wub[...])
            hid.append((gate * _sigmoid(gate) * up).astype(BF16))
        for t in range(n):
            acc[rows[t], :] += _dot(hid[t], wdb[...])

        @pl.when(last_j)
        def _():
            for t in range(n):
                out_copy(r + t).start()

    @pl.when(rhi > rlo)
    def _():
        wgb[...] = wg_ref[...].astype(BF16)
        wub[...] = wu_ref[...].astype(BF16)
        wdb[...] = wd_ref[...].astype(BF16)
        row_blocks(rlo, 1)
        rest = rhi - rlo - 1

        def pair(t, carry):
            row_blocks(rlo + 1 + 2 * t, 2)
            return carry

        lax.fori_loop(0, lax.shift_right_logical(rest, 1), pair, 0)

        @pl.when((rest & 1) == 1)
        def _():
            row_blocks(rhi - 1, 1)

    def zero_copy(t):
        dst = pl.multiple_of((tail_ref[0] + t) * FFN_TM, FFN_TM)
        return pltpu.make_async_copy(acc.at[pl.ds(0, FFN_TM), :], o_ref.at[pl.ds(dst, FFN_TM), :], sem)

    def wait_copies(make, n):
        def wait(t, carry):
            make(t).wait()
            return carry

        lax.fori_loop(0, n, wait, 0)

    @pl.when(jnp.logical_and(rhi > 0, last_j))
    def _():
        wait_copies(out_copy, rhi)

    @pl.when(jnp.logical_and(v == pl.num_programs(0) - 1, last_j))
    def _():
        acc[0:FFN_TM, :] = jnp.zeros((FFN_TM, acc.shape[1]), F32)

        def start(t, carry):
            zero_copy(t).start()
            return carry

        lax.fori_loop(0, tail_ref[1], start, 0)
        wait_copies(zero_copy, tail_ref[1])


def ffn_schedule(pad_start, pad_end, n_rows):
    E = pad_start.shape[0]
    kmax = n_rows // FFN_RB
    nv = min(kmax + E, kmax * E)
    nb = (pad_end - pad_start) // FFN_TM
    n_win = (nb + FFN_RB // FFN_TM - 1) // (FFN_RB // FFN_TM)
    per_win = (nb + jnp.maximum(n_win, 1) - 1) // jnp.maximum(n_win, 1)
    k = jnp.arange(kmax, dtype=jnp.int32)[None, :]
    start = pad_start[:, None] + k * per_win[:, None] * FFN_TM
    flat = (k < n_win[:, None]).reshape(-1)
    nblk = jnp.clip(nb[:, None] - k * per_win[:, None], 0, per_win[:, None]).reshape(-1)
    order = jnp.argsort(jnp.logical_not(flat), stable=True)[:nv].astype(jnp.int32)
    n_valid = jnp.sum(flat.astype(jnp.int32))
    valid = jnp.arange(nv, dtype=jnp.int32) < n_valid
    order = jnp.where(valid, order, order[jnp.maximum(n_valid - 1, 0)])
    row0 = start.reshape(-1)[order].astype(jnp.int32)
    ex = (order // kmax).astype(jnp.int32)
    return row0, ex, valid.astype(jnp.int32), jnp.where(valid, nblk[order], 0).astype(jnp.int32)


def grouped_ffn(x, w_gu, w_down, sched, n_out, rows_end):
    R, D = x.shape
    E, _, F2 = w_gu.shape
    F = F2 // 2
    n_f = F // FFN_TF
    row0, ex, work, nblk = sched
    nv = row0.shape[0]
    tail = jnp.stack([rows_end // FFN_TM, (n_out - rows_end) // FFN_TM]).astype(jnp.int32)

    def jf(j, work_ref, v):
        return jnp.where(work_ref[v] == 1, j, n_f - 1)

    grid_spec = pltpu.PrefetchScalarGridSpec(
        num_scalar_prefetch=5,
        grid=(nv, n_f),
        in_specs=[
            pl.BlockSpec((pl.Element(FFN_RB), pl.Element(D)),
                         lambda v, j, row0, ex, wk, nb, tl: (pl.multiple_of(row0[v], FFN_TM), 0),
                         pipeline_mode=pl.Buffered(1)),
            pl.BlockSpec((None, D, FFN_TF), lambda v, j, row0, ex, wk, nb, tl: (ex[v], 0, jf(j, wk, v))),
            pl.BlockSpec((None, D, FFN_TF), lambda v, j, row0, ex, wk, nb, tl: (ex[v], 0, n_f + jf(j, wk, v))),
            pl.BlockSpec((None, FFN_TF, D), lambda v, j, row0, ex, wk, nb, tl: (ex[v], jf(j, wk, v), 0)),
        ],
        out_specs=pl.BlockSpec(memory_space=pl.ANY),
        scratch_shapes=[pltpu.VMEM((FFN_RB, D), F32),
                        pltpu.VMEM((D, FFN_TF), BF16), pltpu.VMEM((D, FFN_TF), BF16),
                        pltpu.VMEM((FFN_TF, D), BF16), pltpu.SemaphoreType.DMA(())],
    )
    return pl.pallas_call(
        functools.partial(_ffn_kernel, n_f=n_f),
        grid_spec=grid_spec,
        out_shape=jax.ShapeDtypeStruct((n_out, D), F32),
        compiler_params=_cparams(2, VMEM_FFN_MB),
        name="grouped_ffn",
    )(row0, ex, work, nblk, tail, x, w_gu, w_gu, w_down)


def _gather_kernel(tok_ref, live_ref, a_ref, o_ref, buf, sem, *, tm):
    i = pl.program_id(0)
    n_live = live_ref[0]

    def issue(step, slot):
        base = step * tm

        def body(g, carry):
            for u in range(ROW_DMA_UNROLL):
                r = g * ROW_DMA_UNROLL + u
                pltpu.make_async_copy(a_ref.at[pl.ds(tok_ref[base + r], 1), :],
                                      buf.at[slot, pl.ds(r, 1), :], sem.at[slot]).start()
            return carry

        lax.fori_loop(0, tm // ROW_DMA_UNROLL, body, 0)

    @pl.when(i == 0)
    def _():
        issue(0, 0)

    @pl.when(i + 1 < n_live)
    def _():
        issue(i + 1, (i + 1) & 1)

    @pl.when(i < n_live)
    def _():
        slot = i & 1
        pltpu.make_async_copy(a_ref.at[pl.ds(0, tm), :], buf.at[slot], sem.at[slot]).wait()
        o_ref[...] = buf[slot].astype(o_ref.dtype)

    @pl.when(i >= n_live)
    def _():
        o_ref[...] = jnp.zeros_like(o_ref)


def gather_rows(a, slot_tok, rows_end, tm=512):
    T, D = a.shape
    R = slot_tok.shape[0]
    tm = min(tm, T)
    n_live = jnp.maximum((rows_end + tm - 1) // tm, 1).astype(jnp.int32).reshape(1)
    grid_spec = pltpu.PrefetchScalarGridSpec(
        num_scalar_prefetch=2,
        grid=(R // tm,),
        in_specs=[pl.BlockSpec(memory_space=pl.ANY)],
        out_specs=pl.BlockSpec((tm, D), lambda i, tok, live: (i, 0)),
        scratch_shapes=[pltpu.VMEM((2, tm, D), a.dtype), pltpu.SemaphoreType.DMA((2,))],
    )
    return pl.pallas_call(
        functools.partial(_gather_kernel, tm=tm),
        grid_spec=grid_spec,
        out_shape=jax.ShapeDtypeStruct((R, D), BF16),
        compiler_params=_cparams(1, VMEM_MB),
        name="gather_rows",
    )(slot_tok, n_live, a)


def _combine_kernel(dest_ref, y_ref, rw_ref, h_ref, g_ref, hn_ref, a_ref, buf, sem, *, tm, n_steps):
    i = pl.program_id(0)

    def issue(step, slot):
        base = step * tm

        def body(g, carry):
            for u in range(ROW_DMA_UNROLL // TOP_K):
                r = g * (ROW_DMA_UNROLL // TOP_K) + u
                for k in range(TOP_K):
                    src = dest_ref[(base + r) * TOP_K + k]
                    pltpu.make_async_copy(y_ref.at[pl.ds(src, 1), :], buf.at[slot, k, pl.ds(r, 1), :],
                                          sem.at[slot]).start()
            return carry

        lax.fori_loop(0, tm // (ROW_DMA_UNROLL // TOP_K), body, 0)

    @pl.when(i == 0)
    def _():
        issue(0, 0)

    @pl.when(i + 1 < n_steps)
    def _():
        issue(i + 1, (i + 1) & 1)

    slot = i & 1
    for k in range(TOP_K):
        pltpu.make_async_copy(y_ref.at[pl.ds(0, tm), :], buf.at[slot, k], sem.at[slot]).wait()
    rw = rw_ref[...]
    hn = h_ref[...] + rw[:, 2:3] * buf[slot, 0] + rw[:, 3:4] * buf[slot, 1]
    hn_ref[...] = hn
    a = hn * lax.rsqrt(jnp.mean(hn * hn, axis=-1, keepdims=True) + EPS) * g_ref[...]
    a_ref[...] = a.astype(a_ref.dtype)


def combine_residual_norm(y_sorted, dest, rw, h, gain, a_dtype, tm=256):
    T, D = h.shape
    tm = min(tm, T)
    n_steps = T // tm
    row = pl.BlockSpec((tm, D), lambda i, d: (i, 0))
    grid_spec = pltpu.PrefetchScalarGridSpec(
        num_scalar_prefetch=1,
        grid=(n_steps,),
        in_specs=[pl.BlockSpec(memory_space=pl.ANY),
                  pl.BlockSpec((tm, LANES), lambda i, d: (i, 0)),
                  row,
                  pl.BlockSpec((1, D), lambda i, d: (0, 0))],
        out_specs=[row, row],
        scratch_shapes=[pltpu.VMEM((2, TOP_K, tm, D), F32), pltpu.SemaphoreType.DMA((2,))],
    )
    return pl.pallas_call(
        functools.partial(_combine_kernel, tm=tm, n_steps=n_steps),
        grid_spec=grid_spec,
        out_shape=[jax.ShapeDtypeStruct((T, D), F32), jax.ShapeDtypeStruct((T, D), a_dtype)],
        compiler_params=_cparams(1, VMEM_MB),
        name="combine_residual_norm",
    )(dest, y_sorted, rw, h, gain.reshape(1, D).astype(F32))


def mixer_layer(h, a, l, lb, mem_n, cos_t, sin_t, p, batch, seq, n_mem, next_gain, a_dtype, w_router):
    D = h.shape[1]
    n_a = 4 * HG_HEADS * HG_D + 2 * MLA_RANK
    kr0 = n_a
    q_mem0 = kr0 + MLA_ROPE
    z1 = matmul_wt(a, p["w_in_t"], layer=l, row0=0, n_cols=n_a, tn=1024, tm=1024, out_dtype=F32,
                   name="in_proj_a")
    n_b = MEM_HEADS * MEM_HEAD_DIM + 3 * D
    z2 = matmul_wt(a, p["w_in_t"], layer=l, row0=q_mem0, n_cols=n_b, tn=1024, tm=1024, out_dtype=BF16,
                   name="in_proj_b")
    wq1, wq2, wkv = mla_weights(p["w_q_up"][l], p["w_kv_up"][l])
    wkr = rope_key_weight(p["w_in_t"], l, kr0)

    y_hg = hgrn_mix(z1, lb, p["hg_norm"][l], batch, seq)
    tq = min(ATTN_Q_ROWS, seq)
    tk = min(ATTN_BLOCK, tq)
    qp, kt, v = mla_prep(z1, a, cos_t, sin_t, p["mla_q_norm"][l], p["mla_kv_norm"][l], wq1, wq2, wkv,
                         wkr, tk)
    y_mla = causal_attention(qp, kt, v, batch, seq, tq, tk)
    mem_kv = matmul(mem_n, p["w_mem_kv"], layer=l, n_cols=2 * MEM_HEADS * MEM_HEAD_DIM, tn=512,
                    tm=1024, out_dtype=BF16, name="mem_kv")
    y_mem = memory_attention(z2, mem_kv, batch, seq, n_mem)
    merged = branch_merge(y_hg, y_mla, y_mem, z2, p["w_branch"], l)
    return proj_residual_norm(merged, p["w_o"], l, h, next_gain, a_dtype, w_router)


def group_bounds(starts, ends, index, n_groups, n_rows):
    E = starts.shape[0]
    before = jnp.zeros((index * E,), jnp.int32)
    after = jnp.full(((n_groups - index - 1) * E,), n_rows, jnp.int32)
    return (jnp.concatenate([before, starts.astype(jnp.int32), after]),
            jnp.concatenate([before, ends.astype(jnp.int32), after]))


def moe_routing(rw, n_tokens):
    A = n_tokens * TOP_K
    flat_e = rw[:, :TOP_K].astype(jnp.int32).reshape(A)
    onehot = (flat_e[:, None] == jnp.arange(N_EXPERTS, dtype=jnp.int32)[None, :]).astype(jnp.int32)
    csum = jnp.cumsum(onehot, axis=0)
    rank = jnp.take_along_axis(csum, flat_e[:, None], axis=1)[:, 0] - 1
    counts = csum[-1]
    padded = (counts + FFN_TM - 1) // FFN_TM * FFN_TM
    pad_end = jnp.cumsum(padded).astype(jnp.int32)
    pad_start = pad_end - padded
    dest = (pad_start[flat_e] + rank).astype(jnp.int32)
    n_rows = (A + N_EXPERTS * FFN_TM + FFN_RB - 1) // FFN_RB * FFN_RB + FFN_RB
    filler = jnp.arange(n_rows, dtype=jnp.int32) % n_tokens
    slot_tok = filler.at[dest].set(jnp.arange(A, dtype=jnp.int32) // TOP_K)
    return dest, slot_tok, pad_start, pad_end, n_rows


def kernel(x, mem, positions, hg_lb_logits, attn_norm, w_in, hg_norm, mla_q_norm, w_q_up, mla_kv_norm,
           w_kv_up, mem_norm, w_mem_kv, w_branch, w_o, ffn_norm, w_dense_gu, w_dense_down, w_router,
           w_expert_gu, w_expert_down, final_norm):
    batch, seq, D = x.shape
    n_mem = mem.shape[1]
    depth = w_in.shape[0]
    T = batch * seq
    p = dict(w_in_t=jnp.swapaxes(w_in, 1, 2), hg_norm=hg_norm, mla_q_norm=mla_q_norm, w_q_up=w_q_up, mla_kv_norm=mla_kv_norm,
             w_kv_up=w_kv_up, w_mem_kv=w_mem_kv, w_branch=w_branch.astype(BF16), w_o=w_o.astype(BF16))

    cos_t, sin_t = rope_tables(positions.astype(jnp.int32))
    lb_sm = jax.nn.softmax(hg_lb_logits.astype(F32), axis=0)
    lb_all = jnp.clip(jnp.cumsum(lb_sm, axis=0) - lb_sm[0:1], 0.0, 1.0)
    mem_n = rmsnorm_cast(mem.reshape(batch * n_mem, D), mem_norm, BF16)

    n_dense, n_moe = w_dense_gu.shape[0], w_expert_gu.shape[0]
    w_moe_gu = w_expert_gu.reshape((n_moe * N_EXPERTS,) + w_expert_gu.shape[2:])
    w_moe_down = w_expert_down.reshape((n_moe * N_EXPERTS,) + w_expert_down.shape[2:])

    h = x.reshape(T, D)
    a = rmsnorm_cast(h, attn_norm[0], BF16)
    for l in range(depth):
        moe = l % 2 == 1
        last = l + 1 == depth
        next_gain = final_norm if last else attn_norm[l + 1]
        next_dtype = F32 if last else BF16
        h, a_ffn, *routing = mixer_layer(h, a, l, lb_all[l], mem_n, cos_t, sin_t, p, batch, seq, n_mem,
                                         ffn_norm[l], F32 if moe else BF16,
                                         w_router[l // 2] if moe else None)
        if not moe:
            zero = jnp.zeros((1,), jnp.int32)
            starts, ends = group_bounds(zero, zero + T, l // 2, n_dense, T)
            y = grouped_ffn(a_ffn, w_dense_gu, w_dense_down, ffn_schedule(starts, ends, T), T,
                            jnp.int32(T))
            h, a = residual_norm(y, h, next_gain, next_dtype)
        else:
            (rw,) = routing
            dest, slot_tok, pad_start, pad_end, n_rows = moe_routing(rw, T)
            xs = gather_rows(a_ffn, slot_tok, pad_end[-1])
            starts, ends = group_bounds(pad_start, pad_end, l // 2, n_moe, n_rows)
            ys = grouped_ffn(xs, w_moe_gu, w_moe_down, ffn_schedule(starts, ends, n_rows),
                             n_rows - FFN_RB, pad_end[-1])
            h, a = combine_residual_norm(ys, dest, rw, h, next_gain, next_dtype)
    return a.reshape(batch, seq, D)
```

```python
import functools

import numpy as np
import jax
import jax.numpy as jnp
from jax import lax
from jax.experimental import pallas as pl
from jax.experimental.pallas import tpu as pltpu

F32 = jnp.float32
BF16 = jnp.bfloat16

EPS = 1e-6
NEG_BIG = -1e30
MIN_FORGET = 1e-20
ROPE_THETA = 10000.0

HG_HEADS = 8
HG_D = 128
HG_CHUNK = 64
HG_SUB = 16
MLA_HEADS = 8
MLA_RANK = 512
MLA_NOPE = 128
MLA_ROPE = 64
MLA_V = 128
MLA_QK_PAD = 256
MEM_HEADS = 4
MEM_HEAD_DIM = 256
N_EXPERTS = 8
TOP_K = 2

LOG2E = 1.4426950408889634

LANES = 128
SUBLANES = 8
ATTN_BLOCK = 512
ATTN_Q_ROWS = 512
FFN_TM = 256
FFN_RB = 2048
FFN_TF = 512
ROW_DMA_UNROLL = 8
VMEM_MB = 48
VMEM_FFN_MB = 60


def _cparams(n_axes, vmem_mb=None):
    kw = dict(dimension_semantics=("arbitrary",) * n_axes)
    if vmem_mb is not None:
        kw["vmem_limit_bytes"] = vmem_mb * 1024 * 1024
    return pltpu.CompilerParams(**kw)


def _sigmoid(x):
    return 1.0 / (1.0 + jnp.exp(-x))


def _dot(a, b):
    return jnp.dot(a, b, preferred_element_type=F32)


def _dot_nt(a, b):
    return lax.dot_general(a, b, (((1,), (1,)), ((), ())), preferred_element_type=F32)


def _rope_kernel(pos_ref, c_ref, cos_ref, sin_ref):
    ang = pos_ref[...].astype(F32) * c_ref[0:1, :]
    cos_ref[...] = jnp.cos(ang) * c_ref[1:2, :]
    sin_ref[...] = jnp.sin(ang) * c_ref[2:3, :]


def rope_tables(positions, tm=1024):
    T = positions.size
    half = MLA_ROPE // 2
    inv = ROPE_THETA ** (-(np.arange(half, dtype=np.float32) / half))
    consts = np.zeros((8, LANES), np.float32)
    consts[0, :half] = inv
    consts[0, half:2 * half] = inv
    consts[1, :2 * half] = 1.0
    consts[2, :half] = -1.0
    consts[2, half:2 * half] = 1.0
    tm = min(tm, T)
    out = jax.ShapeDtypeStruct((T, LANES), F32)
    return pl.pallas_call(
        _rope_kernel,
        grid=(T // tm,),
        in_specs=[pl.BlockSpec((tm, 1), lambda i: (i, 0)),
                  pl.BlockSpec((8, LANES), lambda i: (0, 0))],
        out_specs=[pl.BlockSpec((tm, LANES), lambda i: (i, 0)),
                   pl.BlockSpec((tm, LANES), lambda i: (i, 0))],
        out_shape=[out, out],
        compiler_params=_cparams(1),
        name="rope_tables",
    )(positions.reshape(T, 1), jnp.asarray(consts))


def _rmsnorm_kernel(x_ref, g_ref, o_ref):
    x = x_ref[...].astype(F32)
    y = x * lax.rsqrt(jnp.mean(x * x, axis=-1, keepdims=True) + EPS)
    o_ref[...] = (y * g_ref[...]).astype(o_ref.dtype)


def rmsnorm_cast(x, g, out_dtype, tm=512):
    M, D = x.shape
    tm = min(tm, M)
    return pl.pallas_call(
        _rmsnorm_kernel,
        grid=(M // tm,),
        in_specs=[pl.BlockSpec((tm, D), lambda i: (i, 0)),
                  pl.BlockSpec((1, D), lambda i: (0, 0))],
        out_specs=pl.BlockSpec((tm, D), lambda i: (i, 0)),
        out_shape=jax.ShapeDtypeStruct((M, D), out_dtype),
        compiler_params=_cparams(1),
        name="rmsnorm",
    )(x, g.reshape(1, D).astype(F32))


def _mm_kernel(x_ref, w_ref, o_ref, *scratch, cast):
    if cast:
        (wb,) = scratch

        @pl.when(pl.program_id(1) == 0)
        def _():
            wb[...] = w_ref[...].astype(BF16)

        w = wb[...]
    else:
        w = w_ref[...]
    o_ref[...] = _dot(x_ref[...], w).astype(o_ref.dtype)


def matmul(x, w, *, n_cols, tn, tm, out_dtype, layer=None, name="matmul"):
    M, K = x.shape
    tm = min(tm, M)
    cast = w.dtype != BF16
    if layer is None:
        w_spec = pl.BlockSpec((K, tn), lambda j, i: (0, j))
    else:
        w_spec = pl.BlockSpec((None, K, tn), lambda j, i: (layer, 0, j))
    scratch = [pltpu.VMEM((K, tn), BF16)] if cast else []
    return pl.pallas_call(
        functools.partial(_mm_kernel, cast=cast),
        grid=(n_cols // tn, M // tm),
        in_specs=[pl.BlockSpec((tm, K), lambda j, i: (i, 0)), w_spec],
        out_specs=pl.BlockSpec((tm, tn), lambda j, i: (i, j)),
        out_shape=jax.ShapeDtypeStruct((M, n_cols), out_dtype),
        scratch_shapes=scratch,
        compiler_params=_cparams(2, VMEM_MB),
        name=name,
    )(x, w)


def _mm_wt_kernel(x_ref, wt_ref, o_ref, wb):
    @pl.when(pl.program_id(1) == 0)
    def _():
        wb[...] = wt_ref[...].T.astype(BF16)

    o_ref[...] = _dot(x_ref[...], wb[...]).astype(o_ref.dtype)


def matmul_wt(x, wt, *, layer, row0, n_cols, tn, tm, out_dtype, name):
    M, K = x.shape
    tm = min(tm, M)
    assert n_cols % tn == 0
    if row0 % tn == 0:
        w_spec = pl.BlockSpec((None, tn, K), lambda j, i: (layer, row0 // tn + j, 0))
    else:
        w_spec = pl.BlockSpec((pl.Squeezed(), pl.Element(tn), pl.Element(K)),
                              lambda j, i: (layer, pl.multiple_of(row0 + j * tn, 8), 0))
    return pl.pallas_call(
        _mm_wt_kernel,
        grid=(n_cols // tn, M // tm),
        in_specs=[pl.BlockSpec((tm, K), lambda j, i: (i, 0)), w_spec],
        out_specs=pl.BlockSpec((tm, tn), lambda j, i: (i, j)),
        out_shape=jax.ShapeDtypeStruct((M, n_cols), out_dtype),
        scratch_shapes=[pltpu.VMEM((K, tn), BF16)],
        compiler_params=_cparams(2, VMEM_MB),
        name=name,
    )(x, wt)


def _rope_key_weight_kernel(wt_ref, o_ref):
    half = MLA_ROPE // 2
    w = wt_ref[...]
    zero = jnp.zeros((LANES - MLA_ROPE, w.shape[1]), F32)
    rows = jnp.concatenate([w, zero, w[half:], w[:half], zero], axis=0)
    o_ref[...] = rows.T.astype(BF16)


def rope_key_weight(wt, layer, row0):
    K = wt.shape[2]
    assert row0 % MLA_ROPE == 0
    return pl.pallas_call(
        _rope_key_weight_kernel,
        grid=(1,),
        in_specs=[pl.BlockSpec((None, MLA_ROPE, K), lambda i: (layer, row0 // MLA_ROPE, 0))],
        out_specs=pl.BlockSpec((K, 2 * LANES), lambda i: (0, 0)),
        out_shape=jax.ShapeDtypeStruct((K, 2 * LANES), BF16),
        compiler_params=_cparams(1),
        name="rope_key_weight",
    )(wt)


def _hgrn_kernel(q_ref, f_ref, i_ref, g_ref, lb_ref, gn_ref, tri_ref, ones_ref, dmask_ref, o_ref,
                 st_ref, bk_scr):
    L, SUB, D, G = HG_CHUNK, HG_SUB, HG_D, SUBLANES

    @pl.when(pl.program_id(1) == 0)
    def _():
        st_ref[...] = jnp.zeros_like(st_ref)

    lb = lb_ref[...]
    f = lb + (1.0 - lb) * _sigmoid(f_ref[...])
    lf = jnp.log(jnp.maximum(f, MIN_FORGET)) * LOG2E
    kk = 1.0 - f
    tri = tri_ref[...]
    p1 = lf.astype(BF16)
    r1 = lf - p1.astype(F32)
    p2 = r1.astype(BF16)
    p3 = (r1 - p2.astype(F32)).astype(BF16)
    b = _dot(tri, p1) + _dot(tri, p2) + _dot(tri, p3)

    q = q_ref[...]
    v = i_ref[...]
    g = g_ref[...]
    gn = gn_ref[...]
    ones = ones_ref[...]
    bk = b - jnp.log2(jnp.maximum(kk, 0.0))
    bk_scr[...] = bk
    b_last = b[L - 1:L, :]
    q_in = (q * jnp.exp2(b)).astype(BF16)
    k_dec = jnp.exp2(b_last - bk).astype(BF16)
    e_last = jnp.exp2(b_last)

    def first_matmuls(h):
        hs = slice(h * D, (h + 1) * D)
        bh, qh, bkh, vh = b[:, hs], q[:, hs], bk[:, hs], v[:, hs]
        vb = vh.astype(BF16)
        st = st_ref[h]
        inter = _dot_nt(q_in[:, hs], st.astype(BF16))
        st_ref[h] = e_last[:, hs] * st + lax.dot_general(
            vb, k_dec[:, hs], (((0,), (0,)), ((), ())), preferred_element_type=F32)
        ssums, off_parts = [], []
        for i in range(L // SUB):
            r0 = i * SUB
            bs, qs = bh[r0:r0 + SUB], qh[r0:r0 + SUB]
            terms = []
            for s in range(SUB):
                bk_s = bk_scr[r0 + s:r0 + s + 1, hs]
                for grp in range(s // G, SUB // G):
                    rows = slice(grp * G, (grp + 1) * G)
                    d = bs[rows] - bk_s
                    if grp == s // G and s % G:
                        d = d + dmask_ref[s % G]
                    terms.append(qs[rows] * jnp.exp2(d))
            ssums.append(_dot(jnp.concatenate(terms, axis=0).astype(BF16), ones))
            if i == 0:
                off_parts.append(jnp.zeros((SUB, L), F32))
            else:
                ref_row = bh[r0 - 1:r0]
                q_i = (qs * jnp.exp2(bs - ref_row)).astype(BF16)
                k_i = jnp.concatenate([jnp.exp2(ref_row - bkh[0:r0]), jnp.zeros((L - r0, D), F32)],
                                      axis=0).astype(BF16)
                off_parts.append(_dot_nt(q_i, k_i))
        return inter, ssums, off_parts, vh, vb

    def finish(h, inter, ssums, off_parts, vh, vb):
        hs = slice(h * D, (h + 1) * D)
        diag_parts = []
        for i in range(L // SUB):
            acc = [None] * (SUB // G)
            t = 0
            for s in range(SUB):
                v_s = i_ref[i * SUB + s:i * SUB + s + 1, hs]
                for grp in range(s // G, SUB // G):
                    term = ssums[i][t * G:(t + 1) * G] * v_s
                    acc[grp] = term if acc[grp] is None else acc[grp] + term
                    t += 1
            diag_parts.extend(acc)
        s_off = jnp.concatenate(off_parts, axis=0).astype(BF16)
        o = inter + jnp.concatenate(diag_parts, axis=0) + _dot(s_off, vb)
        on = o * lax.rsqrt(jnp.mean(o * o, axis=-1, keepdims=True) + EPS) * gn[:, hs]
        gh = g[:, hs]
        o_ref[:, hs] = (on * (gh * _sigmoid(gh))).astype(o_ref.dtype)

    pending = None
    for h in range(HG_HEADS):
        cur = first_matmuls(h)
        if pending is not None:
            finish(h - 1, *pending)
        pending = cur
    finish(HG_HEADS - 1, *pending)


def hgrn_mix(z1, lb, hg_norm, batch, seq):
    T = batch * seq
    W = HG_HEADS * HG_D
    L = HG_CHUNK
    nc = seq // L
    tri = jnp.asarray(np.tril(np.ones((L, L), np.float32)), BF16)
    ones = jnp.ones((HG_D, HG_D), BF16)
    G = SUBLANES
    dmask_np = np.where(np.arange(G)[None, :, None] >= np.arange(G)[:, None, None], 0.0, NEG_BIG)
    dmask = jnp.asarray(np.broadcast_to(dmask_np, (G, G, HG_D)).astype(np.float32))

    def col(c):
        return pl.BlockSpec((L, W), lambda b, j, c=c: (b * nc + j, c))

    const = lambda shape: pl.BlockSpec(shape, lambda b, j: (0,) * len(shape))
    return pl.pallas_call(
        _hgrn_kernel,
        grid=(batch, nc),
        in_specs=[col(0), col(1), col(2), col(3), const((1, W)), const((1, W)),
                  const((L, L)), const((HG_D, HG_D)), const((G, G, HG_D))],
        out_specs=pl.BlockSpec((L, W), lambda b, j: (b * nc + j, 0)),
        out_shape=jax.ShapeDtypeStruct((T, W), BF16),
        scratch_shapes=[pltpu.VMEM((HG_HEADS, HG_D, HG_D), F32), pltpu.VMEM((L, W), F32)],
        compiler_params=_cparams(2, VMEM_MB),
        name="hgrn2",
    )(z1, z1, z1, z1, lb.reshape(1, W).astype(F32),
      jnp.tile(hg_norm.astype(F32), HG_HEADS).reshape(1, W), tri, ones, dmask)


def _mla_prep_kernel(cq_ref, ckv_ref, a_ref, cos_ref, sin_ref, gq_ref, gkv_ref,
                     wq1_ref, wq2_ref, wkv_ref, wkr_ref, q_ref, kt_ref, v_ref):
    scale = (MLA_NOPE + MLA_ROPE) ** -0.5 * LOG2E
    cos = cos_ref[...]
    sin = sin_ref[...]

    def norm(x, g):
        return (x * lax.rsqrt(jnp.mean(x * x, axis=-1, keepdims=True) + EPS) * g).astype(BF16)

    cn = norm(cq_ref[...], gq_ref[...])
    q1 = _dot(cn, wq1_ref[...])
    q2 = _dot(cn, wq2_ref[...])
    cvn = norm(ckv_ref[...], gkv_ref[...])
    kv = _dot(cvn, wkv_ref[...])
    kr = _dot(a_ref[...], wkr_ref[...])
    kr_rot_t = (kr[:, 0:LANES] * cos + kr[:, LANES:2 * LANES] * sin).T.astype(BF16)
    P = MLA_QK_PAD
    for h in range(MLA_HEADS):
        q_ref[:, h * P:h * P + LANES] = (q1[:, h * P:h * P + LANES] * scale).astype(BF16)
        q_ref[:, h * P + LANES:(h + 1) * P] = (
            (q1[:, h * P + LANES:(h + 1) * P] * cos + q2[:, h * LANES:(h + 1) * LANES] * sin) * scale
        ).astype(BF16)
        kt_ref[h * P:h * P + LANES, :] = kv[:, h * LANES:(h + 1) * LANES].T.astype(BF16)
        kt_ref[h * P + LANES:(h + 1) * P, :] = kr_rot_t
    nv = MLA_HEADS * MLA_V
    v_ref[...] = kv[:, nv:2 * nv].astype(BF16)


def mla_prep(z1, a, cos_t, sin_t, gq, gkv, wq1, wq2, wkv, wkr, tm):
    T = z1.shape[0]
    R = MLA_RANK
    cq_blk = (4 * HG_HEADS * HG_D) // R
    row = lambda w: pl.BlockSpec((tm, w), lambda i: (i, 0))
    const = lambda a: pl.BlockSpec(a.shape, lambda i: (0, 0))
    qk_w = MLA_HEADS * MLA_QK_PAD
    return pl.pallas_call(
        _mla_prep_kernel,
        grid=(T // tm,),
        in_specs=[pl.BlockSpec((tm, R), lambda i: (i, cq_blk)),
                  pl.BlockSpec((tm, R), lambda i: (i, cq_blk + 1)),
                  row(a.shape[1]), row(LANES), row(LANES),
                  pl.BlockSpec((1, R), lambda i: (0, 0)), pl.BlockSpec((1, R), lambda i: (0, 0)),
                  const(wq1), const(wq2), const(wkv), const(wkr)],
        out_specs=[row(qk_w), pl.BlockSpec((None, qk_w, tm), lambda i: (i, 0, 0)), row(MLA_HEADS * MLA_V)],
        out_shape=[jax.ShapeDtypeStruct((T, qk_w), BF16), jax.ShapeDtypeStruct((T // tm, qk_w, tm), BF16),
                   jax.ShapeDtypeStruct((T, MLA_HEADS * MLA_V), BF16)],
        compiler_params=_cparams(1, VMEM_MB),
        name="mla_prep",
    )(z1, z1, a, cos_t, sin_t, gq.reshape(1, R).astype(F32), gkv.reshape(1, R).astype(F32),
      wq1, wq2, wkv, wkr)


def mla_weights(w_q_up, w_kv_up):
    R, half = MLA_RANK, MLA_ROPE // 2
    wq = w_q_up.reshape(R, MLA_HEADS, MLA_NOPE + MLA_ROPE)
    nope, rope = wq[..., :MLA_NOPE], wq[..., MLA_NOPE:]
    rope_sw = jnp.concatenate([rope[..., half:], rope[..., :half]], axis=-1)
    z64 = jnp.zeros((R, MLA_HEADS, MLA_QK_PAD - MLA_NOPE - MLA_ROPE), w_q_up.dtype)
    wq1 = jnp.concatenate([nope, rope, z64], axis=-1).reshape(R, MLA_HEADS * MLA_QK_PAD)
    wq2 = jnp.concatenate([rope_sw, z64], axis=-1).reshape(R, MLA_HEADS * LANES)
    wkv = w_kv_up.reshape(R, MLA_HEADS, MLA_NOPE + MLA_V)
    wkv = jnp.concatenate([wkv[..., :MLA_NOPE].reshape(R, -1), wkv[..., MLA_NOPE:].reshape(R, -1)], axis=-1)
    return wq1.astype(BF16), wq2.astype(BF16), wkv.astype(BF16)


def _causal_attn_kernel(q_ref, kt_ref, v_ref, o_ref, m_ref, l_ref, acc_ref, *, tq, tk):
    i = pl.program_id(1)
    H, P, DV = MLA_HEADS, MLA_QK_PAD, MLA_V
    n_c = tk // LANES
    n_diag = tq // tk
    m_ref[...] = jnp.full_like(m_ref, NEG_BIG)
    l_ref[...] = jnp.zeros_like(l_ref)
    acc_ref[...] = jnp.zeros_like(acc_ref)

    def scores(j, h):
        return _dot(q_ref[:, h * P:(h + 1) * P], kt_ref[j, h * P:(h + 1) * P, :])

    def softmax(h, s, col_off):
        if col_off is not None:
            rows = lax.broadcasted_iota(jnp.int32, (tq, tk), 0)
            cols = lax.broadcasted_iota(jnp.int32, (tq, tk), 1) + col_off
            s = jnp.where(cols <= rows, s, NEG_BIG)
        chunks = [s[:, c * LANES:(c + 1) * LANES] for c in range(n_c)]
        m_old = m_ref[h]
        m_new = jnp.maximum(m_old, jnp.max(functools.reduce(jnp.maximum, chunks), axis=-1, keepdims=True))
        ps = [jnp.exp2(c - m_new) for c in chunks]
        alpha = jnp.exp2(m_old - m_new)
        l_ref[h] = alpha * l_ref[h] + jnp.sum(functools.reduce(jnp.add, ps), axis=-1, keepdims=True)
        m_ref[h] = m_new
        return jnp.concatenate([c.astype(BF16) for c in ps], axis=1), alpha

    def values(j, h, p, alpha):
        start = pl.multiple_of(j * tk, tk)
        acc_ref[h] = alpha * acc_ref[h] + _dot(p, v_ref[pl.ds(start, tk), h * DV:(h + 1) * DV])

    def block(j, col_off):
        s, pa = {}, {}
        for step in range(H + 2):
            if step < H:
                s[step] = scores(j, step)
            if 1 <= step <= H:
                pa[step - 1] = softmax(step - 1, s.pop(step - 1), col_off)
            if step >= 2:
                values(j, step - 2, *pa.pop(step - 2))

    def body(j, carry):
        block(j, None)
        return carry

    lax.fori_loop(0, i * n_diag, body, 0)
    for d in range(n_diag):
        block(i * n_diag + d, d * tk)
    for h in range(H):
        o_ref[:, h * DV:(h + 1) * DV] = (acc_ref[h] / l_ref[h]).astype(o_ref.dtype)


def causal_attention(qp, kt, v, batch, seq, tq, tk):
    T = batch * seq
    nq, nk = seq // tq, seq // tk
    H, P, DV = MLA_HEADS, MLA_QK_PAD, MLA_V
    return pl.pallas_call(
        functools.partial(_causal_attn_kernel, tq=tq, tk=tk),
        grid=(batch, nq),
        in_specs=[pl.BlockSpec((tq, H * P), lambda b, i: (b * nq + i, 0)),
                  pl.BlockSpec((nk, H * P, tk), lambda b, i: (b, 0, 0)),
                  pl.BlockSpec((seq, H * DV), lambda b, i: (b, 0))],
        out_specs=pl.BlockSpec((tq, H * DV), lambda b, i: (b * nq + i, 0)),
        out_shape=jax.ShapeDtypeStruct((T, H * DV), BF16),
        scratch_shapes=[pltpu.VMEM((H, tq, LANES), F32), pltpu.VMEM((H, tq, LANES), F32),
                        pltpu.VMEM((H, tq, DV), F32)],
        compiler_params=_cparams(2, VMEM_MB),
        name="mla_attention",
    )(qp, kt, v)


def _mem_attn_kernel(q_ref, k_ref, v_ref, o_ref):
    scale = MEM_HEAD_DIM ** -0.5
    Dh = MEM_HEAD_DIM
    for h in range(MEM_HEADS):
        hs = slice(h * Dh, (h + 1) * Dh)
        s = _dot_nt(q_ref[:, hs], k_ref[:, hs]) * scale
        m = jnp.max(s, axis=-1, keepdims=True)
        p = jnp.exp(s - m)
        l = jnp.sum(p, axis=-1, keepdims=True)
        o = _dot(p.astype(BF16), v_ref[:, hs])
        o_ref[:, hs] = (o / l).astype(o_ref.dtype)


def memory_attention(z2, mem_kv, batch, seq, n_mem, tq=512):
    T = batch * seq
    tq = min(tq, seq)
    nq = seq // tq
    W = MEM_HEADS * MEM_HEAD_DIM
    return pl.pallas_call(
        _mem_attn_kernel,
        grid=(batch, nq),
        in_specs=[pl.BlockSpec((tq, W), lambda b, i: (b * nq + i, 0)),
                  pl.BlockSpec((n_mem, W), lambda b, i: (b, 0)),
                  pl.BlockSpec((n_mem, W), lambda b, i: (b, 1))],
        out_specs=pl.BlockSpec((tq, W), lambda b, i: (b * nq + i, 0)),
        out_shape=jax.ShapeDtypeStruct((T, W), BF16),
        compiler_params=_cparams(2, VMEM_MB),
        name="mem_attention",
    )(z2, mem_kv, mem_kv)


def _merge_kernel(y0_ref, y1_ref, y2_ref, g0_ref, g1_ref, g2_ref, w_ref, o_ref):
    acc = None
    for n, (y_ref, g_ref) in enumerate(((y0_ref, g0_ref), (y1_ref, g1_ref), (y2_ref, g2_ref))):
        term = _sigmoid(g_ref[...].astype(F32)) * _dot(y_ref[...], w_ref[n])
        acc = term if acc is None else acc + term
    o_ref[...] = acc.astype(o_ref.dtype)


def branch_merge(y_hg, y_mla, y_mem, z2, w_branch, layer, tm=1024, tn=512):
    T, W = y_hg.shape
    D = w_branch.shape[-1]
    tm = min(tm, T)
    g_off = W // tn

    def gate(n):
        return pl.BlockSpec((tm, tn), lambda i, j, n=n: (i, g_off + n * (D // tn) + j))

    y_spec = pl.BlockSpec((tm, W), lambda i, j: (i, 0))
    return pl.pallas_call(
        _merge_kernel,
        grid=(T // tm, D // tn),
        in_specs=[y_spec, y_spec, y_spec, gate(0), gate(1), gate(2),
                  pl.BlockSpec((None, 3, W, tn), lambda i, j: (layer, 0, 0, j))],
        out_specs=pl.BlockSpec((tm, tn), lambda i, j: (i, j)),
        out_shape=jax.ShapeDtypeStruct((T, D), BF16),
        compiler_params=_cparams(2, VMEM_MB),
        name="branch_merge",
    )(y_hg, y_mla, y_mem, z2, z2, z2, w_branch)


def _top2_route(logits):
    lane = lax.broadcasted_iota(jnp.int32, logits.shape, 1)
    lg = jnp.where(lane < N_EXPERTS, logits, NEG_BIG)
    m1 = jnp.max(lg, axis=-1, keepdims=True)
    i1 = jnp.min(jnp.where(lg == m1, lane, LANES), axis=-1, keepdims=True)
    lg2 = jnp.where(lane == i1, NEG_BIG, lg)
    m2 = jnp.max(lg2, axis=-1, keepdims=True)
    i2 = jnp.min(jnp.where(lg2 == m2, lane, LANES), axis=-1, keepdims=True)
    e2 = jnp.exp(m2 - m1)
    w1 = 1.0 / (1.0 + e2)
    w2 = e2 / (1.0 + e2)
    return jnp.where(lane == 0, i1.astype(F32),
                     jnp.where(lane == 1, i2.astype(F32),
                               jnp.where(lane == 2, w1, jnp.where(lane == 3, w2, 0.0))))


def _proj_res_norm_kernel(x_ref, w_ref, h_ref, g_ref, *rest, route):
    if route:
        wr_ref, hn_ref, a_ref, rw_ref = rest
    else:
        hn_ref, a_ref = rest
    hn = h_ref[...] + _dot(x_ref[...], w_ref[...])
    hn_ref[...] = hn
    a = hn * lax.rsqrt(jnp.mean(hn * hn, axis=-1, keepdims=True) + EPS) * g_ref[...]
    a_ref[...] = a.astype(a_ref.dtype)
    if route:
        rw_ref[...] = _top2_route(_dot(a.astype(BF16), wr_ref[...]))


def proj_residual_norm(x, w, layer, h, gain, a_dtype, w_router=None, tm=512):
    T, K = x.shape
    D = w.shape[2]
    tm = min(tm, T)
    route = w_router is not None
    row = lambda width: pl.BlockSpec((tm, width), lambda i: (i, 0))
    const = lambda r, c: pl.BlockSpec((r, c), lambda i: (0, 0))
    in_specs = [row(K), pl.BlockSpec((None, K, D), lambda i: (layer, 0, 0)), row(D), const(1, D)]
    out_specs = [row(D), row(D)]
    out_shape = [jax.ShapeDtypeStruct((T, D), F32), jax.ShapeDtypeStruct((T, D), a_dtype)]
    args = [x, w, h, gain.reshape(1, D).astype(F32)]
    if route:
        in_specs.append(const(D, LANES))
        out_specs.append(row(LANES))
        out_shape.append(jax.ShapeDtypeStruct((T, LANES), F32))
        args.append(jnp.zeros((D, LANES), BF16).at[:, :N_EXPERTS].set(w_router.astype(BF16)))
    return pl.pallas_call(
        functools.partial(_proj_res_norm_kernel, route=route),
        grid=(T // tm,),
        in_specs=in_specs,
        out_specs=out_specs,
        out_shape=out_shape,
        compiler_params=_cparams(1, VMEM_MB),
        name="proj_residual_norm",
    )(*args)


def _ffn_kernel(row0_ref, ex_ref, work_ref, nblk_ref, tail_ref, x_ref, wg_ref, wu_ref, wd_ref, *rest,
                n_f, residual):
    if residual:
        h_ref, g_ref, o_ref, a_ref, acc, wgb, wub, wdb, sem, abuf, sem_a = rest
    else:
        o_ref, acc, wgb, wub, wdb, sem = rest
    v = pl.program_id(0)
    j = pl.program_id(1)
    rlo = 0
    rhi = nblk_ref[v]

    last_j = j == n_f - 1
    row0 = pl.multiple_of(row0_ref[v], FFN_TM)

    def acc_rows(t):
        return acc.at[pl.ds(pl.multiple_of(t * FFN_TM, FFN_TM), FFN_TM), :]

    def hbm_rows(ref, t):
        return ref.at[pl.ds(pl.multiple_of(row0 + t * FFN_TM, FFN_TM), FFN_TM), :]

    def out_copy(t):
        return pltpu.make_async_copy(acc_rows(t), hbm_rows(o_ref, t), sem)

    def run_copies(make, n, start=True):
        def go(t, carry):
            make(t).start()
            return carry

        def wait(t, carry):
            make(t).wait()
            return carry

        if start:
            lax.fori_loop(0, n, go, 0)
        lax.fori_loop(0, n, wait, 0)

    @pl.when(jnp.logical_and(rhi > 0, j == 0))
    def _():
        if residual:
            run_copies(lambda t: pltpu.make_async_copy(hbm_rows(h_ref, t), acc_rows(t), sem), rhi)
        else:
            acc[...] = jnp.zeros_like(acc)

    def row_blocks(r, n):
        rows = [pl.ds(pl.multiple_of((r + t) * FFN_TM, FFN_TM), FFN_TM) for t in range(n)]
        hid = []
        for t in range(n):
            xb = x_ref[rows[t], :].astype(BF16)
            gate = _dot(xb, wgb[...])
            up = _dot(xb, wub[...])
            hid.append((gate * _sigmoid(gate) * up).astype(BF16))
        for t in range(n):
            acc[rows[t], :] += _dot(hid[t], wdb[...])

        @pl.when(last_j)
        def _():
            for t in range(n):
                out_copy(r + t).start()
                if residual:
                    hn = acc[rows[t], :]
                    a = hn * lax.rsqrt(jnp.mean(hn * hn, axis=-1, keepdims=True) + EPS) * g_ref[...]
                    abuf[...] = a.astype(abuf.dtype)
                    a_copy = pltpu.make_async_copy(abuf, hbm_rows(a_ref, r + t), sem_a)
                    a_copy.start()
                    a_copy.wait()

    @pl.when(rhi > rlo)
    def _():
        wgb[...] = wg_ref[...].astype(BF16)
        wub[...] = wu_ref[...].astype(BF16)
        wdb[...] = wd_ref[...].astype(BF16)
        row_blocks(rlo, 1)
        rest = rhi - rlo - 1

        def pair(t, carry):
            row_blocks(rlo + 1 + 2 * t, 2)
            return carry

        lax.fori_loop(0, lax.shift_right_logical(rest, 1), pair, 0)

        @pl.when((rest & 1) == 1)
        def _():
            row_blocks(rhi - 1, 1)

    def zero_copy(t):
        dst = pl.multiple_of((tail_ref[0] + t) * FFN_TM, FFN_TM)
        return pltpu.make_async_copy(acc.at[pl.ds(0, FFN_TM), :], o_ref.at[pl.ds(dst, FFN_TM), :], sem)

    @pl.when(jnp.logical_and(rhi > 0, last_j))
    def _():
        run_copies(out_copy, rhi, start=False)

    if not residual:
        @pl.when(jnp.logical_and(v == pl.num_programs(0) - 1, last_j))
        def _():
            acc[0:FFN_TM, :] = jnp.zeros((FFN_TM, acc.shape[1]), F32)
            run_copies(zero_copy, tail_ref[1])


def ffn_schedule(pad_start, pad_end, n_rows):
    E = pad_start.shape[0]
    kmax = n_rows // FFN_RB
    nv = min(kmax + E, kmax * E)
    nb = (pad_end - pad_start) // FFN_TM
    n_win = (nb + FFN_RB // FFN_TM - 1) // (FFN_RB // FFN_TM)
    per_win = (nb + jnp.maximum(n_win, 1) - 1) // jnp.maximum(n_win, 1)
    k = jnp.arange(kmax, dtype=jnp.int32)[None, :]
    start = pad_start[:, None] + k * per_win[:, None] * FFN_TM
    flat = (k < n_win[:, None]).reshape(-1)
    nblk = jnp.clip(nb[:, None] - k * per_win[:, None], 0, per_win[:, None]).reshape(-1)
    order = jnp.argsort(jnp.logical_not(flat), stable=True)[:nv].astype(jnp.int32)
    n_valid = jnp.sum(flat.astype(jnp.int32))
    valid = jnp.arange(nv, dtype=jnp.int32) < n_valid
    order = jnp.where(valid, order, order[jnp.maximum(n_valid - 1, 0)])
    row0 = start.reshape(-1)[order].astype(jnp.int32)
    ex = (order // kmax).astype(jnp.int32)
    return row0, ex, valid.astype(jnp.int32), jnp.where(valid, nblk[order], 0).astype(jnp.int32)


def grouped_ffn(x, w_gu, w_down, sched, n_out, rows_end, residual=None):
    R, D = x.shape
    E, _, F2 = w_gu.shape
    F = F2 // 2
    n_f = F // FFN_TF
    row0, ex, work, nblk = sched
    nv = row0.shape[0]
    tail = jnp.stack([rows_end // FFN_TM, (n_out - rows_end) // FFN_TM]).astype(jnp.int32)

    def jf(j, work_ref, v):
        return jnp.where(work_ref[v] == 1, j, n_f - 1)

    anywhere = pl.BlockSpec(memory_space=pl.ANY)
    in_specs = [
        pl.BlockSpec((pl.Element(FFN_RB), pl.Element(D)),
                     lambda v, j, row0, ex, wk, nb, tl: (pl.multiple_of(row0[v], FFN_TM), 0),
                     pipeline_mode=pl.Buffered(1)),
        pl.BlockSpec((None, D, FFN_TF), lambda v, j, row0, ex, wk, nb, tl: (ex[v], 0, jf(j, wk, v))),
        pl.BlockSpec((None, D, FFN_TF), lambda v, j, row0, ex, wk, nb, tl: (ex[v], 0, n_f + jf(j, wk, v))),
        pl.BlockSpec((None, FFN_TF, D), lambda v, j, row0, ex, wk, nb, tl: (ex[v], jf(j, wk, v), 0)),
    ]
    out_specs = [anywhere]
    out_shape = [jax.ShapeDtypeStruct((n_out, D), F32)]
    scratch = [pltpu.VMEM((FFN_RB, D), F32), pltpu.VMEM((D, FFN_TF), BF16), pltpu.VMEM((D, FFN_TF), BF16),
               pltpu.VMEM((FFN_TF, D), BF16), pltpu.SemaphoreType.DMA(())]
    args = [row0, ex, work, nblk, tail, x, w_gu, w_gu, w_down]
    if residual is not None:
        h, gain, a_dtype = residual
        assert isinstance(rows_end, int) and rows_end == n_out and h.shape == (n_out, D)
        in_specs += [anywhere, pl.BlockSpec((1, D), lambda v, j, row0, ex, wk, nb, tl: (0, 0))]
        out_specs.append(anywhere)
        out_shape.append(jax.ShapeDtypeStruct((n_out, D), a_dtype))
        scratch += [pltpu.VMEM((FFN_TM, D), a_dtype), pltpu.SemaphoreType.DMA(())]
        args += [h, gain.reshape(1, D).astype(F32)]
    grid_spec = pltpu.PrefetchScalarGridSpec(
        num_scalar_prefetch=5,
        grid=(nv, n_f),
        in_specs=in_specs,
        out_specs=out_specs,
        scratch_shapes=scratch,
    )
    out = pl.pallas_call(
        functools.partial(_ffn_kernel, n_f=n_f, residual=residual is not None),
        grid_spec=grid_spec,
        out_shape=out_shape,
        compiler_params=_cparams(2, VMEM_FFN_MB),
        name="grouped_ffn",
    )(*args)
    return out if residual is not None else out[0]


def _gather_kernel(tok_ref, live_ref, a_ref, o_ref, buf, sem, *, tm):
    i = pl.program_id(0)
    n_live = live_ref[0]

    def issue(step, slot):
        base = step * tm

        def body(g, carry):
            for u in range(ROW_DMA_UNROLL):
                r = g * ROW_DMA_UNROLL + u
                pltpu.make_async_copy(a_ref.at[pl.ds(tok_ref[base + r], 1), :],
                                      buf.at[slot, pl.ds(r, 1), :], sem.at[slot]).start()
            return carry

        lax.fori_loop(0, tm // ROW_DMA_UNROLL, body, 0)

    @pl.when(i == 0)
    def _():
        issue(0, 0)

    @pl.when(i + 1 < n_live)
    def _():
        issue(i + 1, (i + 1) & 1)

    @pl.when(i < n_live)
    def _():
        slot = i & 1
        pltpu.make_async_copy(a_ref.at[pl.ds(0, tm), :], buf.at[slot], sem.at[slot]).wait()
        o_ref[...] = buf[slot].astype(o_ref.dtype)

    @pl.when(i >= n_live)
    def _():
        o_ref[...] = jnp.zeros_like(o_ref)


def gather_rows(a, slot_tok, rows_end, tm=512):
    T, D = a.shape
    R = slot_tok.shape[0]
    tm = min(tm, T)
    n_live = jnp.maximum((rows_end + tm - 1) // tm, 1).astype(jnp.int32).reshape(1)
    grid_spec = pltpu.PrefetchScalarGridSpec(
        num_scalar_prefetch=2,
        grid=(R // tm,),
        in_specs=[pl.BlockSpec(memory_space=pl.ANY)],
        out_specs=pl.BlockSpec((tm, D), lambda i, tok, live: (i, 0)),
        scratch_shapes=[pltpu.VMEM((2, tm, D), a.dtype), pltpu.SemaphoreType.DMA((2,))],
    )
    return pl.pallas_call(
        functools.partial(_gather_kernel, tm=tm),
        grid_spec=grid_spec,
        out_shape=jax.ShapeDtypeStruct((R, D), BF16),
        compiler_params=_cparams(1, VMEM_MB),
        name="gather_rows",
    )(slot_tok, n_live, a)


def _combine_kernel(dest_ref, y_ref, rw_ref, h_ref, g_ref, hn_ref, a_ref, buf, sem, *, tm, n_steps):
    i = pl.program_id(0)

    def issue(step, slot):
        base = step * tm

        def body(g, carry):
            for u in range(ROW_DMA_UNROLL // TOP_K):
                r = g * (ROW_DMA_UNROLL // TOP_K) + u
                for k in range(TOP_K):
                    src = dest_ref[(base + r) * TOP_K + k]
                    pltpu.make_async_copy(y_ref.at[pl.ds(src, 1), :], buf.at[slot, k, pl.ds(r, 1), :],
                                          sem.at[slot]).start()
            return carry

        lax.fori_loop(0, tm // (ROW_DMA_UNROLL // TOP_K), body, 0)

    @pl.when(i == 0)
    def _():
        issue(0, 0)

    @pl.when(i + 1 < n_steps)
    def _():
        issue(i + 1, (i + 1) & 1)

    slot = i & 1
    for k in range(TOP_K):
        pltpu.make_async_copy(y_ref.at[pl.ds(0, tm), :], buf.at[slot, k], sem.at[slot]).wait()
    rw = rw_ref[...]
    hn = h_ref[...] + rw[:, 2:3] * buf[slot, 0] + rw[:, 3:4] * buf[slot, 1]
    hn_ref[...] = hn
    a = hn * lax.rsqrt(jnp.mean(hn * hn, axis=-1, keepdims=True) + EPS) * g_ref[...]
    a_ref[...] = a.astype(a_ref.dtype)


def combine_residual_norm(y_sorted, dest, rw, h, gain, a_dtype, tm=256):
    T, D = h.shape
    tm = min(tm, T)
    n_steps = T // tm
    row = pl.BlockSpec((tm, D), lambda i, d: (i, 0))
    grid_spec = pltpu.PrefetchScalarGridSpec(
        num_scalar_prefetch=1,
        grid=(n_steps,),
        in_specs=[pl.BlockSpec(memory_space=pl.ANY),
                  pl.BlockSpec((tm, LANES), lambda i, d: (i, 0)),
                  row,
                  pl.BlockSpec((1, D), lambda i, d: (0, 0))],
        out_specs=[row, row],
        scratch_shapes=[pltpu.VMEM((2, TOP_K, tm, D), F32), pltpu.SemaphoreType.DMA((2,))],
    )
    return pl.pallas_call(
        functools.partial(_combine_kernel, tm=tm, n_steps=n_steps),
        grid_spec=grid_spec,
        out_shape=[jax.ShapeDtypeStruct((T, D), F32), jax.ShapeDtypeStruct((T, D), a_dtype)],
        compiler_params=_cparams(1, VMEM_MB),
        name="combine_residual_norm",
    )(dest, y_sorted, rw, h, gain.reshape(1, D).astype(F32))


def mixer_layer(h, a, l, lb, mem_n, cos_t, sin_t, p, batch, seq, n_mem, next_gain, a_dtype, w_router):
    D = h.shape[1]
    n_a = 4 * HG_HEADS * HG_D + 2 * MLA_RANK
    kr0 = n_a
    q_mem0 = kr0 + MLA_ROPE
    z1 = matmul_wt(a, p["w_in_t"], layer=l, row0=0, n_cols=n_a, tn=1024, tm=1024, out_dtype=F32,
                   name="in_proj_a")
    n_b = MEM_HEADS * MEM_HEAD_DIM + 3 * D
    z2 = matmul_wt(a, p["w_in_t"], layer=l, row0=q_mem0, n_cols=n_b, tn=1024, tm=1024, out_dtype=BF16,
                   name="in_proj_b")
    wq1, wq2, wkv = mla_weights(p["w_q_up"][l], p["w_kv_up"][l])
    wkr = rope_key_weight(p["w_in_t"], l, kr0)

    y_hg = hgrn_mix(z1, lb, p["hg_norm"][l], batch, seq)
    tq = min(ATTN_Q_ROWS, seq)
    tk = min(ATTN_BLOCK, tq)
    qp, kt, v = mla_prep(z1, a, cos_t, sin_t, p["mla_q_norm"][l], p["mla_kv_norm"][l], wq1, wq2, wkv,
                         wkr, tk)
    y_mla = causal_attention(qp, kt, v, batch, seq, tq, tk)
    mem_kv = matmul(mem_n, p["w_mem_kv"], layer=l, n_cols=2 * MEM_HEADS * MEM_HEAD_DIM, tn=512,
                    tm=1024, out_dtype=BF16, name="mem_kv")
    y_mem = memory_attention(z2, mem_kv, batch, seq, n_mem)
    merged = branch_merge(y_hg, y_mla, y_mem, z2, p["w_branch"], l)
    return proj_residual_norm(merged, p["w_o"], l, h, next_gain, a_dtype, w_router)


def group_bounds(starts, ends, index, n_groups, n_rows):
    E = starts.shape[0]
    before = jnp.zeros((index * E,), jnp.int32)
    after = jnp.full(((n_groups - index - 1) * E,), n_rows, jnp.int32)
    return (jnp.concatenate([before, starts.astype(jnp.int32), after]),
            jnp.concatenate([before, ends.astype(jnp.int32), after]))


def moe_routing(rw, n_tokens):
    A = n_tokens * TOP_K
    flat_e = rw[:, :TOP_K].astype(jnp.int32).reshape(A)
    onehot = (flat_e[:, None] == jnp.arange(N_EXPERTS, dtype=jnp.int32)[None, :]).astype(jnp.int32)
    csum = jnp.cumsum(onehot, axis=0)
    rank = jnp.take_along_axis(csum, flat_e[:, None], axis=1)[:, 0] - 1
    counts = csum[-1]
    padded = (counts + FFN_TM - 1) // FFN_TM * FFN_TM
    pad_end = jnp.cumsum(padded).astype(jnp.int32)
    pad_start = pad_end - padded
    dest = (pad_start[flat_e] + rank).astype(jnp.int32)
    n_rows = (A + N_EXPERTS * FFN_TM + FFN_RB - 1) // FFN_RB * FFN_RB + FFN_RB
    filler = jnp.arange(n_rows, dtype=jnp.int32) % n_tokens
    slot_tok = filler.at[dest].set(jnp.arange(A, dtype=jnp.int32) // TOP_K)
    return dest, slot_tok, pad_start, pad_end, n_rows


def kernel(x, mem, positions, hg_lb_logits, attn_norm, w_in, hg_norm, mla_q_norm, w_q_up, mla_kv_norm,
           w_kv_up, mem_norm, w_mem_kv, w_branch, w_o, ffn_norm, w_dense_gu, w_dense_down, w_router,
           w_expert_gu, w_expert_down, final_norm):
    batch, seq, D = x.shape
    n_mem = mem.shape[1]
    depth = w_in.shape[0]
    T = batch * seq
    p = dict(w_in_t=jnp.swapaxes(w_in, 1, 2), hg_norm=hg_norm, mla_q_norm=mla_q_norm, w_q_up=w_q_up, mla_kv_norm=mla_kv_norm,
             w_kv_up=w_kv_up, w_mem_kv=w_mem_kv, w_branch=w_branch.astype(BF16), w_o=w_o.astype(BF16))

    cos_t, sin_t = rope_tables(positions.astype(jnp.int32))
    lb_sm = jax.nn.softmax(hg_lb_logits.astype(F32), axis=0)
    lb_all = jnp.clip(jnp.cumsum(lb_sm, axis=0) - lb_sm[0:1], 0.0, 1.0)
    mem_n = rmsnorm_cast(mem.reshape(batch * n_mem, D), mem_norm, BF16)

    n_dense, n_moe = w_dense_gu.shape[0], w_expert_gu.shape[0]
    w_moe_gu = w_expert_gu.reshape((n_moe * N_EXPERTS,) + w_expert_gu.shape[2:])
    w_moe_down = w_expert_down.reshape((n_moe * N_EXPERTS,) + w_expert_down.shape[2:])

    h = x.reshape(T, D)
    a = rmsnorm_cast(h, attn_norm[0], BF16)
    for l in range(depth):
        moe = l % 2 == 1
        last = l + 1 == depth
        next_gain = final_norm if last else attn_norm[l + 1]
        next_dtype = F32 if last else BF16
        h, a_ffn, *routing = mixer_layer(h, a, l, lb_all[l], mem_n, cos_t, sin_t, p, batch, seq, n_mem,
                                         ffn_norm[l], F32 if moe else BF16,
                                         w_router[l // 2] if moe else None)
        if not moe:
            zero = jnp.zeros((1,), jnp.int32)
            starts, ends = group_bounds(zero, zero + T, l // 2, n_dense, T)
            h, a = grouped_ffn(a_ffn, w_dense_gu, w_dense_down, ffn_schedule(starts, ends, T), T, T,
                               residual=(h, next_gain, next_dtype))
        else:
            (rw,) = routing
            dest, slot_tok, pad_start, pad_end, n_rows = moe_routing(rw, T)
            xs = gather_rows(a_ffn, slot_tok, pad_end[-1])
            starts, ends = group_bounds(pad_start, pad_end, l // 2, n_moe, n_rows)
            ys = grouped_ffn(xs, w_moe_gu, w_moe_down, ffn_schedule(starts, ends, n_rows),
                             n_rows - FFN_RB, pad_end[-1])
            h, a = combine_residual_norm(ys, dest, rw, h, next_gain, next_dtype)
    return a.reshape(batch, seq, D)
```

```python
import functools

import numpy as np
import jax
import jax.numpy as jnp
from jax import lax
from jax.experimental import pallas as pl
from jax.experimental.pallas import tpu as pltpu

F32 = jnp.float32
BF16 = jnp.bfloat16

EPS = 1e-6
NEG_BIG = -1e30
MIN_FORGET = 1e-20
ROPE_THETA = 10000.0

HG_HEADS = 8
HG_D = 128
HG_CHUNK = 64
HG_SUB = 16
MLA_HEADS = 8
MLA_RANK = 512
MLA_NOPE = 128
MLA_ROPE = 64
MLA_V = 128
MLA_QK_PAD = 256
MEM_HEADS = 4
MEM_HEAD_DIM = 256
N_EXPERTS = 8
TOP_K = 2

LOG2E = 1.4426950408889634

LANES = 128
SUBLANES = 8
ATTN_BLOCK = 512
ATTN_Q_ROWS = 512
FFN_TM = 256
FFN_RB = 2048
FFN_TF = 512
ROW_DMA_UNROLL = 8
VMEM_MB = 48
VMEM_FFN_MB = 62


def _cparams(n_axes, vmem_mb=None):
    kw = dict(dimension_semantics=("arbitrary",) * n_axes)
    if vmem_mb is not None:
        kw["vmem_limit_bytes"] = vmem_mb * 1024 * 1024
    return pltpu.CompilerParams(**kw)


def _sigmoid(x):
    return 1.0 / (1.0 + jnp.exp(-x))


def _dot(a, b):
    return jnp.dot(a, b, preferred_element_type=F32)


def _dot_nt(a, b):
    return lax.dot_general(a, b, (((1,), (1,)), ((), ())), preferred_element_type=F32)


def _rope_kernel(pos_ref, c_ref, cos_ref, sin_ref):
    ang = pos_ref[...].astype(F32) * c_ref[0:1, :]
    cos_ref[...] = jnp.cos(ang) * c_ref[1:2, :]
    sin_ref[...] = jnp.sin(ang) * c_ref[2:3, :]


def rope_tables(positions, tm=1024):
    T = positions.size
    half = MLA_ROPE // 2
    inv = ROPE_THETA ** (-(np.arange(half, dtype=np.float32) / half))
    consts = np.zeros((8, LANES), np.float32)
    consts[0, :half] = inv
    consts[0, half:2 * half] = inv
    consts[1, :2 * half] = 1.0
    consts[2, :half] = -1.0
    consts[2, half:2 * half] = 1.0
    tm = min(tm, T)
    out = jax.ShapeDtypeStruct((T, LANES), F32)
    return pl.pallas_call(
        _rope_kernel,
        grid=(T // tm,),
        in_specs=[pl.BlockSpec((tm, 1), lambda i: (i, 0)),
                  pl.BlockSpec((8, LANES), lambda i: (0, 0))],
        out_specs=[pl.BlockSpec((tm, LANES), lambda i: (i, 0)),
                   pl.BlockSpec((tm, LANES), lambda i: (i, 0))],
        out_shape=[out, out],
        compiler_params=_cparams(1),
        name="rope_tables",
    )(positions.reshape(T, 1), jnp.asarray(consts))


def _rmsnorm_kernel(x_ref, g_ref, o_ref):
    x = x_ref[...].astype(F32)
    y = x * lax.rsqrt(jnp.mean(x * x, axis=-1, keepdims=True) + EPS)
    o_ref[...] = (y * g_ref[...]).astype(o_ref.dtype)


def rmsnorm_cast(x, g, out_dtype, tm=512):
    M, D = x.shape
    tm = min(tm, M)
    return pl.pallas_call(
        _rmsnorm_kernel,
        grid=(M // tm,),
        in_specs=[pl.BlockSpec((tm, D), lambda i: (i, 0)),
                  pl.BlockSpec((1, D), lambda i: (0, 0))],
        out_specs=pl.BlockSpec((tm, D), lambda i: (i, 0)),
        out_shape=jax.ShapeDtypeStruct((M, D), out_dtype),
        compiler_params=_cparams(1),
        name="rmsnorm",
    )(x, g.reshape(1, D).astype(F32))


def _mm_kernel(x_ref, w_ref, o_ref, *scratch, cast):
    if cast:
        (wb,) = scratch

        @pl.when(pl.program_id(1) == 0)
        def _():
            wb[...] = w_ref[...].astype(BF16)

        w = wb[...]
    else:
        w = w_ref[...]
    o_ref[...] = _dot(x_ref[...], w).astype(o_ref.dtype)


def matmul(x, w, *, n_cols, tn, tm, out_dtype, layer=None, name="matmul"):
    M, K = x.shape
    tm = min(tm, M)
    cast = w.dtype != BF16
    if layer is None:
        w_spec = pl.BlockSpec((K, tn), lambda j, i: (0, j))
    else:
        w_spec = pl.BlockSpec((None, K, tn), lambda j, i: (layer, 0, j))
    scratch = [pltpu.VMEM((K, tn), BF16)] if cast else []
    return pl.pallas_call(
        functools.partial(_mm_kernel, cast=cast),
        grid=(n_cols // tn, M // tm),
        in_specs=[pl.BlockSpec((tm, K), lambda j, i: (i, 0)), w_spec],
        out_specs=pl.BlockSpec((tm, tn), lambda j, i: (i, j)),
        out_shape=jax.ShapeDtypeStruct((M, n_cols), out_dtype),
        scratch_shapes=scratch,
        compiler_params=_cparams(2, VMEM_MB),
        name=name,
    )(x, w)


def _mm_wt_kernel(x_ref, wt_ref, o_ref, wb):
    @pl.when(pl.program_id(1) == 0)
    def _():
        wb[...] = wt_ref[...].T.astype(BF16)

    o_ref[...] = _dot(x_ref[...], wb[...]).astype(o_ref.dtype)


def matmul_wt(x, wt, *, layer, row0, n_cols, tn, tm, out_dtype, name):
    M, K = x.shape
    tm = min(tm, M)
    assert n_cols % tn == 0
    if row0 % tn == 0:
        w_spec = pl.BlockSpec((None, tn, K), lambda j, i: (layer, row0 // tn + j, 0))
    else:
        w_spec = pl.BlockSpec((pl.Squeezed(), pl.Element(tn), pl.Element(K)),
                              lambda j, i: (layer, pl.multiple_of(row0 + j * tn, 8), 0))
    return pl.pallas_call(
        _mm_wt_kernel,
        grid=(n_cols // tn, M // tm),
        in_specs=[pl.BlockSpec((tm, K), lambda j, i: (i, 0)), w_spec],
        out_specs=pl.BlockSpec((tm, tn), lambda j, i: (i, j)),
        out_shape=jax.ShapeDtypeStruct((M, n_cols), out_dtype),
        scratch_shapes=[pltpu.VMEM((K, tn), BF16)],
        compiler_params=_cparams(2, VMEM_MB),
        name=name,
    )(x, wt)


def _rope_key_weight_kernel(wt_ref, o_ref):
    half = MLA_ROPE // 2
    w = wt_ref[...]
    zero = jnp.zeros((LANES - MLA_ROPE, w.shape[1]), F32)
    rows = jnp.concatenate([w, zero, w[half:], w[:half], zero], axis=0)
    o_ref[...] = rows.T.astype(BF16)


def rope_key_weight(wt, layer, row0):
    K = wt.shape[2]
    assert row0 % MLA_ROPE == 0
    return pl.pallas_call(
        _rope_key_weight_kernel,
        grid=(1,),
        in_specs=[pl.BlockSpec((None, MLA_ROPE, K), lambda i: (layer, row0 // MLA_ROPE, 0))],
        out_specs=pl.BlockSpec((K, 2 * LANES), lambda i: (0, 0)),
        out_shape=jax.ShapeDtypeStruct((K, 2 * LANES), BF16),
        compiler_params=_cparams(1),
        name="rope_key_weight",
    )(wt)


def _hgrn_kernel(q_ref, f_ref, i_ref, g_ref, lb_ref, gn_ref, tri_ref, ones_ref, dmask_ref, o_ref,
                 st_ref, bk_scr):
    L, SUB, D, G = HG_CHUNK, HG_SUB, HG_D, SUBLANES

    @pl.when(pl.program_id(1) == 0)
    def _():
        st_ref[...] = jnp.zeros_like(st_ref)

    lb = lb_ref[...]
    f = lb + (1.0 - lb) * _sigmoid(f_ref[...])
    lf = jnp.log(jnp.maximum(f, MIN_FORGET)) * LOG2E
    kk = 1.0 - f
    tri = tri_ref[...]
    p1 = lf.astype(BF16)
    r1 = lf - p1.astype(F32)
    p2 = r1.astype(BF16)
    p3 = (r1 - p2.astype(F32)).astype(BF16)
    b = _dot(tri, p1) + _dot(tri, p2) + _dot(tri, p3)

    q = q_ref[...]
    v = i_ref[...]
    g = g_ref[...]
    gn = gn_ref[...]
    ones = ones_ref[...]
    bk = b - jnp.log2(jnp.maximum(kk, 0.0))
    bk_scr[...] = bk
    b_last = b[L - 1:L, :]
    q_in = (q * jnp.exp2(b)).astype(BF16)
    k_dec = jnp.exp2(b_last - bk).astype(BF16)
    e_last = jnp.exp2(b_last)

    def first_matmuls(h):
        hs = slice(h * D, (h + 1) * D)
        bh, qh, bkh, vh = b[:, hs], q[:, hs], bk[:, hs], v[:, hs]
        vb = vh.astype(BF16)
        st = st_ref[h]
        inter = _dot_nt(q_in[:, hs], st.astype(BF16))
        st_ref[h] = e_last[:, hs] * st + lax.dot_general(
            vb, k_dec[:, hs], (((0,), (0,)), ((), ())), preferred_element_type=F32)
        ssums, off_parts = [], []
        for i in range(L // SUB):
            r0 = i * SUB
            bs, qs = bh[r0:r0 + SUB], qh[r0:r0 + SUB]
            terms = []
            for s in range(SUB):
                bk_s = bk_scr[r0 + s:r0 + s + 1, hs]
                for grp in range(s // G, SUB // G):
                    rows = slice(grp * G, (grp + 1) * G)
                    d = bs[rows] - bk_s
                    if grp == s // G and s % G:
                        d = d + dmask_ref[s % G]
                    terms.append(qs[rows] * jnp.exp2(d))
            ssums.append(_dot(jnp.concatenate(terms, axis=0).astype(BF16), ones))
            if i == 0:
                off_parts.append(jnp.zeros((SUB, L), F32))
            else:
                ref_row = bh[r0 - 1:r0]
                q_i = (qs * jnp.exp2(bs - ref_row)).astype(BF16)
                k_i = jnp.concatenate([jnp.exp2(ref_row - bkh[0:r0]), jnp.zeros((L - r0, D), F32)],
                                      axis=0).astype(BF16)
                off_parts.append(_dot_nt(q_i, k_i))
        return inter, ssums, off_parts, vh, vb

    def finish(h, inter, ssums, off_parts, vh, vb):
        hs = slice(h * D, (h + 1) * D)
        diag_parts = []
        for i in range(L // SUB):
            acc = [None] * (SUB // G)
            t = 0
            for s in range(SUB):
                v_s = i_ref[i * SUB + s:i * SUB + s + 1, hs]
                for grp in range(s // G, SUB // G):
                    term = ssums[i][t * G:(t + 1) * G] * v_s
                    acc[grp] = term if acc[grp] is None else acc[grp] + term
                    t += 1
            diag_parts.extend(acc)
        s_off = jnp.concatenate(off_parts, axis=0).astype(BF16)
        o = inter + jnp.concatenate(diag_parts, axis=0) + _dot(s_off, vb)
        on = o * lax.rsqrt(jnp.mean(o * o, axis=-1, keepdims=True) + EPS) * gn[:, hs]
        gh = g[:, hs]
        o_ref[:, hs] = (on * (gh * _sigmoid(gh))).astype(o_ref.dtype)

    pending = None
    for h in range(HG_HEADS):
        cur = first_matmuls(h)
        if pending is not None:
            finish(h - 1, *pending)
        pending = cur
    finish(HG_HEADS - 1, *pending)


def hgrn_mix(z1, lb, hg_norm, batch, seq):
    T = batch * seq
    W = HG_HEADS * HG_D
    L = HG_CHUNK
    nc = seq // L
    tri = jnp.asarray(np.tril(np.ones((L, L), np.float32)), BF16)
    ones = jnp.ones((HG_D, HG_D), BF16)
    G = SUBLANES
    dmask_np = np.where(np.arange(G)[None, :, None] >= np.arange(G)[:, None, None], 0.0, NEG_BIG)
    dmask = jnp.asarray(np.broadcast_to(dmask_np, (G, G, HG_D)).astype(np.float32))

    def col(c):
        return pl.BlockSpec((L, W), lambda b, j, c=c: (b * nc + j, c))

    const = lambda shape: pl.BlockSpec(shape, lambda b, j: (0,) * len(shape))
    return pl.pallas_call(
        _hgrn_kernel,
        grid=(batch, nc),
        in_specs=[col(0), col(1), col(2), col(3), const((1, W)), const((1, W)),
                  const((L, L)), const((HG_D, HG_D)), const((G, G, HG_D))],
        out_specs=pl.BlockSpec((L, W), lambda b, j: (b * nc + j, 0)),
        out_shape=jax.ShapeDtypeStruct((T, W), BF16),
        scratch_shapes=[pltpu.VMEM((HG_HEADS, HG_D, HG_D), F32), pltpu.VMEM((L, W), F32)],
        compiler_params=_cparams(2, VMEM_MB),
        name="hgrn2",
    )(z1, z1, z1, z1, lb.reshape(1, W).astype(F32),
      jnp.tile(hg_norm.astype(F32), HG_HEADS).reshape(1, W), tri, ones, dmask)


def _mla_prep_kernel(cq_ref, ckv_ref, a_ref, cos_ref, sin_ref, gq_ref, gkv_ref,
                     wq1_ref, wq2_ref, wkv_ref, wkr_ref, q_ref, kt_ref, v_ref):
    scale = (MLA_NOPE + MLA_ROPE) ** -0.5 * LOG2E
    cos = cos_ref[...]
    sin = sin_ref[...]

    def norm(x, g):
        return (x * lax.rsqrt(jnp.mean(x * x, axis=-1, keepdims=True) + EPS) * g).astype(BF16)

    cn = norm(cq_ref[...], gq_ref[...])
    q1 = _dot(cn, wq1_ref[...])
    q2 = _dot(cn, wq2_ref[...])
    cvn = norm(ckv_ref[...], gkv_ref[...])
    kv = _dot(cvn, wkv_ref[...])
    kr = _dot(a_ref[...], wkr_ref[...])
    kr_rot_t = (kr[:, 0:LANES] * cos + kr[:, LANES:2 * LANES] * sin).T.astype(BF16)
    P = MLA_QK_PAD
    for h in range(MLA_HEADS):
        q_ref[:, h * P:h * P + LANES] = (q1[:, h * P:h * P + LANES] * scale).astype(BF16)
        q_ref[:, h * P + LANES:(h + 1) * P] = (
            (q1[:, h * P + LANES:(h + 1) * P] * cos + q2[:, h * LANES:(h + 1) * LANES] * sin) * scale
        ).astype(BF16)
        kt_ref[h * P:h * P + LANES, :] = kv[:, h * LANES:(h + 1) * LANES].T.astype(BF16)
        kt_ref[h * P + LANES:(h + 1) * P, :] = kr_rot_t
    nv = MLA_HEADS * MLA_V
    v_ref[...] = kv[:, nv:2 * nv].astype(BF16)


def mla_prep(z1, a, cos_t, sin_t, gq, gkv, wq1, wq2, wkv, wkr, tm):
    T = z1.shape[0]
    R = MLA_RANK
    cq_blk = (4 * HG_HEADS * HG_D) // R
    row = lambda w: pl.BlockSpec((tm, w), lambda i: (i, 0))
    const = lambda a: pl.BlockSpec(a.shape, lambda i: (0, 0))
    qk_w = MLA_HEADS * MLA_QK_PAD
    return pl.pallas_call(
        _mla_prep_kernel,
        grid=(T // tm,),
        in_specs=[pl.BlockSpec((tm, R), lambda i: (i, cq_blk)),
                  pl.BlockSpec((tm, R), lambda i: (i, cq_blk + 1)),
                  row(a.shape[1]), row(LANES), row(LANES),
                  pl.BlockSpec((1, R), lambda i: (0, 0)), pl.BlockSpec((1, R), lambda i: (0, 0)),
                  const(wq1), const(wq2), const(wkv), const(wkr)],
        out_specs=[row(qk_w), pl.BlockSpec((None, qk_w, tm), lambda i: (i, 0, 0)), row(MLA_HEADS * MLA_V)],
        out_shape=[jax.ShapeDtypeStruct((T, qk_w), BF16), jax.ShapeDtypeStruct((T // tm, qk_w, tm), BF16),
                   jax.ShapeDtypeStruct((T, MLA_HEADS * MLA_V), BF16)],
        compiler_params=_cparams(1, VMEM_MB),
        name="mla_prep",
    )(z1, z1, a, cos_t, sin_t, gq.reshape(1, R).astype(F32), gkv.reshape(1, R).astype(F32),
      wq1, wq2, wkv, wkr)


def mla_weights(w_q_up, w_kv_up):
    R, half = MLA_RANK, MLA_ROPE // 2
    wq = w_q_up.reshape(R, MLA_HEADS, MLA_NOPE + MLA_ROPE)
    nope, rope = wq[..., :MLA_NOPE], wq[..., MLA_NOPE:]
    rope_sw = jnp.concatenate([rope[..., half:], rope[..., :half]], axis=-1)
    z64 = jnp.zeros((R, MLA_HEADS, MLA_QK_PAD - MLA_NOPE - MLA_ROPE), w_q_up.dtype)
    wq1 = jnp.concatenate([nope, rope, z64], axis=-1).reshape(R, MLA_HEADS * MLA_QK_PAD)
    wq2 = jnp.concatenate([rope_sw, z64], axis=-1).reshape(R, MLA_HEADS * LANES)
    wkv = w_kv_up.reshape(R, MLA_HEADS, MLA_NOPE + MLA_V)
    wkv = jnp.concatenate([wkv[..., :MLA_NOPE].reshape(R, -1), wkv[..., MLA_NOPE:].reshape(R, -1)], axis=-1)
    return wq1.astype(BF16), wq2.astype(BF16), wkv.astype(BF16)


def _causal_attn_kernel(q_ref, kt_ref, v_ref, o_ref, m_ref, l_ref, acc_ref, *, tq, tk):
    i = pl.program_id(1)
    H, P, DV = MLA_HEADS, MLA_QK_PAD, MLA_V
    n_c = tk // LANES
    n_diag = tq // tk
    m_ref[...] = jnp.full_like(m_ref, NEG_BIG)
    l_ref[...] = jnp.zeros_like(l_ref)
    acc_ref[...] = jnp.zeros_like(acc_ref)

    def scores(j, h):
        return _dot(q_ref[:, h * P:(h + 1) * P], kt_ref[j, h * P:(h + 1) * P, :])

    def softmax(h, s, col_off):
        if col_off is not None:
            rows = lax.broadcasted_iota(jnp.int32, (tq, tk), 0)
            cols = lax.broadcasted_iota(jnp.int32, (tq, tk), 1) + col_off
            s = jnp.where(cols <= rows, s, NEG_BIG)
        chunks = [s[:, c * LANES:(c + 1) * LANES] for c in range(n_c)]
        m_old = m_ref[h]
        m_new = jnp.maximum(m_old, jnp.max(functools.reduce(jnp.maximum, chunks), axis=-1, keepdims=True))
        ps = [jnp.exp2(c - m_new) for c in chunks]
        alpha = jnp.exp2(m_old - m_new)
        l_ref[h] = alpha * l_ref[h] + jnp.sum(functools.reduce(jnp.add, ps), axis=-1, keepdims=True)
        m_ref[h] = m_new
        return jnp.concatenate([c.astype(BF16) for c in ps], axis=1), alpha

    def values(j, h, p, alpha):
        start = pl.multiple_of(j * tk, tk)
        acc_ref[h] = alpha * acc_ref[h] + _dot(p, v_ref[pl.ds(start, tk), h * DV:(h + 1) * DV])

    def block(j, col_off):
        s, pa = {}, {}
        for step in range(H + 2):
            if step < H:
                s[step] = scores(j, step)
            if 1 <= step <= H:
                pa[step - 1] = softmax(step - 1, s.pop(step - 1), col_off)
            if step >= 2:
                values(j, step - 2, *pa.pop(step - 2))

    def body(j, carry):
        block(j, None)
        return carry

    lax.fori_loop(0, i * n_diag, body, 0)
    for d in range(n_diag):
        block(i * n_diag + d, d * tk)
    for h in range(H):
        o_ref[:, h * DV:(h + 1) * DV] = (acc_ref[h] / l_ref[h]).astype(o_ref.dtype)


def causal_attention(qp, kt, v, batch, seq, tq, tk):
    T = batch * seq
    nq, nk = seq // tq, seq // tk
    H, P, DV = MLA_HEADS, MLA_QK_PAD, MLA_V
    return pl.pallas_call(
        functools.partial(_causal_attn_kernel, tq=tq, tk=tk),
        grid=(batch, nq),
        in_specs=[pl.BlockSpec((tq, H * P), lambda b, i: (b * nq + i, 0)),
                  pl.BlockSpec((nk, H * P, tk), lambda b, i: (b, 0, 0)),
                  pl.BlockSpec((seq, H * DV), lambda b, i: (b, 0))],
        out_specs=pl.BlockSpec((tq, H * DV), lambda b, i: (b * nq + i, 0)),
        out_shape=jax.ShapeDtypeStruct((T, H * DV), BF16),
        scratch_shapes=[pltpu.VMEM((H, tq, LANES), F32), pltpu.VMEM((H, tq, LANES), F32),
                        pltpu.VMEM((H, tq, DV), F32)],
        compiler_params=_cparams(2, VMEM_MB),
        name="mla_attention",
    )(qp, kt, v)


def _mem_attn_kernel(q_ref, k_ref, v_ref, o_ref):
    scale = MEM_HEAD_DIM ** -0.5
    Dh = MEM_HEAD_DIM
    for h in range(MEM_HEADS):
        hs = slice(h * Dh, (h + 1) * Dh)
        s = _dot_nt(q_ref[:, hs], k_ref[:, hs]) * scale
        m = jnp.max(s, axis=-1, keepdims=True)
        p = jnp.exp(s - m)
        l = jnp.sum(p, axis=-1, keepdims=True)
        o = _dot(p.astype(BF16), v_ref[:, hs])
        o_ref[:, hs] = (o / l).astype(o_ref.dtype)


def memory_attention(z2, mem_kv, batch, seq, n_mem, tq=512):
    T = batch * seq
    tq = min(tq, seq)
    nq = seq // tq
    W = MEM_HEADS * MEM_HEAD_DIM
    return pl.pallas_call(
        _mem_attn_kernel,
        grid=(batch, nq),
        in_specs=[pl.BlockSpec((tq, W), lambda b, i: (b * nq + i, 0)),
                  pl.BlockSpec((n_mem, W), lambda b, i: (b, 0)),
                  pl.BlockSpec((n_mem, W), lambda b, i: (b, 1))],
        out_specs=pl.BlockSpec((tq, W), lambda b, i: (b * nq + i, 0)),
        out_shape=jax.ShapeDtypeStruct((T, W), BF16),
        compiler_params=_cparams(2, VMEM_MB),
        name="mem_attention",
    )(z2, mem_kv, mem_kv)


def _merge_kernel(y0_ref, y1_ref, y2_ref, g0_ref, g1_ref, g2_ref, w_ref, o_ref):
    acc = None
    for n, (y_ref, g_ref) in enumerate(((y0_ref, g0_ref), (y1_ref, g1_ref), (y2_ref, g2_ref))):
        term = _sigmoid(g_ref[...].astype(F32)) * _dot(y_ref[...], w_ref[n])
        acc = term if acc is None else acc + term
    o_ref[...] = acc.astype(o_ref.dtype)


def branch_merge(y_hg, y_mla, y_mem, z2, w_branch, layer, tm=1024, tn=512):
    T, W = y_hg.shape
    D = w_branch.shape[-1]
    tm = min(tm, T)
    g_off = W // tn

    def gate(n):
        return pl.BlockSpec((tm, tn), lambda i, j, n=n: (i, g_off + n * (D // tn) + j))

    y_spec = pl.BlockSpec((tm, W), lambda i, j: (i, 0))
    return pl.pallas_call(
        _merge_kernel,
        grid=(T // tm, D // tn),
        in_specs=[y_spec, y_spec, y_spec, gate(0), gate(1), gate(2),
                  pl.BlockSpec((None, 3, W, tn), lambda i, j: (layer, 0, 0, j))],
        out_specs=pl.BlockSpec((tm, tn), lambda i, j: (i, j)),
        out_shape=jax.ShapeDtypeStruct((T, D), BF16),
        compiler_params=_cparams(2, VMEM_MB),
        name="branch_merge",
    )(y_hg, y_mla, y_mem, z2, z2, z2, w_branch)


def _top2_route(logits):
    lane = lax.broadcasted_iota(jnp.int32, logits.shape, 1)
    lg = jnp.where(lane < N_EXPERTS, logits, NEG_BIG)
    m1 = jnp.max(lg, axis=-1, keepdims=True)
    i1 = jnp.min(jnp.where(lg == m1, lane, LANES), axis=-1, keepdims=True)
    lg2 = jnp.where(lane == i1, NEG_BIG, lg)
    m2 = jnp.max(lg2, axis=-1, keepdims=True)
    i2 = jnp.min(jnp.where(lg2 == m2, lane, LANES), axis=-1, keepdims=True)
    e2 = jnp.exp(m2 - m1)
    w1 = 1.0 / (1.0 + e2)
    w2 = e2 / (1.0 + e2)
    return jnp.where(lane == 0, i1.astype(F32),
                     jnp.where(lane == 1, i2.astype(F32),
                               jnp.where(lane == 2, w1, jnp.where(lane == 3, w2, 0.0))))


def _proj_res_norm_kernel(x_ref, w_ref, h_ref, g_ref, *rest, route):
    if route:
        wr_ref, hn_ref, a_ref, rw_ref = rest
    else:
        hn_ref, a_ref = rest
    hn = h_ref[...] + _dot(x_ref[...], w_ref[...])
    hn_ref[...] = hn
    a = hn * lax.rsqrt(jnp.mean(hn * hn, axis=-1, keepdims=True) + EPS) * g_ref[...]
    a_ref[...] = a.astype(a_ref.dtype)
    if route:
        rw_ref[...] = _top2_route(_dot(a.astype(BF16), wr_ref[...]))


def proj_residual_norm(x, w, layer, h, gain, a_dtype, w_router=None, tm=512):
    T, K = x.shape
    D = w.shape[2]
    tm = min(tm, T)
    route = w_router is not None
    row = lambda width: pl.BlockSpec((tm, width), lambda i: (i, 0))
    const = lambda r, c: pl.BlockSpec((r, c), lambda i: (0, 0))
    in_specs = [row(K), pl.BlockSpec((None, K, D), lambda i: (layer, 0, 0)), row(D), const(1, D)]
    out_specs = [row(D), row(D)]
    out_shape = [jax.ShapeDtypeStruct((T, D), F32), jax.ShapeDtypeStruct((T, D), a_dtype)]
    args = [x, w, h, gain.reshape(1, D).astype(F32)]
    if route:
        in_specs.append(const(D, LANES))
        out_specs.append(row(LANES))
        out_shape.append(jax.ShapeDtypeStruct((T, LANES), F32))
        args.append(jnp.zeros((D, LANES), BF16).at[:, :N_EXPERTS].set(w_router.astype(BF16)))
    return pl.pallas_call(
        functools.partial(_proj_res_norm_kernel, route=route),
        grid=(T // tm,),
        in_specs=in_specs,
        out_specs=out_specs,
        out_shape=out_shape,
        compiler_params=_cparams(1, VMEM_MB),
        name="proj_residual_norm",
    )(*args)


def _ffn_kernel(row0_ref, ex_ref, work_ref, nblk_ref, tail_ref, x_ref, wg_ref, wu_ref, wd_ref, *rest,
                n_f, residual):
    if residual:
        h_ref, g_ref, o_ref, a_ref, acc, wgb, wub, wdb, sem, abuf, sem_a = rest
    else:
        o_ref, acc, wgb, wub, wdb, sem = rest
    v = pl.program_id(0)
    j = pl.program_id(1)
    rlo = 0
    rhi = nblk_ref[v]

    last_j = j == n_f - 1
    row0 = pl.multiple_of(row0_ref[v], FFN_TM)

    def acc_rows(t):
        return acc.at[pl.ds(pl.multiple_of(t * FFN_TM, FFN_TM), FFN_TM), :]

    def hbm_rows(ref, t):
        return ref.at[pl.ds(pl.multiple_of(row0 + t * FFN_TM, FFN_TM), FFN_TM), :]

    def out_copy(t):
        return pltpu.make_async_copy(acc_rows(t), hbm_rows(o_ref, t), sem)

    def a_copy(t):
        return pltpu.make_async_copy(abuf, hbm_rows(a_ref, t), sem_a)

    def run_copies(make, n, start=True):
        def go(t, carry):
            make(t).start()
            return carry

        def wait(t, carry):
            make(t).wait()
            return carry

        if start:
            lax.fori_loop(0, n, go, 0)
        lax.fori_loop(0, n, wait, 0)

    @pl.when(jnp.logical_and(rhi > 0, j == 0))
    def _():
        if residual:
            run_copies(lambda t: pltpu.make_async_copy(hbm_rows(h_ref, t), acc_rows(t), sem), rhi)
        else:
            acc[...] = jnp.zeros_like(acc)

    def row_blocks(r, n):
        rows = [pl.ds(pl.multiple_of((r + t) * FFN_TM, FFN_TM), FFN_TM) for t in range(n)]
        hid = []
        for t in range(n):
            xb = x_ref[rows[t], :].astype(BF16)
            gate = _dot(xb, wgb[...])
            up = _dot(xb, wub[...])
            hid.append((gate * _sigmoid(gate) * up).astype(BF16))
        for t in range(n):
            acc[rows[t], :] += _dot(hid[t], wdb[...])

        @pl.when(last_j)
        def _():
            for t in range(n):
                out_copy(r + t).start()
                if residual:
                    blk = r + t
                    hn = acc[rows[t], :]
                    a = hn * lax.rsqrt(jnp.mean(hn * hn, axis=-1, keepdims=True) + EPS) * g_ref[...]

                    @pl.when(blk >= 1)
                    def _():
                        a_copy(blk - 1).wait()

                    abuf[...] = a.astype(abuf.dtype)
                    a_copy(blk).start()

    @pl.when(rhi > rlo)
    def _():
        wgb[...] = wg_ref[...].astype(BF16)
        wub[...] = wu_ref[...].astype(BF16)
        wdb[...] = wd_ref[...].astype(BF16)
        row_blocks(rlo, 1)
        rest = rhi - rlo - 1

        def pair(t, carry):
            row_blocks(rlo + 1 + 2 * t, 2)
            return carry

        lax.fori_loop(0, lax.shift_right_logical(rest, 1), pair, 0)

        @pl.when((rest & 1) == 1)
        def _():
            row_blocks(rhi - 1, 1)

    def zero_copy(t):
        dst = pl.multiple_of((tail_ref[0] + t) * FFN_TM, FFN_TM)
        return pltpu.make_async_copy(acc.at[pl.ds(0, FFN_TM), :], o_ref.at[pl.ds(dst, FFN_TM), :], sem)

    @pl.when(jnp.logical_and(rhi > 0, last_j))
    def _():
        run_copies(out_copy, rhi, start=False)
        if residual:
            a_copy(rhi - 1).wait()

    if not residual:
        @pl.when(jnp.logical_and(v == pl.num_programs(0) - 1, last_j))
        def _():
            acc[0:FFN_TM, :] = jnp.zeros((FFN_TM, acc.shape[1]), F32)
            run_copies(zero_copy, tail_ref[1])


def ffn_schedule(pad_start, pad_end, n_rows):
    E = pad_start.shape[0]
    kmax = n_rows // FFN_RB
    nv = min(kmax + E, kmax * E)
    nb = (pad_end - pad_start) // FFN_TM
    n_win = (nb + FFN_RB // FFN_TM - 1) // (FFN_RB // FFN_TM)
    per_win = (nb + jnp.maximum(n_win, 1) - 1) // jnp.maximum(n_win, 1)
    k = jnp.arange(kmax, dtype=jnp.int32)[None, :]
    start = pad_start[:, None] + k * per_win[:, None] * FFN_TM
    flat = (k < n_win[:, None]).reshape(-1)
    nblk = jnp.clip(nb[:, None] - k * per_win[:, None], 0, per_win[:, None]).reshape(-1)
    order = jnp.argsort(jnp.logical_not(flat), stable=True)[:nv].astype(jnp.int32)
    n_valid = jnp.sum(flat.astype(jnp.int32))
    valid = jnp.arange(nv, dtype=jnp.int32) < n_valid
    order = jnp.where(valid, order, order[jnp.maximum(n_valid - 1, 0)])
    row0 = start.reshape(-1)[order].astype(jnp.int32)
    ex = (order // kmax).astype(jnp.int32)
    return row0, ex, valid.astype(jnp.int32), jnp.where(valid, nblk[order], 0).astype(jnp.int32)


def grouped_ffn(x, w_gu, w_down, sched, n_out, rows_end, residual=None):
    R, D = x.shape
    E, _, F2 = w_gu.shape
    F = F2 // 2
    n_f = F // FFN_TF
    row0, ex, work, nblk = sched
    nv = row0.shape[0]
    tail = jnp.stack([rows_end // FFN_TM, (n_out - rows_end) // FFN_TM]).astype(jnp.int32)

    def jf(j, work_ref, v):
        return jnp.where(work_ref[v] == 1, j, n_f - 1)

    anywhere = pl.BlockSpec(memory_space=pl.ANY)
    in_specs = [
        pl.BlockSpec((pl.Element(FFN_RB), pl.Element(D)),
                     lambda v, j, row0, ex, wk, nb, tl: (pl.multiple_of(row0[v], FFN_TM), 0),
                     pipeline_mode=pl.Buffered(1)),
        pl.BlockSpec((None, D, FFN_TF), lambda v, j, row0, ex, wk, nb, tl: (ex[v], 0, jf(j, wk, v))),
        pl.BlockSpec((None, D, FFN_TF), lambda v, j, row0, ex, wk, nb, tl: (ex[v], 0, n_f + jf(j, wk, v))),
        pl.BlockSpec((None, FFN_TF, D), lambda v, j, row0, ex, wk, nb, tl: (ex[v], jf(j, wk, v), 0)),
    ]
    out_specs = [anywhere]
    out_shape = [jax.ShapeDtypeStruct((n_out, D), F32)]
    scratch = [pltpu.VMEM((FFN_RB, D), F32), pltpu.VMEM((D, FFN_TF), BF16), pltpu.VMEM((D, FFN_TF), BF16),
               pltpu.VMEM((FFN_TF, D), BF16), pltpu.SemaphoreType.DMA(())]
    args = [row0, ex, work, nblk, tail, x, w_gu, w_gu, w_down]
    if residual is not None:
        h, gain, a_dtype = residual
        assert isinstance(rows_end, int) and rows_end == n_out and h.shape == (n_out, D)
        in_specs += [anywhere, pl.BlockSpec((1, D), lambda v, j, row0, ex, wk, nb, tl: (0, 0))]
        out_specs.append(anywhere)
        out_shape.append(jax.ShapeDtypeStruct((n_out, D), a_dtype))
        scratch += [pltpu.VMEM((FFN_TM, D), a_dtype), pltpu.SemaphoreType.DMA(())]
        args += [h, gain.reshape(1, D).astype(F32)]
    grid_spec = pltpu.PrefetchScalarGridSpec(
        num_scalar_prefetch=5,
        grid=(nv, n_f),
        in_specs=in_specs,
        out_specs=out_specs,
        scratch_shapes=scratch,
    )
    out = pl.pallas_call(
        functools.partial(_ffn_kernel, n_f=n_f, residual=residual is not None),
        grid_spec=grid_spec,
        out_shape=out_shape,
        compiler_params=_cparams(2, VMEM_FFN_MB),
        name="grouped_ffn",
    )(*args)
    return out if residual is not None else out[0]


def _gather_kernel(tok_ref, live_ref, a_ref, o_ref, buf, sem, *, tm):
    i = pl.program_id(0)
    n_live = live_ref[0]

    def issue(step, slot):
        base = step * tm

        def body(g, carry):
            for u in range(ROW_DMA_UNROLL):
                r = g * ROW_DMA_UNROLL + u
                pltpu.make_async_copy(a_ref.at[pl.ds(tok_ref[base + r], 1), :],
                                      buf.at[slot, pl.ds(r, 1), :], sem.at[slot]).start()
            return carry

        lax.fori_loop(0, tm // ROW_DMA_UNROLL, body, 0)

    @pl.when(i == 0)
    def _():
        issue(0, 0)

    @pl.when(i + 1 < n_live)
    def _():
        issue(i + 1, (i + 1) & 1)

    @pl.when(i < n_live)
    def _():
        slot = i & 1
        pltpu.make_async_copy(a_ref.at[pl.ds(0, tm), :], buf.at[slot], sem.at[slot]).wait()
        o_ref[...] = buf[slot].astype(o_ref.dtype)

    @pl.when(i >= n_live)
    def _():
        o_ref[...] = jnp.zeros_like(o_ref)


def gather_rows(a, slot_tok, rows_end, tm=512):
    T, D = a.shape
    R = slot_tok.shape[0]
    tm = min(tm, T)
    n_live = jnp.maximum((rows_end + tm - 1) // tm, 1).astype(jnp.int32).reshape(1)
    grid_spec = pltpu.PrefetchScalarGridSpec(
        num_scalar_prefetch=2,
        grid=(R // tm,),
        in_specs=[pl.BlockSpec(memory_space=pl.ANY)],
        out_specs=pl.BlockSpec((tm, D), lambda i, tok, live: (i, 0)),
        scratch_shapes=[pltpu.VMEM((2, tm, D), a.dtype), pltpu.SemaphoreType.DMA((2,))],
    )
    return pl.pallas_call(
        functools.partial(_gather_kernel, tm=tm),
        grid_spec=grid_spec,
        out_shape=jax.ShapeDtypeStruct((R, D), BF16),
        compiler_params=_cparams(1, VMEM_MB),
        name="gather_rows",
    )(slot_tok, n_live, a)


def _combine_kernel(dest_ref, y_ref, rw_ref, h_ref, g_ref, hn_ref, a_ref, buf, sem, *, tm, n_steps):
    i = pl.program_id(0)

    def issue(step, slot):
        base = step * tm

        def body(g, carry):
            for u in range(ROW_DMA_UNROLL // TOP_K):
                r = g * (ROW_DMA_UNROLL // TOP_K) + u
                for k in range(TOP_K):
                    src = dest_ref[(base + r) * TOP_K + k]
                    pltpu.make_async_copy(y_ref.at[pl.ds(src, 1), :], buf.at[slot, k, pl.ds(r, 1), :],
                                          sem.at[slot]).start()
            return carry

        lax.fori_loop(0, tm // (ROW_DMA_UNROLL // TOP_K), body, 0)

    @pl.when(i == 0)
    def _():
        issue(0, 0)

    @pl.when(i + 1 < n_steps)
    def _():
        issue(i + 1, (i + 1) & 1)

    slot = i & 1
    for k in range(TOP_K):
        pltpu.make_async_copy(y_ref.at[pl.ds(0, tm), :], buf.at[slot, k], sem.at[slot]).wait()
    rw = rw_ref[...]
    hn = h_ref[...] + rw[:, 2:3] * buf[slot, 0] + rw[:, 3:4] * buf[slot, 1]
    hn_ref[...] = hn
    a = hn * lax.rsqrt(jnp.mean(hn * hn, axis=-1, keepdims=True) + EPS) * g_ref[...]
    a_ref[...] = a.astype(a_ref.dtype)


def combine_residual_norm(y_sorted, dest, rw, h, gain, a_dtype, tm=256):
    T, D = h.shape
    tm = min(tm, T)
    n_steps = T // tm
    row = pl.BlockSpec((tm, D), lambda i, d: (i, 0))
    grid_spec = pltpu.PrefetchScalarGridSpec(
        num_scalar_prefetch=1,
        grid=(n_steps,),
        in_specs=[pl.BlockSpec(memory_space=pl.ANY),
                  pl.BlockSpec((tm, LANES), lambda i, d: (i, 0)),
                  row,
                  pl.BlockSpec((1, D), lambda i, d: (0, 0))],
        out_specs=[row, row],
        scratch_shapes=[pltpu.VMEM((2, TOP_K, tm, D), F32), pltpu.SemaphoreType.DMA((2,))],
    )
    return pl.pallas_call(
        functools.partial(_combine_kernel, tm=tm, n_steps=n_steps),
        grid_spec=grid_spec,
        out_shape=[jax.ShapeDtypeStruct((T, D), F32), jax.ShapeDtypeStruct((T, D), a_dtype)],
        compiler_params=_cparams(1, VMEM_MB),
        name="combine_residual_norm",
    )(dest, y_sorted, rw, h, gain.reshape(1, D).astype(F32))


def mixer_layer(h, a, l, lb, mem_n, cos_t, sin_t, p, batch, seq, n_mem, next_gain, a_dtype, w_router):
    D = h.shape[1]
    n_a = 4 * HG_HEADS * HG_D + 2 * MLA_RANK
    kr0 = n_a
    q_mem0 = kr0 + MLA_ROPE
    z1 = matmul_wt(a, p["w_in_t"], layer=l, row0=0, n_cols=n_a, tn=1024, tm=1024, out_dtype=F32,
                   name="in_proj_a")
    n_b = MEM_HEADS * MEM_HEAD_DIM + 3 * D
    z2 = matmul_wt(a, p["w_in_t"], layer=l, row0=q_mem0, n_cols=n_b, tn=1024, tm=1024, out_dtype=BF16,
                   name="in_proj_b")
    wq1, wq2, wkv = mla_weights(p["w_q_up"][l], p["w_kv_up"][l])
    wkr = rope_key_weight(p["w_in_t"], l, kr0)

    y_hg = hgrn_mix(z1, lb, p["hg_norm"][l], batch, seq)
    tq = min(ATTN_Q_ROWS, seq)
    tk = min(ATTN_BLOCK, tq)
    qp, kt, v = mla_prep(z1, a, cos_t, sin_t, p["mla_q_norm"][l], p["mla_kv_norm"][l], wq1, wq2, wkv,
                         wkr, tk)
    y_mla = causal_attention(qp, kt, v, batch, seq, tq, tk)
    mem_kv = matmul(mem_n, p["w_mem_kv"], layer=l, n_cols=2 * MEM_HEADS * MEM_HEAD_DIM, tn=512,
                    tm=1024, out_dtype=BF16, name="mem_kv")
    y_mem = memory_attention(z2, mem_kv, batch, seq, n_mem)
    merged = branch_merge(y_hg, y_mla, y_mem, z2, p["w_branch"], l)
    return proj_residual_norm(merged, p["w_o"], l, h, next_gain, a_dtype, w_router)


def group_bounds(starts, ends, index, n_groups, n_rows):
    E = starts.shape[0]
    before = jnp.zeros((index * E,), jnp.int32)
    after = jnp.full(((n_groups - index - 1) * E,), n_rows, jnp.int32)
    return (jnp.concatenate([before, starts.astype(jnp.int32), after]),
            jnp.concatenate([before, ends.astype(jnp.int32), after]))


def moe_routing(rw, n_tokens):
    A = n_tokens * TOP_K
    flat_e = rw[:, :TOP_K].astype(jnp.int32).reshape(A)
    onehot = (flat_e[:, None] == jnp.arange(N_EXPERTS, dtype=jnp.int32)[None, :]).astype(jnp.int32)
    csum = jnp.cumsum(onehot, axis=0)
    rank = jnp.take_along_axis(csum, flat_e[:, None], axis=1)[:, 0] - 1
    counts = csum[-1]
    padded = (counts + FFN_TM - 1) // FFN_TM * FFN_TM
    pad_end = jnp.cumsum(padded).astype(jnp.int32)
    pad_start = pad_end - padded
    dest = (pad_start[flat_e] + rank).astype(jnp.int32)
    n_rows = (A + N_EXPERTS * FFN_TM + FFN_RB - 1) // FFN_RB * FFN_RB + FFN_RB
    filler = jnp.arange(n_rows, dtype=jnp.int32) % n_tokens
    slot_tok = filler.at[dest].set(jnp.arange(A, dtype=jnp.int32) // TOP_K)
    return dest, slot_tok, pad_start, pad_end, n_rows


def kernel(x, mem, positions, hg_lb_logits, attn_norm, w_in, hg_norm, mla_q_norm, w_q_up, mla_kv_norm,
           w_kv_up, mem_norm, w_mem_kv, w_branch, w_o, ffn_norm, w_dense_gu, w_dense_down, w_router,
           w_expert_gu, w_expert_down, final_norm):
    batch, seq, D = x.shape
    n_mem = mem.shape[1]
    depth = w_in.shape[0]
    T = batch * seq
    p = dict(w_in_t=jnp.swapaxes(w_in, 1, 2), hg_norm=hg_norm, mla_q_norm=mla_q_norm, w_q_up=w_q_up, mla_kv_norm=mla_kv_norm,
             w_kv_up=w_kv_up, w_mem_kv=w_mem_kv, w_branch=w_branch.astype(BF16), w_o=w_o.astype(BF16))

    cos_t, sin_t = rope_tables(positions.astype(jnp.int32))
    lb_sm = jax.nn.softmax(hg_lb_logits.astype(F32), axis=0)
    lb_all = jnp.clip(jnp.cumsum(lb_sm, axis=0) - lb_sm[0:1], 0.0, 1.0)
    mem_n = rmsnorm_cast(mem.reshape(batch * n_mem, D), mem_norm, BF16)

    n_dense, n_moe = w_dense_gu.shape[0], w_expert_gu.shape[0]
    w_moe_gu = w_expert_gu.reshape((n_moe * N_EXPERTS,) + w_expert_gu.shape[2:])
    w_moe_down = w_expert_down.reshape((n_moe * N_EXPERTS,) + w_expert_down.shape[2:])

    h = x.reshape(T, D)
    a = rmsnorm_cast(h, attn_norm[0], BF16)
    for l in range(depth):
        moe = l % 2 == 1
        last = l + 1 == depth
        next_gain = final_norm if last else attn_norm[l + 1]
        next_dtype = F32 if last else BF16
        h, a_ffn, *routing = mixer_layer(h, a, l, lb_all[l], mem_n, cos_t, sin_t, p, batch, seq, n_mem,
                                         ffn_norm[l], F32 if moe else BF16,
                                         w_router[l // 2] if moe else None)
        if not moe:
            zero = jnp.zeros((1,), jnp.int32)
            starts, ends = group_bounds(zero, zero + T, l // 2, n_dense, T)
            h, a = grouped_ffn(a_ffn, w_dense_gu, w_dense_down, ffn_schedule(starts, ends, T), T, T,
                               residual=(h, next_gain, next_dtype))
        else:
            (rw,) = routing
            dest, slot_tok, pad_start, pad_end, n_rows = moe_routing(rw, T)
            xs = gather_rows(a_ffn, slot_tok, pad_end[-1])
            starts, ends = group_bounds(pad_start, pad_end, l // 2, n_moe, n_rows)
            ys = grouped_ffn(xs, w_moe_gu, w_moe_down, ffn_schedule(starts, ends, n_rows),
                             n_rows - FFN_RB, pad_end[-1])
            h, a = combine_residual_norm(ys, dest, rw, h, next_gain, next_dtype)
    return a.reshape(batch, seq, D)
```
